```python
import math
import jax
import jax.numpy as jnp
from jax import lax
import numpy as np


D_MODEL = 1024
BATCH = 8
SEQ = 4096
DEPTH = 4
DEC_BATCH = 4
DEC_SEQ = 4096
PAST_LEN = 128

N_EVEN = (DEPTH + 1) // 2
N_ODD = DEPTH // 2

HY_WIDTH = D_MODEL
HY_EMB = 33
HY_FILTER_HIDDEN = 64
HY_SHORT_K = 3
HY_SHORT_DECAY_PCT = 0.3
HY_LONG_DECAY_PCT = 1.5
HY_DECAY_TARGET = 1e-2

SSD_WIDTH = D_MODEL
SSD_HEAD_DIM = 64
SSD_HEADS = SSD_WIDTH // SSD_HEAD_DIM
SSD_GROUPS = 4
SSD_STATE = 128
SSD_CONV_K = 4
SSD_CHUNK = 128
SSD_XBC = SSD_WIDTH + 2 * SSD_GROUPS * SSD_STATE

LRU_WIDTH = 2 * D_MODEL
LRU_HEADS = 16
LRU_BLOCK = LRU_WIDTH // LRU_HEADS
LRU_CONV_K = 4
LRU_C = 8.0

EVEN_IN = 4 * HY_WIDTH + SSD_WIDTH + SSD_XBC + 2 * SSD_HEADS
EVEN_MIX = HY_WIDTH + SSD_WIDTH
ODD_IN = 2 * LRU_WIDTH

kernel_name = 'hybrid_hyena_ssd_rglru_adaln_encoder'


def rms_norm(x, g, eps=1e-6):
    xf = x.astype(jnp.float32)
    y = xf * lax.rsqrt(jnp.mean(xf * xf, axis=-1, keepdims=True) + eps)
    return (y * g.astype(jnp.float32)).astype(x.dtype)


def dw_conv_centred(x, w, b):
    K = w.shape[0]
    L = x.shape[1]
    left = K // 2
    xp = jnp.pad(x, ((0, 0), (left, K - 1 - left), (0, 0)))
    return b + sum(xp[:, k:k + L] * w[k] for k in range(K))


def hyena_filter(L, w1, b1, w2, b2, w3, freq):
    f32 = jnp.float32
    pos = jnp.arange(L, dtype=f32)[:, None]
    t = pos / max(L - 1, 1)
    bands = (HY_EMB - 1) // 2
    f = jnp.linspace(1e-4, bands - 1, bands, dtype=f32)[None, :]
    ang = f * pos * (2.0 * math.pi / L)
    z = jnp.concatenate([t, jnp.cos(ang), -jnp.sin(ang)], axis=-1)
    fr = freq.astype(f32)
    h = jnp.sin(fr * (z @ w1.astype(f32) + b1.astype(f32)))
    h = jnp.sin(fr * (h @ w2.astype(f32) + b2.astype(f32)))
    h = h @ w3.astype(f32)
    deltas = jnp.abs(jnp.linspace(math.log(HY_DECAY_TARGET) / HY_LONG_DECAY_PCT,
                                  math.log(HY_DECAY_TARGET) / HY_SHORT_DECAY_PCT,
                                  HY_WIDTH, dtype=f32))
    window = jnp.exp(-t * deltas)
    h_fwd = h[:, :HY_WIDTH] * window
    h_bwd = h[:, HY_WIDTH:] * window
    filt = jnp.concatenate([h_fwd, jnp.zeros((1, HY_WIDTH), f32), h_bwd[:0:-1]], axis=0)
    return filt / jnp.sum(jnp.abs(filt), axis=0, keepdims=True)


def hyena_branch(u, conv_w, conv_b, fw1, fb1, fw2, fb2, fw3, freq, bias):
    L = u.shape[1]
    uc = dw_conv_centred(u, conv_w, conv_b)
    x0, x1, v = jnp.split(uc, 3, axis=-1)
    filt = hyena_filter(L, fw1, fb1, fw2, fb2, fw3, freq)
    vf = (v * x1).astype(jnp.float32)
    n = 2 * L
    y = jnp.fft.irfft(jnp.fft.rfft(vf, n=n, axis=1) * jnp.fft.rfft(filt, n=n, axis=0)[None],
                      n=n, axis=1)[:, :L]
    y = y + vf * bias.astype(jnp.float32)
    return x0 * y.astype(u.dtype)


def ssd_scan(x, dt, a, bm, cm):
    b, L = x.shape[0], x.shape[1]
    c, q = L // SSD_CHUNK, SSD_CHUNK
    G, R, P, N = SSD_GROUPS, SSD_HEADS // SSD_GROUPS, SSD_HEAD_DIM, SSD_STATE
    xs = (x * dt[..., None]).reshape(b, c, q, G, R, P)
    la = (dt * a).reshape(b, c, q, G, R)
    bc = bm.reshape(b, c, q, G, N)
    cc = cm.reshape(b, c, q, G, N)
    a_cum = jnp.cumsum(la, axis=2)
    diff = a_cum[:, :, :, None] - a_cum[:, :, None, :]
    mask = jnp.tril(jnp.ones((q, q), dtype=bool))[:, :, None, None]
    seg = jnp.exp(jnp.where(mask, diff, -jnp.inf))
    cb = jnp.einsum('bclgn,bcsgn->bclsg', cc, bc)
    y_diag = jnp.einsum('bclsgr,bcsgrp->bclgrp', cb[..., None] * seg, xs)
    decay_states = jnp.exp(a_cum[:, :, -1:] - a_cum)
    states = jnp.einsum('bclgn,bclgrp->bcgrpn', bc, decay_states[..., None] * xs)
    chunk_decay = jnp.exp(a_cum[:, :, -1])

    def step(h, inp):
        s, d = inp
        return h * d[..., None, None] + s, h

    h0 = jnp.zeros((b, G, R, P, N), x.dtype)
    _, prev = lax.scan(step, h0, (jnp.moveaxis(states, 1, 0), jnp.moveaxis(chunk_decay, 1, 0)))
    prev = jnp.moveaxis(prev, 0, 1)
    y_off = jnp.einsum('bclgn,bcgrpn->bclgrp', cc, prev) * jnp.exp(a_cum)[..., None]
    return (y_diag + y_off).reshape(b, L, SSD_HEADS, P)


def ssd_branch(z, xbc, dt_raw, conv_w, conv_b, dt_bias, a_log, d_skip, norm_g):
    f32 = jnp.float32
    b, L, _ = z.shape
    xbc = jax.nn.silu(dw_conv_centred(xbc, conv_w, conv_b)).astype(f32)
    xs, bm, cm = jnp.split(xbc, [SSD_WIDTH, SSD_WIDTH + SSD_GROUPS * SSD_STATE], axis=-1)
    xs = xs.reshape(b, L, SSD_HEADS, SSD_HEAD_DIM)
    bm = bm.reshape(b, L, SSD_GROUPS, SSD_STATE)
    cm = cm.reshape(b, L, SSD_GROUPS, SSD_STATE)
    dt = jax.nn.softplus(dt_raw.astype(f32).reshape(b, L, 2, SSD_HEADS) + dt_bias.astype(f32))
    a = -jnp.exp(a_log.astype(f32))
    fl = lambda t: jnp.flip(t, axis=1)
    y_fwd = ssd_scan(xs, dt[:, :, 0], a[0], bm, cm)
    y_bwd = fl(ssd_scan(fl(xs), fl(dt[:, :, 1]), a[1], fl(bm), fl(cm)))
    y = y_fwd + y_bwd + xs * d_skip.astype(f32)[:, None]
    y = y.reshape(b, L, SSD_WIDTH) * jax.nn.silu(z.astype(f32))
    yg = y.reshape(b, L, SSD_GROUPS, SSD_WIDTH // SSD_GROUPS)
    yg = yg * lax.rsqrt(jnp.mean(yg * yg, axis=-1, keepdims=True) + 1e-5)
    return (yg.reshape(b, L, SSD_WIDTH) * norm_g.astype(f32)).astype(z.dtype)


def rg_lru(x, w_a, b_a, w_x, b_x, lam):
    f32 = jnp.float32
    b, L, W = x.shape
    xh = x.reshape(b, L, LRU_HEADS, LRU_BLOCK)
    r = jax.nn.sigmoid(jnp.einsum('blhi,hij->blhj', xh, w_a.astype(f32)).reshape(b, L, W) + b_a.astype(f32))
    i = jax.nn.sigmoid(jnp.einsum('blhi,hij->blhj', xh, w_x.astype(f32)).reshape(b, L, W) + b_x.astype(f32))
    log_a = LRU_C * r * jax.nn.log_sigmoid(lam.astype(f32))
    a = jnp.exp(log_a)
    u = x * i * jnp.sqrt(-jnp.expm1(2.0 * log_a))

    def step(h, inp):
        a_t, u_t = inp
        h = a_t * h + u_t
        return h, h

    _, hs = lax.scan(step, jnp.zeros((b, W), f32), (jnp.swapaxes(a, 0, 1), jnp.swapaxes(u, 0, 1)))
    return jnp.swapaxes(hs, 0, 1)


def even_mixer(h, p, j):
    u = h @ p['ev_w_in'][j]
    o1 = 3 * HY_WIDTH
    o2 = 4 * HY_WIDTH
    o3 = o2 + SSD_WIDTH
    o4 = o3 + SSD_XBC
    hy_u, hy_gate, z, xbc, dt_raw = jnp.split(u, [o1, o2, o3, o4], axis=-1)
    y_a = hyena_branch(hy_u, p['hy_conv_w'][j], p['hy_conv_b'][j], p['hy_fw1'][j], p['hy_fb1'][j],
                       p['hy_fw2'][j], p['hy_fb2'][j], p['hy_fw3'][j], p['hy_freq'][j],
                       p['hy_bias'][j]) * jax.nn.silu(hy_gate)
    y_b = ssd_branch(z, xbc, dt_raw, p['ssd_conv_w'][j], p['ssd_conv_b'][j], p['ssd_dt_bias'][j],
                     p['ssd_A_log'][j], p['ssd_D'][j], p['ssd_norm_g'][j])
    return jnp.concatenate([y_a, y_b], axis=-1) @ p['ev_w_out'][j]


def odd_mixer(h, p, j):
    u = h @ p['od_w_in'][j]
    xb, gate = jnp.split(u, 2, axis=-1)
    xb = dw_conv_centred(xb, p['lru_conv_w'][j], p['lru_conv_b'][j]).astype(jnp.float32)
    y_fwd = rg_lru(xb, p['lru_w_a'][j, 0], p['lru_b_a'][j, 0], p['lru_w_x'][j, 0],
                   p['lru_b_x'][j, 0], p['lru_lam'][j, 0])
    y_bwd = jnp.flip(rg_lru(jnp.flip(xb, axis=1), p['lru_w_a'][j, 1], p['lru_b_a'][j, 1],
                            p['lru_w_x'][j, 1], p['lru_b_x'][j, 1], p['lru_lam'][j, 1]), axis=1)
    y = (y_fwd + y_bwd).astype(h.dtype) * jax.nn.silu(gate)
    return y @ p['od_w_out'][j]


def trunk(x, c, p):
    cs = jax.nn.silu(c)
    for i in range(DEPTH):
        shift, scale, gate = jnp.split(cs @ p['mod_w'][i] + p['mod_b'][i], 3, axis=-1)
        hn = rms_norm(x, p['norm_g'][i]) * (1.0 + scale[:, None]) + shift[:, None]
        out = even_mixer(hn, p, i // 2) if i % 2 == 0 else odd_mixer(hn, p, i // 2)
        x = x + gate[:, None] * out
    return rms_norm(x, p['final_g'])


def setup_inputs(seed: int = 0) -> dict:
    key = jax.random.key(seed)
    ks = iter(list(jax.random.split(key, 48)))
    f32 = jnp.float32

    def nrm(shape, scale):
        return scale * jax.random.normal(next(ks), shape, f32)

    def unif(shape, lo, hi):
        return jax.random.uniform(next(ks), shape, f32, minval=lo, maxval=hi)

    D = D_MODEL
    x_prompt = nrm((BATCH, SEQ, D), 1.0)
    x_sample = nrm((DEC_BATCH, DEC_SEQ, D), 1.0)
    c_prompt = nrm((BATCH, D), 1.0)
    c_sample = nrm((DEC_BATCH, D), 1.0)
    mod_w = nrm((DEPTH, D, 3 * D), 0.5 * D ** -0.5)
    mod_b = nrm((DEPTH, 3 * D), 0.02)
    norm_g = 1.0 + nrm((DEPTH, D), 0.02)
    final_g = 1.0 + nrm((D,), 0.02)
    ev_w_in = nrm((N_EVEN, D, EVEN_IN), D ** -0.5)
    ev_w_out = nrm((N_EVEN, EVEN_MIX, D), EVEN_MIX ** -0.5)
    hy_conv_w = nrm((N_EVEN, HY_SHORT_K, 3 * HY_WIDTH), HY_SHORT_K ** -0.5)
    hy_conv_b = nrm((N_EVEN, 3 * HY_WIDTH), 0.02)
    hy_fw1 = nrm((N_EVEN, HY_EMB, HY_FILTER_HIDDEN), HY_EMB ** -0.5)
    hy_fb1 = nrm((N_EVEN, HY_FILTER_HIDDEN), 0.02)
    hy_fw2 = nrm((N_EVEN, HY_FILTER_HIDDEN, HY_FILTER_HIDDEN), HY_FILTER_HIDDEN ** -0.5)
    hy_fb2 = nrm((N_EVEN, HY_FILTER_HIDDEN), 0.02)
    hy_fw3 = nrm((N_EVEN, HY_FILTER_HIDDEN, 2 * HY_WIDTH), HY_FILTER_HIDDEN ** -0.5)
    hy_freq = 1.0 + nrm((N_EVEN, HY_FILTER_HIDDEN), 0.1)
    hy_bias = nrm((N_EVEN, HY_WIDTH), 0.1)
    ssd_conv_w = nrm((N_EVEN, SSD_CONV_K, SSD_XBC), SSD_CONV_K ** -0.5)
    ssd_conv_b = nrm((N_EVEN, SSD_XBC), 0.02)
    dt0 = jnp.exp(unif((N_EVEN, 2, SSD_HEADS), math.log(1e-3), math.log(1e-1)))
    ssd_dt_bias = dt0 + jnp.log(-jnp.expm1(-dt0))
    ssd_A_log = jnp.log(unif((N_EVEN, 2, SSD_HEADS), 1.0, 16.0))
    ssd_D = 1.0 + nrm((N_EVEN, SSD_HEADS), 0.1)
    ssd_norm_g = 1.0 + nrm((N_EVEN, SSD_WIDTH), 0.02)
    od_w_in = nrm((N_ODD, D, ODD_IN), D ** -0.5)
    od_w_out = nrm((N_ODD, LRU_WIDTH, D), LRU_WIDTH ** -0.5)
    lru_conv_w = nrm((N_ODD, LRU_CONV_K, LRU_WIDTH), LRU_CONV_K ** -0.5)
    lru_conv_b = nrm((N_ODD, LRU_WIDTH), 0.02)
    lru_w_a = nrm((N_ODD, 2, LRU_HEADS, LRU_BLOCK, LRU_BLOCK), LRU_BLOCK ** -0.5)
    lru_b_a = nrm((N_ODD, 2, LRU_WIDTH), 0.02)
    lru_w_x = nrm((N_ODD, 2, LRU_HEADS, LRU_BLOCK, LRU_BLOCK), LRU_BLOCK ** -0.5)
    lru_b_x = nrm((N_ODD, 2, LRU_WIDTH), 0.02)
    s = unif((N_ODD, 2, LRU_WIDTH), 0.9, 0.999) ** (1.0 / LRU_C)
    lru_lam = jnp.log(s) - jnp.log1p(-s)
    return {'x_prompt': x_prompt, 'x_sample': x_sample, 'c_prompt': c_prompt, 'c_sample': c_sample,
            'mod_w': mod_w, 'mod_b': mod_b, 'norm_g': norm_g, 'final_g': final_g,
            'ev_w_in': ev_w_in, 'ev_w_out': ev_w_out,
            'hy_conv_w': hy_conv_w, 'hy_conv_b': hy_conv_b, 'hy_fw1': hy_fw1, 'hy_fb1': hy_fb1,
            'hy_fw2': hy_fw2, 'hy_fb2': hy_fb2, 'hy_fw3': hy_fw3, 'hy_freq': hy_freq, 'hy_bias': hy_bias,
            'ssd_conv_w': ssd_conv_w, 'ssd_conv_b': ssd_conv_b, 'ssd_dt_bias': ssd_dt_bias,
            'ssd_A_log': ssd_A_log, 'ssd_D': ssd_D, 'ssd_norm_g': ssd_norm_g,
            'od_w_in': od_w_in, 'od_w_out': od_w_out, 'lru_conv_w': lru_conv_w, 'lru_conv_b': lru_conv_b,
            'lru_w_a': lru_w_a, 'lru_b_a': lru_b_a, 'lru_w_x': lru_w_x, 'lru_b_x': lru_b_x,
            'lru_lam': lru_lam}


def reference(x_prompt, x_sample, c_prompt, c_sample, mod_w, mod_b, norm_g, final_g,
              ev_w_in, ev_w_out, hy_conv_w, hy_conv_b, hy_fw1, hy_fb1, hy_fw2, hy_fb2, hy_fw3,
              hy_freq, hy_bias, ssd_conv_w, ssd_conv_b, ssd_dt_bias, ssd_A_log, ssd_D, ssd_norm_g,
              od_w_in, od_w_out, lru_conv_w, lru_conv_b, lru_w_a, lru_b_a, lru_w_x, lru_b_x, lru_lam):
    p = {'mod_w': mod_w, 'mod_b': mod_b, 'norm_g': norm_g, 'final_g': final_g,
         'ev_w_in': ev_w_in, 'ev_w_out': ev_w_out,
         'hy_conv_w': hy_conv_w, 'hy_conv_b': hy_conv_b, 'hy_fw1': hy_fw1, 'hy_fb1': hy_fb1,
         'hy_fw2': hy_fw2, 'hy_fb2': hy_fb2, 'hy_fw3': hy_fw3, 'hy_freq': hy_freq, 'hy_bias': hy_bias,
         'ssd_conv_w': ssd_conv_w, 'ssd_conv_b': ssd_conv_b, 'ssd_dt_bias': ssd_dt_bias,
         'ssd_A_log': ssd_A_log, 'ssd_D': ssd_D, 'ssd_norm_g': ssd_norm_g,
         'od_w_in': od_w_in, 'od_w_out': od_w_out, 'lru_conv_w': lru_conv_w, 'lru_conv_b': lru_conv_b,
         'lru_w_a': lru_w_a, 'lru_b_a': lru_b_a, 'lru_w_x': lru_w_x, 'lru_b_x': lru_b_x,
         'lru_lam': lru_lam}
    y_prompt = trunk(x_prompt, c_prompt, p)
    y_sample = trunk(x_sample, c_sample, p)
    return (y_prompt, y_sample)
```

```python
import functools
import math

import numpy as np
import jax
import jax.numpy as jnp
from jax import lax
from jax.experimental import pallas as pl
from jax.experimental.pallas import tpu as pltpu

F32 = jnp.float32
BF16 = jnp.bfloat16
HI = lax.Precision.HIGHEST

V7X_VMEM_BYTES = 64 * 1024 * 1024
VMEM_LIMIT = V7X_VMEM_BYTES - 8 * 1024 * 1024
LANES = 128
SUBLANES = 8

HY_EMB = 33
HY_SHORT_DECAY_PCT = 0.3
HY_LONG_DECAY_PCT = 1.5
HY_DECAY_TARGET = 1e-2
SSD_HEADS = 16
SSD_HEAD_DIM = 64
SSD_GROUPS = 4
SSD_STATE = 128
SSD_CHUNK = 128
LRU_HEADS = 16
LRU_BLOCK = 128
LRU_C = 8.0

DFT_N1 = 64
DFT_N2 = 128
DFT_K1 = DFT_N1 // 2 + 1
DFT_K1P = 40


def _cparams(*sem):
    return pltpu.CompilerParams(dimension_semantics=sem, vmem_limit_bytes=VMEM_LIMIT)


def _resident(shape, index_map):
    return pl.BlockSpec(shape, index_map, pipeline_mode=pl.Buffered(1))


def _silu(x):
    return x * jax.nn.sigmoid(x)


def _softplus(x):
    return jnp.maximum(x, 0.0) + jnp.log1p(jnp.exp(-jnp.abs(x)))


def _split3(x):
    hi = x.astype(BF16)
    r1 = x - hi.astype(F32)
    mid = r1.astype(BF16)
    lo = (r1 - mid.astype(F32)).astype(BF16)
    return hi, mid, lo


def _dot_sel(x, sel):
    hi, mid, lo = _split3(x)
    d = functools.partial(jnp.dot, preferred_element_type=F32)
    return d(hi, sel) + d(mid, sel) + d(lo, sel)


def _sel_dot(sel, x):
    hi, mid, lo = _split3(x)
    d = functools.partial(jnp.dot, preferred_element_type=F32)
    return d(sel, hi) + d(sel, mid) + d(sel, lo)


def _mod_kernel(c_ref, w_ref, b_ref, o_ref):
    cs = _silu(c_ref[...])
    o_ref[0] = jnp.dot(cs, w_ref[0], precision=HI, preferred_element_type=F32) + b_ref[0]


def _modulation(c_pad, mod_w, mod_b):
    depth, d, d3 = mod_w.shape
    bp = c_pad.shape[0]
    return pl.pallas_call(
        _mod_kernel,
        grid=(depth, d3 // d),
        in_specs=[
            pl.BlockSpec((bp, d), lambda i, j: (0, 0)),
            pl.BlockSpec((1, d, d), lambda i, j: (i, 0, j)),
            pl.BlockSpec((1, 1, d), lambda i, j: (i, 0, j)),
        ],
        out_specs=pl.BlockSpec((1, bp, d), lambda i, j: (i, 0, j)),
        out_shape=jax.ShapeDtypeStruct((depth, bp, d3), F32),
        compiler_params=_cparams("arbitrary", "arbitrary"),
        name="adaln_mod",
    )(c_pad, mod_w, mod_b.reshape(depth, 1, d3))


def _inproj_kernel(x_ref, sh_ref, sc_ref, g_ref, *refs, n_out):
    w_refs, o_refs = refs[:n_out], refs[n_out:]
    x = x_ref[0]
    ms = jnp.mean(x * x, axis=-1, keepdims=True)
    hn = (x * lax.rsqrt(ms + 1e-6)) * g_ref[...]
    hn = hn * (1.0 + sc_ref[0]) + sh_ref[0]
    hb = hn.astype(BF16)
    for w_ref, o_ref in zip(w_refs, o_refs):
        o_ref[0] = jnp.dot(hb, w_ref[...], preferred_element_type=F32)


def _in_projection(x, shift, scale, g, weights, tl=256):
    b, l, d = x.shape
    n_out = len(weights)
    in_specs = [
        pl.BlockSpec((1, tl, d), lambda i, j: (i, j, 0)),
        pl.BlockSpec((1, 1, d), lambda i, j: (i, 0, 0)),
        pl.BlockSpec((1, 1, d), lambda i, j: (i, 0, 0)),
        pl.BlockSpec((1, d), lambda i, j: (0, 0)),
    ] + [_resident(w.shape, lambda i, j: (0, 0)) for w in weights]
    out_specs = [pl.BlockSpec((1, tl, w.shape[1]), lambda i, j: (i, j, 0)) for w in weights]
    out_shape = [jax.ShapeDtypeStruct((b, l, w.shape[1]), F32) for w in weights]
    return pl.pallas_call(
        functools.partial(_inproj_kernel, n_out=n_out),
        grid=(b, l // tl),
        in_specs=in_specs,
        out_specs=out_specs,
        out_shape=out_shape,
        compiler_params=_cparams("parallel", "parallel"),
        name="in_proj",
    )(x, shift, scale, g, *weights)


_CONV_ROWS = 256
_HALO = SUBLANES


def _fill_padded(pad_ref, src_ref, l):
    tc = pad_ref.shape[1]
    zeros = jnp.zeros((_HALO, tc), F32)
    pad_ref[pl.ds(0, _HALO), :] = zeros
    pad_ref[pl.ds(l + _HALO, _HALO), :] = zeros

    def body(i, carry):
        r = pl.multiple_of(i * _CONV_ROWS, _CONV_ROWS)
        pad_ref[pl.ds(r + _HALO, _CONV_ROWS), :] = src_ref[pl.ds(r, _CONV_ROWS), :]
        return carry

    lax.fori_loop(0, l // _CONV_ROWS, body, 0)


def _conv_rows(pad_ref, r, w, bias, taps, left):
    n = _CONV_ROWS + 2 * _HALO
    xe = pad_ref[pl.ds(r, n), :]
    acc = None
    for k in range(taps):
        off = k - left
        xs = xe if off == 0 else pltpu.roll(xe, (-off) % n, axis=0)
        term = w[k:k + 1, :] * xs[_HALO:_HALO + _CONV_ROWS, :]
        acc = term if acc is None else acc + term
    return acc + bias


def _hy_conv_kernel(x0_ref, x1_ref, v_ref, w0_ref, w1_ref, w2_ref, b0_ref, b1_ref, b2_ref,
                    x0c_ref, vf_ref, p0_ref, p1_ref, p2_ref, *, l):
    _fill_padded(p0_ref, x0_ref.at[0], l)
    _fill_padded(p1_ref, x1_ref.at[0], l)
    _fill_padded(p2_ref, v_ref.at[0], l)
    w0, w1, w2 = w0_ref[...], w1_ref[...], w2_ref[...]
    b0, b1, b2 = b0_ref[...], b1_ref[...], b2_ref[...]

    def body(i, carry):
        r = pl.multiple_of(i * _CONV_ROWS, _CONV_ROWS)
        x0c_ref[0, pl.ds(r, _CONV_ROWS), :] = _conv_rows(p0_ref, r, w0, b0, 3, 1)
        x1c = _conv_rows(p1_ref, r, w1, b1, 3, 1)
        vc = _conv_rows(p2_ref, r, w2, b2, 3, 1)
        vf_ref[0, pl.ds(r, _CONV_ROWS), :] = vc * x1c
        return carry

    lax.fori_loop(0, l // _CONV_ROWS, body, 0)


def _hyena_short_conv(hyu, conv_w, conv_b, tc=128):
    b, l, c3 = hyu.shape
    c = c3 // 3
    nct = c // tc
    xspec = lambda o: pl.BlockSpec((1, l, tc), lambda i, j, o=o: (i, 0, j + o * nct))
    wspec = lambda o: pl.BlockSpec((3, tc), lambda i, j, o=o: (0, j + o * nct))
    bspec = lambda o: pl.BlockSpec((1, tc), lambda i, j, o=o: (0, j + o * nct))
    ospec = pl.BlockSpec((1, l, tc), lambda i, j: (i, 0, j))
    return pl.pallas_call(
        functools.partial(_hy_conv_kernel, l=l),
        grid=(b, nct),
        in_specs=[xspec(0), xspec(1), xspec(2), wspec(0), wspec(1), wspec(2),
                  bspec(0), bspec(1), bspec(2)],
        out_specs=[ospec, ospec],
        out_shape=[jax.ShapeDtypeStruct((b, l, c), F32)] * 2,
        scratch_shapes=[pltpu.VMEM((l + 2 * _HALO, tc), F32)] * 3,
        compiler_params=_cparams("parallel", "parallel"),
        name="hyena_short_conv",
    )(hyu, hyu, hyu, conv_w, conv_w, conv_w, conv_b, conv_b, conv_b)


def _conv_silu_kernel(x_ref, w_ref, b_ref, o_ref, p_ref, *, l, taps, left):
    _fill_padded(p_ref, x_ref.at[0], l)
    w, bias = w_ref[...], b_ref[...]

    def body(i, carry):
        r = pl.multiple_of(i * _CONV_ROWS, _CONV_ROWS)
        o_ref[0, pl.ds(r, _CONV_ROWS), :] = _silu(_conv_rows(p_ref, r, w, bias, taps, left))
        return carry

    lax.fori_loop(0, l // _CONV_ROWS, body, 0)


def _conv_silu(x, conv_w, conv_b, tc=256):
    b, l, c = x.shape
    taps = conv_w.shape[0]
    return pl.pallas_call(
        functools.partial(_conv_silu_kernel, l=l, taps=taps, left=taps // 2),
        grid=(b, c // tc),
        in_specs=[pl.BlockSpec((1, l, tc), lambda i, j: (i, 0, j)),
                  pl.BlockSpec((taps, tc), lambda i, j: (0, j)),
                  pl.BlockSpec((1, tc), lambda i, j: (0, j))],
        out_specs=pl.BlockSpec((1, l, tc), lambda i, j: (i, 0, j)),
        out_shape=jax.ShapeDtypeStruct((b, l, c), F32),
        scratch_shapes=[pltpu.VMEM((l + 2 * _HALO, tc), F32)],
        compiler_params=_cparams("parallel", "parallel"),
        name="ssd_conv_silu",
    )(x, conv_w, conv_b)


def _filter_kernel(z_ref, w1_ref, b1_ref, fr_ref, w2_ref, b2_ref, w3_ref, dl_ref,
                   hfg_ref, nrm_ref, *, tr, c):
    i = pl.program_id(0)
    z = z_ref[...]
    fr = fr_ref[...]
    d = functools.partial(jnp.dot, precision=HI, preferred_element_type=F32)
    h = jnp.sin(fr * (d(z, w1_ref[...]) + b1_ref[...]))
    h = jnp.sin(fr * (d(h, w2_ref[...]) + b2_ref[...]))
    h = d(h, w3_ref[...])
    window = jnp.exp(-z[:, 0:1] * dl_ref[...])
    hf = h[:, :c] * window
    row = lax.broadcasted_iota(jnp.int32, (tr, 1), 0) + i * tr
    g = jnp.where(row == 0, 0.0, h[:, c:] * window)
    hfg_ref[0] = hf
    hfg_ref[1] = g
    part = jnp.sum(jnp.abs(hf) + jnp.abs(g), axis=0, keepdims=True)

    @pl.when(i == 0)
    def _():
        nrm_ref[...] = part

    @pl.when(i != 0)
    def _():
        nrm_ref[...] = nrm_ref[...] + part


def _hyena_filter_taps(zfeat, w1, b1, freq, w2, b2, w3, deltas, tr=512):
    l, zp = zfeat.shape
    hp = w1.shape[1]
    c2 = w3.shape[1]
    c = c2 // 2
    full = lambda shape: pl.BlockSpec(shape, lambda i: (0,) * len(shape))
    return pl.pallas_call(
        functools.partial(_filter_kernel, tr=tr, c=c),
        grid=(l // tr,),
        in_specs=[pl.BlockSpec((tr, zp), lambda i: (i, 0)), full((zp, hp)), full((1, hp)),
                  full((1, hp)), full((hp, hp)), full((1, hp)), full((hp, c2)), full((1, c))],
        out_specs=[pl.BlockSpec((2, tr, c), lambda i: (0, i, 0)), full((1, c))],
        out_shape=[jax.ShapeDtypeStruct((2, l, c), F32), jax.ShapeDtypeStruct((1, c), F32)],
        compiler_params=_cparams("arbitrary"),
        name="hyena_filter_mlp",
    )(zfeat, w1, b1, freq, w2, b2, w3, deltas)


def _dft1_kernel(x_ref, fre_ref, fim_ref, re_ref, im_ref):
    x = x_ref[0]
    re_ref[0] = jnp.dot(fre_ref[...], x, precision=HI, preferred_element_type=F32)
    im_ref[0] = jnp.dot(fim_ref[...], x, precision=HI, preferred_element_type=F32)


def _dft_level1(x2, fre, fim, tlane=4096):
    b, nh, w = x2.shape
    kp = fre.shape[0]
    return pl.pallas_call(
        _dft1_kernel,
        grid=(b, w // tlane),
        in_specs=[pl.BlockSpec((1, nh, tlane), lambda i, j: (i, 0, j)),
                  pl.BlockSpec((kp, nh), lambda i, j: (0, 0)),
                  pl.BlockSpec((kp, nh), lambda i, j: (0, 0))],
        out_specs=[pl.BlockSpec((1, kp, tlane), lambda i, j: (i, 0, j))] * 2,
        out_shape=[jax.ShapeDtypeStruct((b, kp, w), F32)] * 2,
        compiler_params=_cparams("parallel", "parallel"),
        name="dft_level1",
    )(x2, fre, fim)


def _filter_spec_kernel(re_ref, im_ref, g_ref, nrm_ref, hre_ref, him_ref):
    n2 = re_ref.shape[1]
    g = g_ref[...]
    d = functools.partial(jnp.dot, precision=HI, preferred_element_type=F32)
    xf = d(g, jnp.concatenate([re_ref[0], im_ref[0]], axis=0))
    xg = d(g, jnp.concatenate([re_ref[1], im_ref[1]], axis=0))
    inv = 1.0 / nrm_ref[...]
    hre_ref[...] = (xf[:n2] + xg[:n2]) * inv
    him_ref[...] = (xf[n2:] - xg[n2:]) * inv


def _filter_spectrum(b1re, b1im, g32, nrm, tc=512):
    _, _, n2, c = b1re.shape
    k1 = g32.shape[0]
    return pl.pallas_call(
        _filter_spec_kernel,
        grid=(k1, c // tc),
        in_specs=[pl.BlockSpec((2, None, n2, tc), lambda k, j: (0, k, 0, j)),
                  pl.BlockSpec((2, None, n2, tc), lambda k, j: (0, k, 0, j)),
                  pl.BlockSpec((None, 2 * n2, 2 * n2), lambda k, j: (k, 0, 0)),
                  pl.BlockSpec((1, tc), lambda k, j: (0, j))],
        out_specs=[pl.BlockSpec((None, n2, tc), lambda k, j: (k, 0, j))] * 2,
        out_shape=[jax.ShapeDtypeStruct((k1, n2, c), F32)] * 2,
        compiler_params=_cparams("parallel", "parallel"),
        name="hyena_filter_spectrum",
    )(b1re, b1im, g32, nrm)


def _dft2_kernel(re_ref, im_ref, g_ref, gi_ref, hre_ref, him_ref, zre_ref, zim_ref, *, k1):
    k = pl.program_id(0)
    n2 = re_ref.shape[0]

    @pl.when(k < k1)
    def _():
        d = functools.partial(jnp.dot, preferred_element_type=F32)
        x = d(g_ref[...], jnp.concatenate([re_ref[...], im_ref[...]], axis=0).astype(BF16))
        xr, xi = x[:n2], x[n2:]
        hr, hi = hre_ref[...], him_ref[...]
        y = jnp.concatenate([xr * hr - xi * hi, xr * hi + xi * hr], axis=0).astype(BF16)
        z = d(gi_ref[...], y)
        zre_ref[...] = z[:n2]
        zim_ref[...] = z[n2:]

    @pl.when(k >= k1)
    def _():
        zre_ref[...] = jnp.zeros_like(zre_ref)
        zim_ref[...] = jnp.zeros_like(zim_ref)


def _dft_level2_conv(b1re, b1im, g, ginv, hre, him, tc=1024):
    b, kp, n2, c = b1re.shape
    k1 = g.shape[0]
    kk = lambda k: jnp.minimum(k, k1 - 1)
    dspec = pl.BlockSpec((None, None, n2, tc), lambda k, i, j: (i, kk(k), 0, j))
    gspec = pl.BlockSpec((None, 2 * n2, 2 * n2), lambda k, i, j: (kk(k), 0, 0))
    hspec = pl.BlockSpec((None, n2, tc), lambda k, i, j: (kk(k), 0, j))
    ospec = pl.BlockSpec((None, None, n2, tc), lambda k, i, j: (i, k, 0, j))
    return pl.pallas_call(
        functools.partial(_dft2_kernel, k1=k1),
        grid=(kp, b, c // tc),
        in_specs=[dspec, dspec, gspec, gspec, hspec, hspec],
        out_specs=[ospec, ospec],
        out_shape=[jax.ShapeDtypeStruct((b, kp, n2, c), F32)] * 2,
        compiler_params=_cparams("parallel", "parallel", "parallel"),
        name="dft_level2_conv",
    )(b1re, b1im, g, ginv, hre, him)


def _idft1_kernel(zre_ref, zim_ref, are_ref, aim_ref, x0_ref, vf_ref, gt_ref, bias_ref, o_ref):
    d = functools.partial(jnp.dot, precision=HI, preferred_element_type=F32)
    y = d(are_ref[...], zre_ref[0]) + d(aim_ref[...], zim_ref[0])
    vf = vf_ref[0]
    y = y + vf * bias_ref[...]
    o_ref[0] = (x0_ref[0] * y) * _silu(gt_ref[0])


def _idft_level1_gate(zre, zim, are, aim, x0c, vf, gate, bias_t, tlane=4096):
    b, kp, w = zre.shape
    nh = are.shape[0]
    zspec = pl.BlockSpec((1, kp, tlane), lambda i, j: (i, 0, j))
    aspec = pl.BlockSpec((nh, kp), lambda i, j: (0, 0))
    xspec = pl.BlockSpec((1, nh, tlane), lambda i, j: (i, 0, j))
    return pl.pallas_call(
        _idft1_kernel,
        grid=(b, w // tlane),
        in_specs=[zspec, zspec, aspec, aspec, xspec, xspec, xspec,
                  pl.BlockSpec((1, tlane), lambda i, j: (0, 0))],
        out_specs=xspec,
        out_shape=jax.ShapeDtypeStruct((b, nh, w), F32),
        compiler_params=_cparams("parallel", "parallel"),
        name="idft_level1_gate",
    )(zre, zim, are, aim, x0c, vf, gate, bias_t)


@functools.lru_cache(maxsize=None)
def _dft_tables():
    n1, n2, k1, kp = DFT_N1, DFT_N2, DFT_K1, DFT_K1P
    n = n1 * n2
    nh = n1 // 2
    kk = np.arange(k1)[:, None]
    nn = np.arange(nh)[None, :]
    ang = 2.0 * np.pi * kk * nn / n1
    fre = np.zeros((kp, nh)); fim = np.zeros((kp, nh))
    fre[:k1] = np.cos(ang); fim[:k1] = -np.sin(ang)
    m = np.arange(n2)[None, None, :]
    k2 = np.arange(n2)[None, :, None]
    ka = np.arange(k1)[:, None, None]
    ph = -2.0 * np.pi * m * (ka / n + k2 / n2)
    gr, gi = np.cos(ph), np.sin(ph)
    g = np.concatenate([np.concatenate([gr, -gi], axis=2), np.concatenate([gi, gr], axis=2)], axis=1)
    grt, git = np.swapaxes(gr, 1, 2), np.swapaxes(gi, 1, 2)
    ginv = np.concatenate([np.concatenate([grt, git], axis=2), np.concatenate([-git, grt], axis=2)], axis=1)
    wk = np.where((np.arange(k1) == 0) | (np.arange(k1) == n1 // 2), 1.0, 2.0)[None, :]
    ang2 = 2.0 * np.pi * np.arange(nh)[:, None] * np.arange(k1)[None, :] / n1
    are = np.zeros((nh, kp)); aim = np.zeros((nh, kp))
    are[:, :k1] = wk * np.cos(ang2) / n
    aim[:, :k1] = -wk * np.sin(ang2) / n
    f = lambda a: np.asarray(a, np.float32)
    return f(fre), f(fim), f(g), f(ginv), f(are), f(aim)


def _position_features(l):
    f32 = F32
    pos = jnp.arange(l, dtype=f32)[:, None]
    t = pos / max(l - 1, 1)
    bands = (HY_EMB - 1) // 2
    f = jnp.linspace(1e-4, bands - 1, bands, dtype=f32)[None, :]
    ang = f * pos * (2.0 * math.pi / l)
    z = jnp.concatenate([t, jnp.cos(ang), -jnp.sin(ang)], axis=-1)
    return jnp.pad(z, ((0, 0), (0, LANES - HY_EMB)))


def _pad2(a, rows, cols):
    return jnp.pad(a, ((0, rows - a.shape[0]), (0, cols - a.shape[1])))


def _hyena_spectrum(l, c, fw1, fb1, fw2, fb2, fw3, freq):
    fre, fim, g32, _, _, _ = _dft_tables()
    hp = LANES
    zfeat = _position_features(l)
    w1 = _pad2(fw1, LANES, hp)
    w2 = _pad2(fw2, hp, hp)
    w3 = _pad2(fw3, hp, fw3.shape[1])
    row = lambda v: _pad2(v[None, :], 1, hp)
    deltas = jnp.abs(jnp.linspace(math.log(HY_DECAY_TARGET) / HY_LONG_DECAY_PCT,
                                  math.log(HY_DECAY_TARGET) / HY_SHORT_DECAY_PCT, c, dtype=F32))[None, :]
    hfg, nrm = _hyena_filter_taps(zfeat, w1, row(fb1), row(freq), w2, row(fb2), w3, deltas)
    b1re, b1im = _dft_level1(hfg.reshape(2, DFT_N1 // 2, DFT_N2 * c), jnp.asarray(fre), jnp.asarray(fim))
    shp = (2, DFT_K1P, DFT_N2, c)
    return _filter_spectrum(b1re.reshape(shp), b1im.reshape(shp), jnp.asarray(g32), nrm)


def _hyena_branch(hyu, hy_gate, conv_w, conv_b, hre, him, bias):
    b, l, c3 = hyu.shape
    c = c3 // 3
    fre, fim, g32, ginv32, are, aim = _dft_tables()
    x0c, vf = _hyena_short_conv(hyu, conv_w, conv_b[None, :])
    nh, w = DFT_N1 // 2, DFT_N2 * c
    b1re, b1im = _dft_level1(vf.reshape(b, nh, w), jnp.asarray(fre), jnp.asarray(fim))
    shp = (b, DFT_K1P, DFT_N2, c)
    zre, zim = _dft_level2_conv(b1re.reshape(shp), b1im.reshape(shp),
                                jnp.asarray(g32, dtype=BF16), jnp.asarray(ginv32, dtype=BF16), hre, him)
    tlane = 4096
    bias_t = jnp.tile(bias[None, :], (1, tlane // c))
    ya = _idft_level1_gate(zre.reshape(b, DFT_K1P, w), zim.reshape(b, DFT_K1P, w),
                           jnp.asarray(are), jnp.asarray(aim),
                           x0c.reshape(b, nh, w), vf.reshape(b, nh, w), hy_gate.reshape(b, nh, w),
                           bias_t, tlane=tlane)
    return ya.reshape(b, l, c)


def _ssd_kernel(xs_ref, bm_ref, cm_ref, dt_ref, dtb_ref, alog_ref, rep_ref, dsk_ref, mask_ref,
                y_ref, prev_ref, *, reverse, lane0, chunks):
    q, n, hd = SSD_CHUNK, SSD_STATE, SSD_HEAD_DIM
    gw = (SSD_HEADS // SSD_GROUPS) * hd

    @pl.when(pl.program_id(1) == 0)
    def _():
        prev_ref[...] = jnp.zeros_like(prev_ref)

    mask = mask_ref[...]
    maskb = mask > 0
    rep = rep_ref[...]
    neg_a = -jnp.exp(alog_ref[...])
    lane = lax.broadcasted_iota(jnp.int32, (1, LANES), 1)
    lo_half = (lane < hd).astype(F32)
    hi_half = 1.0 - lo_half
    d = functools.partial(jnp.dot, preferred_element_type=F32)
    order = range(chunks - 1, -1, -1) if reverse else range(chunks)
    for k in order:
        rows = pl.ds(k * q, q)
        xs = xs_ref[0, rows, :]
        bm = bm_ref[0, rows, :]
        cm = cm_ref[0, rows, :]
        dt = _softplus(dt_ref[0, rows, :] + dtb_ref[...])
        la = dt * neg_a
        cum = _sel_dot(mask, la)
        cum_t = cum.T
        dt_x = _dot_sel(dt, rep)
        cum_x = _dot_sel(cum, rep)
        edge = cum_x[0:1, :] if reverse else cum_x[q - 1:q, :]
        xdt = xs * dt_x
        d_in = jnp.exp(cum_x)
        xsd = (xdt * jnp.exp(edge - cum_x)).astype(BF16)
        cd = jnp.exp(edge)
        ys = []
        for g in range(SSD_GROUPS):
            cg = cm[:, g * n:(g + 1) * n].astype(BF16)
            bg = bm[:, g * n:(g + 1) * n].astype(BF16)
            cb = lax.dot_general(cg, bg, (((1,), (1,)), ((), ())), preferred_element_type=F32)
            for pair in range(gw // LANES):
                lo = g * gw + pair * LANES
                xp = xdt[:, lo:lo + LANES]
                acc = None
                for half, sel in ((0, lo_half), (1, hi_half)):
                    h = lo // hd + half
                    col = cum[:, lane0 + h:lane0 + h + 1]
                    rw = cum_t[lane0 + h:lane0 + h + 1, :]
                    seg = jnp.exp(jnp.where(maskb, col - rw, -jnp.inf))
                    term = d((cb * seg).astype(BF16), (xp * sel).astype(BF16))
                    acc = term if acc is None else acc + term
                ys.append(acc)
            sl = slice(g * gw, (g + 1) * gw)
            prev_g = prev_ref[:, sl]
            y_off = d(cg, prev_g.astype(BF16)) * d_in[:, sl]
            st = lax.dot_general(bg, xsd[:, sl], (((0,), (0,)), ((), ())), preferred_element_type=F32)
            prev_ref[:, sl] = prev_g * cd[:, sl] + st
            for pair in range(gw // LANES):
                idx = g * (gw // LANES) + pair
                ys[idx] = ys[idx] + y_off[:, pair * LANES:(pair + 1) * LANES]
        y = jnp.concatenate(ys, axis=1)
        if not reverse:
            y = y + xs * dsk_ref[...]
        y_ref[0, rows, :] = y


def _ssd_direction(xbc, dt_raw, dtb, alog, rep, dskip, mask, *, reverse, ts=512):
    b, l, _ = xbc.shape
    hp = SSD_HEADS * SSD_HEAD_DIM
    gn = SSD_GROUPS * SSD_STATE
    nt = l // ts
    tmap = (lambda j: nt - 1 - j) if reverse else (lambda j: j)
    full = lambda shape: pl.BlockSpec(shape, lambda i, j: (0,) * len(shape))
    return pl.pallas_call(
        functools.partial(_ssd_kernel, reverse=reverse, lane0=SSD_HEADS if reverse else 0,
                          chunks=ts // SSD_CHUNK),
        grid=(b, nt),
        in_specs=[pl.BlockSpec((1, ts, hp), lambda i, j: (i, tmap(j), 0)),
                  pl.BlockSpec((1, ts, gn), lambda i, j: (i, tmap(j), hp // gn)),
                  pl.BlockSpec((1, ts, gn), lambda i, j: (i, tmap(j), hp // gn + 1)),
                  pl.BlockSpec((1, ts, LANES), lambda i, j: (i, tmap(j), 0)),
                  full((1, LANES)), full((1, LANES)), full((LANES, hp)), full((1, hp)),
                  full((SSD_CHUNK, SSD_CHUNK))],
        out_specs=pl.BlockSpec((1, ts, hp), lambda i, j: (i, tmap(j), 0)),
        out_shape=jax.ShapeDtypeStruct((b, l, hp), F32),
        scratch_shapes=[pltpu.VMEM((SSD_STATE, hp), F32)],
        compiler_params=_cparams("parallel", "arbitrary"),
        name="ssd_bwd" if reverse else "ssd_fwd",
    )(xbc, xbc, xbc, dt_raw, dtb, alog, rep, dskip, mask)


def _ssd_branch(xbc, dt_raw, conv_w, conv_b, dt_bias, a_log, d_skip):
    xc = _conv_silu(xbc, conv_w, conv_b[None, :])
    pad_row = lambda v: _pad2(v.reshape(1, -1), 1, LANES)
    dtb, alog = pad_row(dt_bias), pad_row(a_log)
    dskip = jnp.repeat(d_skip, SSD_HEAD_DIM)[None, :]
    hp = SSD_HEADS * SSD_HEAD_DIM
    lane_head = np.arange(hp)[None, :] // SSD_HEAD_DIM
    src = np.arange(LANES)[:, None]
    idx = np.arange(SSD_CHUNK)
    outs = []
    for reverse in (False, True):
        rep = jnp.asarray(src == lane_head + (SSD_HEADS if reverse else 0), dtype=BF16)
        allowed = (idx[None, :] >= idx[:, None]) if reverse else (idx[None, :] <= idx[:, None])
        mask = jnp.asarray(allowed, dtype=BF16)
        outs.append(_ssd_direction(xc, dt_raw, dtb, alog, rep, dskip, mask, reverse=reverse))
    return outs


_LRU_ROWS = 256


def _lru_kernel(x_ref, gt_ref, cw_ref, cb_ref, wa_ref, wx_ref, ba_ref, bx_ref, lam_ref,
                o_ref, pad_ref, af_ref, uf_ref, ab_ref, ub_ref, *, l, heads):
    tw = heads * LRU_BLOCK
    _fill_padded(pad_ref, x_ref.at[0], l)
    cw, cb = cw_ref[...], cb_ref[...]
    lam = lam_ref[...]
    log_sig = -_softplus(-lam)
    ba, bx = ba_ref[...], bx_ref[...]
    d = functools.partial(jnp.dot, preferred_element_type=F32)
    a_refs, u_refs = (af_ref, ab_ref), (uf_ref, ub_ref)

    def gates(i, carry):
        r = pl.multiple_of(i * _CONV_ROWS, _CONV_ROWS)
        xc = _conv_rows(pad_ref, r, cw, cb, 4, 2)
        for hd in range(heads):
            ls = slice(hd * LRU_BLOCK, (hd + 1) * LRU_BLOCK)
            xh = xc[:, ls]
            xb = xh.astype(BF16)
            for dr in range(2):
                rg = jax.nn.sigmoid(d(xb, wa_ref[dr, hd]) + ba[dr:dr + 1, ls])
                ig = jax.nn.sigmoid(d(xb, wx_ref[dr, hd]) + bx[dr:dr + 1, ls])
                a = jnp.exp(LRU_C * rg * log_sig[dr:dr + 1, ls])
                u = xh * ig * jnp.sqrt(1.0 - a * a)
                a_refs[dr][pl.ds(r, _CONV_ROWS), ls] = a
                u_refs[dr][pl.ds(r, _CONV_ROWS), ls] = u
        return carry

    lax.fori_loop(0, l // _CONV_ROWS, gates, 0)

    row = lax.broadcasted_iota(jnp.int32, (SUBLANES, tw), 0)
    nt = l // SUBLANES

    def tile_scan(a, u, reverse):
        for s in (1, 2, 4):
            sh = (SUBLANES - s) if reverse else s
            valid = (row < SUBLANES - s) if reverse else (row >= s)
            a_sh = jnp.where(valid, pltpu.roll(a, sh, axis=0), 1.0)
            u_sh = jnp.where(valid, pltpu.roll(u, sh, axis=0), 0.0)
            u = u + a * u_sh
            a = a * a_sh
        return a, u

    def scan(i, carry):
        hf, hb = carry
        rf = pl.multiple_of(i * SUBLANES, SUBLANES)
        rb = pl.multiple_of((nt - 1 - i) * SUBLANES, SUBLANES)
        a, u = tile_scan(af_ref[pl.ds(rf, SUBLANES), :], uf_ref[pl.ds(rf, SUBLANES), :], False)
        h = u + a * hf
        uf_ref[pl.ds(rf, SUBLANES), :] = h
        hf = jnp.broadcast_to(h[SUBLANES - 1:SUBLANES, :], (SUBLANES, tw))
        a, u = tile_scan(ab_ref[pl.ds(rb, SUBLANES), :], ub_ref[pl.ds(rb, SUBLANES), :], True)
        h = u + a * hb
        ub_ref[pl.ds(rb, SUBLANES), :] = h
        hb = jnp.broadcast_to(h[0:1, :], (SUBLANES, tw))
        return hf, hb

    zero = jnp.zeros((SUBLANES, tw), F32)
    lax.fori_loop(0, nt, scan, (zero, zero))

    def combine(i, carry):
        r = pl.multiple_of(i * _LRU_ROWS, _LRU_ROWS)
        rows = pl.ds(r, _LRU_ROWS)
        o_ref[0, rows, :] = (uf_ref[rows, :] + ub_ref[rows, :]) * _silu(gt_ref[0, rows, :])
        return carry

    lax.fori_loop(0, l // _LRU_ROWS, combine, 0)


def _rglru_mixer(xb, gate, conv_w, conv_b, w_a, w_x, b_a, b_x, lam, heads=2):
    b, l, w = xb.shape
    tw = heads * LRU_BLOCK
    xspec = pl.BlockSpec((1, l, tw), lambda i, j: (i, 0, j))
    wspec = pl.BlockSpec((2, heads, LRU_BLOCK, LRU_BLOCK), lambda i, j: (0, j, 0, 0))
    vspec = pl.BlockSpec((2, tw), lambda i, j: (0, j))
    seq = pltpu.VMEM((l, tw), F32)
    return pl.pallas_call(
        functools.partial(_lru_kernel, l=l, heads=heads),
        grid=(b, w // tw),
        in_specs=[xspec, xspec,
                  pl.BlockSpec((4, tw), lambda i, j: (0, j)),
                  pl.BlockSpec((1, tw), lambda i, j: (0, j)),
                  wspec, wspec, vspec, vspec, vspec],
        out_specs=xspec,
        out_shape=jax.ShapeDtypeStruct((b, l, w), F32),
        scratch_shapes=[pltpu.VMEM((l + 2 * _HALO, tw), F32), seq, seq, seq, seq],
        compiler_params=_cparams("parallel", "parallel"),
        name="rglru",
    )(xb, gate, conv_w, conv_b[None, :], w_a.astype(BF16), w_x.astype(BF16), b_a, b_x, lam)


def _rms(x, g, eps):
    return (x * lax.rsqrt(jnp.mean(x * x, axis=-1, keepdims=True) + eps)) * g


def _outproj_even_kernel(x_ref, ya_ref, y0_ref, y1_ref, z_ref, gm_ref, ng_ref, wa_ref, wb_ref, o_ref):
    yb = (y0_ref[0] + y1_ref[0]) * _silu(z_ref[0])
    ng = ng_ref[...]
    gw = yb.shape[1] // SSD_GROUPS
    parts = [_rms(yb[:, g * gw:(g + 1) * gw], ng[:, g * gw:(g + 1) * gw], 1e-5) for g in range(SSD_GROUPS)]
    ybn = jnp.concatenate(parts, axis=1)
    d = functools.partial(jnp.dot, preferred_element_type=F32)
    acc = d(ya_ref[0].astype(BF16), wa_ref[...]) + d(ybn.astype(BF16), wb_ref[...])
    o_ref[0] = x_ref[0] + gm_ref[0] * acc


def _out_projection_even(x, ya, y0, y1, z, gate_mod, norm_g, w_out, tl=512):
    b, l, dm = x.shape
    c = ya.shape[2]
    rows = lambda n: pl.BlockSpec((1, tl, n), lambda i, j: (i, j, 0))
    wa, wb = w_out[:c].astype(BF16), w_out[c:].astype(BF16)
    return pl.pallas_call(
        _outproj_even_kernel,
        grid=(b, l // tl),
        in_specs=[rows(dm), rows(c), rows(c), rows(c), rows(c),
                  pl.BlockSpec((1, 1, dm), lambda i, j: (i, 0, 0)),
                  pl.BlockSpec((1, c), lambda i, j: (0, 0)),
                  _resident(wa.shape, lambda i, j: (0, 0)), _resident(wb.shape, lambda i, j: (0, 0))],
        out_specs=rows(dm),
        out_shape=jax.ShapeDtypeStruct((b, l, dm), F32),
        compiler_params=_cparams("parallel", "parallel"),
        name="out_proj_even",
    )(x, ya, y0, y1, z, gate_mod, norm_g, wa, wb)


def _outproj_odd_kernel(x_ref, y_ref, gm_ref, w_ref, fg_ref, o_ref, *, final):
    acc = jnp.dot(y_ref[0].astype(BF16), w_ref[...], preferred_element_type=F32)
    xn = x_ref[0] + gm_ref[0] * acc
    o_ref[0] = _rms(xn, fg_ref[...], 1e-6) if final else xn


def _out_projection_odd(x, y, gate_mod, w_out, final_g, final, tl=512):
    b, l, dm = x.shape
    w = y.shape[2]
    rows = lambda n: pl.BlockSpec((1, tl, n), lambda i, j: (i, j, 0))
    wb = w_out.astype(BF16)
    return pl.pallas_call(
        functools.partial(_outproj_odd_kernel, final=final),
        grid=(b, l // tl),
        in_specs=[rows(dm), rows(w),
                  pl.BlockSpec((1, 1, dm), lambda i, j: (i, 0, 0)),
                  _resident(wb.shape, lambda i, j: (0, 0)),
                  pl.BlockSpec((1, dm), lambda i, j: (0, 0))],
        out_specs=rows(dm),
        out_shape=jax.ShapeDtypeStruct((b, l, dm), F32),
        compiler_params=_cparams("parallel", "parallel"),
        name="out_proj_odd",
    )(x, y, gate_mod, wb, final_g)


def kernel(x_prompt, x_sample, c_prompt, c_sample, mod_w, mod_b, norm_g, final_g, ev_w_in, ev_w_out, hy_conv_w, hy_conv_b, hy_fw1, hy_fb1, hy_fw2, hy_fb2, hy_fw3, hy_freq, hy_bias, ssd_conv_w, ssd_conv_b, ssd_dt_bias, ssd_A_log, ssd_D, ssd_norm_g, od_w_in, od_w_out, lru_conv_w, lru_conv_b, lru_w_a, lru_b_a, lru_w_x, lru_b_x, lru_lam):
    bp = x_prompt.shape[0]
    x = jnp.concatenate([x_prompt, x_sample], axis=0)
    c = jnp.concatenate([c_prompt, c_sample], axis=0)
    b, l, dm = x.shape
    depth = mod_w.shape[0]
    assert 2 * l == DFT_N1 * DFT_N2

    b_pad = -(-b // SUBLANES) * SUBLANES
    mods = _modulation(jnp.pad(c, ((0, b_pad - b), (0, 0))), mod_w, mod_b)[:, :b]
    shift, scale, gate_mod = (mods[:, :, k * dm:(k + 1) * dm].reshape(depth, b, 1, dm) for k in range(3))

    hyc = hy_bias.shape[1]
    ssw = ssd_norm_g.shape[1]
    xbw = ssd_conv_w.shape[2]
    o1, o2, o3, o4 = 3 * hyc, 4 * hyc, 4 * hyc + ssw, 4 * hyc + ssw + xbw

    for i in range(depth):
        j = i // 2
        g = norm_g[i][None, :]
        if i % 2 == 0:
            w_in = ev_w_in[j].astype(BF16)
            w_dt = jnp.pad(w_in[:, o4:], ((0, 0), (0, LANES - (w_in.shape[1] - o4))))
            hyu, hyg, z, xbc, dt_raw = _in_projection(
                x, shift[i], scale[i], g, [w_in[:, :o1], w_in[:, o1:o2], w_in[:, o2:o3], w_in[:, o3:o4], w_dt])
            hre, him = _hyena_spectrum(l, hyc, hy_fw1[j], hy_fb1[j], hy_fw2[j], hy_fb2[j], hy_fw3[j], hy_freq[j])
            ya = _hyena_branch(hyu, hyg, hy_conv_w[j], hy_conv_b[j], hre, him, hy_bias[j])
            y0, y1 = _ssd_branch(xbc, dt_raw, ssd_conv_w[j], ssd_conv_b[j], ssd_dt_bias[j], ssd_A_log[j], ssd_D[j])
            x = _out_projection_even(x, ya, y0, y1, z, gate_mod[i], ssd_norm_g[j][None, :], ev_w_out[j])
        else:
            w_in = od_w_in[j].astype(BF16)
            half = w_in.shape[1] // 2
            xb, gt = _in_projection(x, shift[i], scale[i], g, [w_in[:, :half], w_in[:, half:]])
            w_a = lru_w_a[j]
            y = _rglru_mixer(xb, gt, lru_conv_w[j], lru_conv_b[j], lru_w_a[j], lru_w_x[j],
                             lru_b_a[j], lru_b_x[j], lru_lam[j])
            x = _out_projection_odd(x, y, gate_mod[i], od_w_out[j], final_g[None, :], final=(i == depth - 1))
    if depth % 2 == 1:
        raise NotImplementedError("final norm is fused into the last (odd) layer's output projection")
    return x[:bp], x[bp:]
```

```python
import functools
import math

import numpy as np
import jax
import jax.numpy as jnp
from jax import lax
from jax.experimental import pallas as pl
from jax.experimental.pallas import tpu as pltpu

F32 = jnp.float32
BF16 = jnp.bfloat16
HI = lax.Precision.HIGHEST

V7X_VMEM_BYTES = 64 * 1024 * 1024
VMEM_LIMIT = V7X_VMEM_BYTES - 8 * 1024 * 1024
LANES = 128
SUBLANES = 8

HY_EMB = 33
HY_SHORT_DECAY_PCT = 0.3
HY_LONG_DECAY_PCT = 1.5
HY_DECAY_TARGET = 1e-2
SSD_HEADS = 16
SSD_HEAD_DIM = 64
SSD_GROUPS = 4
SSD_STATE = 128
SSD_CHUNK = 128
LRU_HEADS = 16
LRU_BLOCK = 128
LRU_C = 8.0

DFT_N1 = 64
DFT_N2 = 128
DFT_K1 = DFT_N1 // 2 + 1
DFT_K1P = 40


def _cparams(*sem):
    return pltpu.CompilerParams(dimension_semantics=sem, vmem_limit_bytes=VMEM_LIMIT)


def _resident(shape, index_map):
    return pl.BlockSpec(shape, index_map, pipeline_mode=pl.Buffered(1))


def _sigmoid(x):
    return 0.5 + 0.5 * jnp.tanh(0.5 * x)


def _silu(x):
    return x * _sigmoid(x)


def _softplus(x):
    return jnp.maximum(x, 0.0) + jnp.log1p(jnp.exp(-jnp.abs(x)))


def _split3(x):
    hi = x.astype(BF16)
    r1 = x - hi.astype(F32)
    mid = r1.astype(BF16)
    lo = (r1 - mid.astype(F32)).astype(BF16)
    return hi, mid, lo


def _dot_sel(x, sel):
    hi, mid, lo = _split3(x)
    d = functools.partial(jnp.dot, preferred_element_type=F32)
    return d(hi, sel) + d(mid, sel) + d(lo, sel)


def _sel_dot(sel, x):
    hi, mid, lo = _split3(x)
    d = functools.partial(jnp.dot, preferred_element_type=F32)
    return d(sel, hi) + d(sel, mid) + d(sel, lo)


def _mod_kernel(c_ref, w_ref, b_ref, o_ref):
    cs = _silu(c_ref[...])
    o_ref[0] = jnp.dot(cs, w_ref[0], precision=HI, preferred_element_type=F32) + b_ref[0]


def _modulation(c_pad, mod_w, mod_b):
    depth, d, d3 = mod_w.shape
    bp = c_pad.shape[0]
    return pl.pallas_call(
        _mod_kernel,
        grid=(depth, d3 // d),
        in_specs=[
            pl.BlockSpec((bp, d), lambda i, j: (0, 0)),
            pl.BlockSpec((1, d, d), lambda i, j: (i, 0, j)),
            pl.BlockSpec((1, 1, d), lambda i, j: (i, 0, j)),
        ],
        out_specs=pl.BlockSpec((1, bp, d), lambda i, j: (i, 0, j)),
        out_shape=jax.ShapeDtypeStruct((depth, bp, d3), F32),
        compiler_params=_cparams("arbitrary", "arbitrary"),
        name="adaln_mod",
    )(c_pad, mod_w, mod_b.reshape(depth, 1, d3))


def _inproj_kernel(x_ref, sh_ref, sc_ref, g_ref, *refs, n_out):
    w_refs, o_refs = refs[:n_out], refs[n_out:]
    x = x_ref[0]
    ms = jnp.mean(x * x, axis=-1, keepdims=True)
    hn = (x * lax.rsqrt(ms + 1e-6)) * g_ref[...]
    hn = hn * (1.0 + sc_ref[0]) + sh_ref[0]
    hb = hn.astype(BF16)
    for w_ref, o_ref in zip(w_refs, o_refs):
        o_ref[0] = jnp.dot(hb, w_ref[...], preferred_element_type=F32).astype(o_ref.dtype)


def _in_projection(x, shift, scale, g, weights, dtypes, tl=256):
    b, l, d = x.shape
    n_out = len(weights)
    in_specs = [
        pl.BlockSpec((1, tl, d), lambda i, j: (i, j, 0)),
        pl.BlockSpec((1, 1, d), lambda i, j: (i, 0, 0)),
        pl.BlockSpec((1, 1, d), lambda i, j: (i, 0, 0)),
        pl.BlockSpec((1, d), lambda i, j: (0, 0)),
    ] + [_resident(w.shape, lambda i, j: (0, 0)) for w in weights]
    out_specs = [pl.BlockSpec((1, tl, w.shape[1]), lambda i, j: (i, j, 0)) for w in weights]
    out_shape = [jax.ShapeDtypeStruct((b, l, w.shape[1]), dt) for w, dt in zip(weights, dtypes)]
    return pl.pallas_call(
        functools.partial(_inproj_kernel, n_out=n_out),
        grid=(b, l // tl),
        in_specs=in_specs,
        out_specs=out_specs,
        out_shape=out_shape,
        compiler_params=_cparams("parallel", "parallel"),
        name="in_proj",
    )(x, shift, scale, g, *weights)


_CONV_ROWS = 256
_HALO = SUBLANES


def _fill_padded(pad_ref, src_ref, l):
    tc = pad_ref.shape[1]
    zeros = jnp.zeros((_HALO, tc), F32)
    pad_ref[pl.ds(0, _HALO), :] = zeros
    pad_ref[pl.ds(l + _HALO, _HALO), :] = zeros

    def body(i, carry):
        r = pl.multiple_of(i * _CONV_ROWS, _CONV_ROWS)
        pad_ref[pl.ds(r + _HALO, _CONV_ROWS), :] = src_ref[pl.ds(r, _CONV_ROWS), :].astype(F32)
        return carry

    lax.fori_loop(0, l // _CONV_ROWS, body, 0)


def _conv_rows_dyn(pad_ref, r, w, bias, taps, left):
    n = _CONV_ROWS + 2 * _HALO
    xe = pad_ref[pl.ds(r, n), :]
    acc = None
    for k in range(taps):
        off = k - left
        xs = xe if off == 0 else pltpu.roll(xe, (-off) % n, axis=0)
        term = w[k:k + 1, :] * xs[_HALO:_HALO + _CONV_ROWS, :]
        acc = term if acc is None else acc + term
    return acc + bias


def _conv_rows(pad_ref, r, w, bias, taps, left):
    acc = None
    for k in range(taps):
        term = w[k:k + 1, :] * pad_ref[pl.ds(r + _HALO + k - left, _CONV_ROWS), :]
        acc = term if acc is None else acc + term
    return acc + bias


def _hy_conv_kernel(x0_ref, x1_ref, v_ref, w0_ref, w1_ref, w2_ref, b0_ref, b1_ref, b2_ref,
                    x0c_ref, vf_ref, p0_ref, p1_ref, p2_ref, *, l):
    _fill_padded(p0_ref, x0_ref.at[0], l)
    _fill_padded(p1_ref, x1_ref.at[0], l)
    _fill_padded(p2_ref, v_ref.at[0], l)
    w0, w1, w2 = w0_ref[...], w1_ref[...], w2_ref[...]
    b0, b1, b2 = b0_ref[...], b1_ref[...], b2_ref[...]

    for r in range(0, l, _CONV_ROWS):
        x0c_ref[0, pl.ds(r, _CONV_ROWS), :] = _conv_rows(p0_ref, r, w0, b0, 3, 1).astype(x0c_ref.dtype)
        x1c = _conv_rows(p1_ref, r, w1, b1, 3, 1)
        vc = _conv_rows(p2_ref, r, w2, b2, 3, 1)
        vf_ref[0, pl.ds(r, _CONV_ROWS), :] = vc * x1c


def _hyena_short_conv(hyu, conv_w, conv_b, tc=128):
    b, l, c3 = hyu.shape
    c = c3 // 3
    nct = c // tc
    xspec = lambda o: pl.BlockSpec((1, l, tc), lambda i, j, o=o: (i, 0, j + o * nct))
    wspec = lambda o: pl.BlockSpec((3, tc), lambda i, j, o=o: (0, j + o * nct))
    bspec = lambda o: pl.BlockSpec((1, tc), lambda i, j, o=o: (0, j + o * nct))
    ospec = pl.BlockSpec((1, l, tc), lambda i, j: (i, 0, j))
    return pl.pallas_call(
        functools.partial(_hy_conv_kernel, l=l),
        grid=(b, nct),
        in_specs=[xspec(0), xspec(1), xspec(2), wspec(0), wspec(1), wspec(2),
                  bspec(0), bspec(1), bspec(2)],
        out_specs=[ospec, ospec],
        out_shape=[jax.ShapeDtypeStruct((b, l, c), BF16), jax.ShapeDtypeStruct((b, l, c), F32)],
        scratch_shapes=[pltpu.VMEM((l + 2 * _HALO, tc), F32)] * 3,
        compiler_params=_cparams("parallel", "parallel"),
        name="hyena_short_conv",
    )(hyu, hyu, hyu, conv_w, conv_w, conv_w, conv_b, conv_b, conv_b)


def _conv_silu_kernel(x_ref, w_ref, b_ref, o_ref, p_ref, *, l, taps, left):
    _fill_padded(p_ref, x_ref.at[0], l)
    w, bias = w_ref[...], b_ref[...]

    for r in range(0, l, _CONV_ROWS):
        o_ref[0, pl.ds(r, _CONV_ROWS), :] = _silu(_conv_rows(p_ref, r, w, bias, taps, left)).astype(o_ref.dtype)


def _conv_silu(x, conv_w, conv_b, tc=256):
    b, l, c = x.shape
    taps = conv_w.shape[0]
    return pl.pallas_call(
        functools.partial(_conv_silu_kernel, l=l, taps=taps, left=taps // 2),
        grid=(b, c // tc),
        in_specs=[pl.BlockSpec((1, l, tc), lambda i, j: (i, 0, j)),
                  pl.BlockSpec((taps, tc), lambda i, j: (0, j)),
                  pl.BlockSpec((1, tc), lambda i, j: (0, j))],
        out_specs=pl.BlockSpec((1, l, tc), lambda i, j: (i, 0, j)),
        out_shape=jax.ShapeDtypeStruct((b, l, c), BF16),
        scratch_shapes=[pltpu.VMEM((l + 2 * _HALO, tc), F32)],
        compiler_params=_cparams("parallel", "parallel"),
        name="ssd_conv_silu",
    )(x, conv_w, conv_b)


def _filter_kernel(z_ref, w1_ref, b1_ref, fr_ref, w2_ref, b2_ref, w3_ref, dl_ref,
                   hfg_ref, nrm_ref, *, tr, c):
    i = pl.program_id(0)
    z = z_ref[...]
    fr = fr_ref[...]
    d = functools.partial(jnp.dot, precision=HI, preferred_element_type=F32)
    h = jnp.sin(fr * (d(z, w1_ref[...]) + b1_ref[...]))
    h = jnp.sin(fr * (d(h, w2_ref[...]) + b2_ref[...]))
    h = d(h, w3_ref[...])
    window = jnp.exp(-z[:, 0:1] * dl_ref[...])
    hf = h[:, :c] * window
    row = lax.broadcasted_iota(jnp.int32, (tr, 1), 0) + i * tr
    g = jnp.where(row == 0, 0.0, h[:, c:] * window)
    hfg_ref[0] = hf
    hfg_ref[1] = g
    part = jnp.sum(jnp.abs(hf) + jnp.abs(g), axis=0, keepdims=True)

    @pl.when(i == 0)
    def _():
        nrm_ref[...] = part

    @pl.when(i != 0)
    def _():
        nrm_ref[...] = nrm_ref[...] + part


def _hyena_filter_taps(zfeat, w1, b1, freq, w2, b2, w3, deltas, tr=512):
    l, zp = zfeat.shape
    hp = w1.shape[1]
    c2 = w3.shape[1]
    c = c2 // 2
    full = lambda shape: pl.BlockSpec(shape, lambda i: (0,) * len(shape))
    return pl.pallas_call(
        functools.partial(_filter_kernel, tr=tr, c=c),
        grid=(l // tr,),
        in_specs=[pl.BlockSpec((tr, zp), lambda i: (i, 0)), full((zp, hp)), full((1, hp)),
                  full((1, hp)), full((hp, hp)), full((1, hp)), full((hp, c2)), full((1, c))],
        out_specs=[pl.BlockSpec((2, tr, c), lambda i: (0, i, 0)), full((1, c))],
        out_shape=[jax.ShapeDtypeStruct((2, l, c), F32), jax.ShapeDtypeStruct((1, c), F32)],
        compiler_params=_cparams("arbitrary"),
        name="hyena_filter_mlp",
    )(zfeat, w1, b1, freq, w2, b2, w3, deltas)


def _dft1_kernel(x_ref, fre_ref, fim_ref, re_ref, im_ref):
    x = x_ref[0]
    re_ref[0] = jnp.dot(fre_ref[...], x, precision=HI, preferred_element_type=F32)
    im_ref[0] = jnp.dot(fim_ref[...], x, precision=HI, preferred_element_type=F32)


def _dft_level1(x2, fre, fim, tlane=4096):
    b, nh, w = x2.shape
    kp = fre.shape[0]
    return pl.pallas_call(
        _dft1_kernel,
        grid=(b, w // tlane),
        in_specs=[pl.BlockSpec((1, nh, tlane), lambda i, j: (i, 0, j)),
                  pl.BlockSpec((kp, nh), lambda i, j: (0, 0)),
                  pl.BlockSpec((kp, nh), lambda i, j: (0, 0))],
        out_specs=[pl.BlockSpec((1, kp, tlane), lambda i, j: (i, 0, j))] * 2,
        out_shape=[jax.ShapeDtypeStruct((b, kp, w), F32)] * 2,
        compiler_params=_cparams("parallel", "parallel"),
        name="dft_level1",
    )(x2, fre, fim)


def _filter_spec_kernel(re_ref, im_ref, g_ref, nrm_ref, hre_ref, him_ref):
    n2 = re_ref.shape[1]
    g = g_ref[...]
    d = functools.partial(jnp.dot, precision=HI, preferred_element_type=F32)
    xf = d(g, jnp.concatenate([re_ref[0], im_ref[0]], axis=0))
    xg = d(g, jnp.concatenate([re_ref[1], im_ref[1]], axis=0))
    inv = 1.0 / nrm_ref[...]
    hre_ref[...] = (xf[:n2] + xg[:n2]) * inv
    him_ref[...] = (xf[n2:] - xg[n2:]) * inv


def _filter_spectrum(b1re, b1im, g32, nrm, tc=512):
    _, _, n2, c = b1re.shape
    k1 = g32.shape[0]
    return pl.pallas_call(
        _filter_spec_kernel,
        grid=(k1, c // tc),
        in_specs=[pl.BlockSpec((2, None, n2, tc), lambda k, j: (0, k, 0, j)),
                  pl.BlockSpec((2, None, n2, tc), lambda k, j: (0, k, 0, j)),
                  pl.BlockSpec((None, 2 * n2, 2 * n2), lambda k, j: (k, 0, 0)),
                  pl.BlockSpec((1, tc), lambda k, j: (0, j))],
        out_specs=[pl.BlockSpec((None, n2, tc), lambda k, j: (k, 0, j))] * 2,
        out_shape=[jax.ShapeDtypeStruct((k1, n2, c), F32)] * 2,
        compiler_params=_cparams("parallel", "parallel"),
        name="hyena_filter_spectrum",
    )(b1re, b1im, g32, nrm)


_LEVEL2_UNROLL = 4


def _fftconv_kernel(vf_ref, x0_ref, gt_ref, f1_ref, a1_ref, g_ref, gi_ref, hre_ref, him_ref, bias_ref,
                    o_ref, s_ref, y_ref):
    nh, ng, sub, tc = vf_ref.shape[1:]
    n2 = ng * sub
    nreal = s_ref.shape[0]
    k1 = nreal // 2 + 1
    d = functools.partial(jnp.dot, preferred_element_type=F32)

    f1 = f1_ref[...]
    for g in range(ng):
        x = vf_ref[0, :, g].reshape(nh * sub, tc).astype(BF16)
        s_ref[:, g] = d(f1, x).reshape(nreal, sub, tc)

    def level2(k, re, im):
        x = d(g_ref[k], jnp.concatenate([re, im], axis=0).astype(BF16))
        xr, xi = x[:n2], x[n2:]
        hr, hi = hre_ref[k], him_ref[k]
        y = jnp.concatenate([xr * hr - xi * hi, xr * hi + xi * hr], axis=0).astype(BF16)
        return d(gi_ref[k], y)

    zero = jnp.zeros((n2, tc), F32)
    for k in (0, k1 - 1):
        z = level2(k, s_ref[k].reshape(n2, tc), zero)
        s_ref[k] = z[:n2].reshape(ng, sub, tc)

    def body(k, carry):
        z = level2(k, s_ref[k].reshape(n2, tc), s_ref[k1 - 1 + k].reshape(n2, tc))
        s_ref[k] = z[:n2].reshape(ng, sub, tc)
        s_ref[k1 - 1 + k] = z[n2:].reshape(ng, sub, tc)
        return carry

    lax.fori_loop(1, k1 - 1, body, 0, unroll=_LEVEL2_UNROLL)

    a1 = a1_ref[...]
    for g in range(ng):
        z = s_ref[:, g].reshape(nreal * sub, tc).astype(BF16)
        y_ref[:, g] = d(a1, z).reshape(nh, sub, tc)

    bias = bias_ref[...]
    step = 2
    rows = step * n2

    def epilogue(i, carry):
        r = pl.multiple_of(i * rows, rows)
        y = y_ref[pl.ds(i * step, step)].reshape(rows, tc)
        vf = vf_ref[0, pl.ds(i * step, step)].reshape(rows, tc)
        x0 = x0_ref[0, pl.ds(r, rows), :].astype(F32)
        gt = gt_ref[0, pl.ds(r, rows), :].astype(F32)
        o_ref[0, pl.ds(r, rows), :] = ((x0 * (y + vf * bias)) * _silu(gt)).astype(o_ref.dtype)
        return carry

    lax.fori_loop(0, nh // step, epilogue, 0)


def _hyena_fftconv(vf, x0c, gate, f1k, a1k, g, ginv, hre, him, bias, tc=256):
    b, l, c = vf.shape
    nh, sub = DFT_N1 // 2, SUBLANES
    ng = DFT_N2 // sub
    kb = g.shape[0]
    n2 = DFT_N2
    rows = pl.BlockSpec((1, l, tc), lambda j, i: (i, 0, j))
    return pl.pallas_call(
        _fftconv_kernel,
        grid=(c // tc, b),
        in_specs=[pl.BlockSpec((1, nh, ng, sub, tc), lambda j, i: (i, 0, 0, 0, j)), rows, rows,
                  _resident(f1k.shape, lambda j, i: (0, 0)), _resident(a1k.shape, lambda j, i: (0, 0)),
                  _resident(g.shape, lambda j, i: (0, 0, 0)), _resident(ginv.shape, lambda j, i: (0, 0, 0)),
                  _resident((kb, n2, tc), lambda j, i: (0, 0, j)), _resident((kb, n2, tc), lambda j, i: (0, 0, j)),
                  pl.BlockSpec((1, tc), lambda j, i: (0, j))],
        out_specs=rows,
        out_shape=jax.ShapeDtypeStruct((b, l, c), BF16),
        scratch_shapes=[pltpu.VMEM((DFT_N1, ng, sub, tc), F32), pltpu.VMEM((nh, ng, sub, tc), F32)],
        compiler_params=_cparams("arbitrary", "arbitrary"),
        name="hyena_fftconv",
    )(vf.reshape(b, nh, ng, sub, c), x0c, gate, f1k, a1k, g, ginv, hre, him, bias)


@functools.lru_cache(maxsize=None)
def _dft_tables():
    n1, n2, k1, kp = DFT_N1, DFT_N2, DFT_K1, DFT_K1P
    n = n1 * n2
    nh = n1 // 2
    kk = np.arange(k1)[:, None]
    nn = np.arange(nh)[None, :]
    ang = 2.0 * np.pi * kk * nn / n1
    fre = np.zeros((kp, nh)); fim = np.zeros((kp, nh))
    fre[:k1] = np.cos(ang); fim[:k1] = -np.sin(ang)
    m = np.arange(n2)[None, None, :]
    k2 = np.arange(n2)[None, :, None]
    ka = np.arange(k1)[:, None, None]
    ph = -2.0 * np.pi * m * (ka / n + k2 / n2)
    gr, gi = np.cos(ph), np.sin(ph)
    g = np.concatenate([np.concatenate([gr, -gi], axis=2), np.concatenate([gi, gr], axis=2)], axis=1)
    grt, git = np.swapaxes(gr, 1, 2), np.swapaxes(gi, 1, 2)
    ginv = np.concatenate([np.concatenate([grt, git], axis=2), np.concatenate([-git, grt], axis=2)], axis=1)
    wk = np.where((np.arange(k1) == 0) | (np.arange(k1) == n1 // 2), 1.0, 2.0)[None, :]
    ang2 = 2.0 * np.pi * np.arange(nh)[:, None] * np.arange(k1)[None, :] / n1
    are = np.zeros((nh, kp)); aim = np.zeros((nh, kp))
    are[:, :k1] = wk * np.cos(ang2) / n
    aim[:, :k1] = -wk * np.sin(ang2) / n
    eye = np.eye(SUBLANES)
    f1k = np.kron(np.concatenate([fre[:k1], fim[1:k1 - 1]], axis=0), eye)
    a1k = np.kron(np.concatenate([are[:, :k1], aim[:, 1:k1 - 1]], axis=1), eye)
    f = lambda a: np.asarray(a, np.float32)
    return f(fre), f(fim), f(g), f(ginv), f(f1k), f(a1k)


def _position_features(l):
    f32 = F32
    pos = jnp.arange(l, dtype=f32)[:, None]
    t = pos / max(l - 1, 1)
    bands = (HY_EMB - 1) // 2
    f = jnp.linspace(1e-4, bands - 1, bands, dtype=f32)[None, :]
    ang = f * pos * (2.0 * math.pi / l)
    z = jnp.concatenate([t, jnp.cos(ang), -jnp.sin(ang)], axis=-1)
    return jnp.pad(z, ((0, 0), (0, LANES - HY_EMB)))


def _pad2(a, rows, cols):
    return jnp.pad(a, ((0, rows - a.shape[0]), (0, cols - a.shape[1])))


def _hyena_spectrum(l, c, fw1, fb1, fw2, fb2, fw3, freq):
    fre, fim, g32, _, _, _ = _dft_tables()
    hp = LANES
    zfeat = _position_features(l)
    w1 = _pad2(fw1, LANES, hp)
    w2 = _pad2(fw2, hp, hp)
    w3 = _pad2(fw3, hp, fw3.shape[1])
    row = lambda v: _pad2(v[None, :], 1, hp)
    deltas = jnp.abs(jnp.linspace(math.log(HY_DECAY_TARGET) / HY_LONG_DECAY_PCT,
                                  math.log(HY_DECAY_TARGET) / HY_SHORT_DECAY_PCT, c, dtype=F32))[None, :]
    hfg, nrm = _hyena_filter_taps(zfeat, w1, row(fb1), row(freq), w2, row(fb2), w3, deltas)
    b1re, b1im = _dft_level1(hfg.reshape(2, DFT_N1 // 2, DFT_N2 * c), jnp.asarray(fre), jnp.asarray(fim))
    shp = (2, DFT_K1P, DFT_N2, c)
    return _filter_spectrum(b1re.reshape(shp), b1im.reshape(shp), jnp.asarray(g32), nrm)


def _hyena_branch(hyu, hy_gate, conv_w, conv_b, hre, him, bias):
    b, l, c3 = hyu.shape
    c = c3 // 3
    _, _, g32, ginv32, f1k, a1k = _dft_tables()
    x0c, vf = _hyena_short_conv(hyu, conv_w, conv_b[None, :])
    bf = lambda a: jnp.asarray(a, dtype=BF16)
    return _hyena_fftconv(vf, x0c, hy_gate, bf(f1k), bf(a1k), bf(g32), bf(ginv32), hre, him, bias[None, :])


def _ssd_kernel(xs_ref, bm_ref, cm_ref, dt_ref, dtb_ref, alog_ref, rep_ref, dsk_ref, mask_ref,
                y_ref, prev_ref, *, reverse, lane0, chunks):
    q, n, hd = SSD_CHUNK, SSD_STATE, SSD_HEAD_DIM
    gw = (SSD_HEADS // SSD_GROUPS) * hd

    @pl.when(pl.program_id(1) == 0)
    def _():
        prev_ref[...] = jnp.zeros_like(prev_ref)

    mask = mask_ref[...]
    maskb = mask > 0
    rep = rep_ref[...]
    neg_a = -jnp.exp(alog_ref[...])
    lane = lax.broadcasted_iota(jnp.int32, (1, LANES), 1)
    lo_half = (lane < hd).astype(F32)
    hi_half = 1.0 - lo_half
    d = functools.partial(jnp.dot, preferred_element_type=F32)
    order = range(chunks - 1, -1, -1) if reverse else range(chunks)
    for k in order:
        rows = pl.ds(k * q, q)
        xs = xs_ref[0, rows, :].astype(F32)
        bm = bm_ref[0, rows, :]
        cm = cm_ref[0, rows, :]
        dt = _softplus(dt_ref[0, rows, :] + dtb_ref[...])
        la = dt * neg_a
        cum = _sel_dot(mask, la)
        cum_t = cum.T
        dt_x = _dot_sel(dt, rep)
        cum_x = _dot_sel(cum, rep)
        edge = cum_x[0:1, :] if reverse else cum_x[q - 1:q, :]
        xdt = xs * dt_x
        d_in = jnp.exp(cum_x)
        xsd = (xdt * jnp.exp(edge - cum_x)).astype(BF16)
        cd = jnp.exp(edge)
        ys = []
        for g in range(SSD_GROUPS):
            cg = cm[:, g * n:(g + 1) * n].astype(BF16)
            bg = bm[:, g * n:(g + 1) * n].astype(BF16)
            cb = lax.dot_general(cg, bg, (((1,), (1,)), ((), ())), preferred_element_type=F32)
            for pair in range(gw // LANES):
                lo = g * gw + pair * LANES
                xp = xdt[:, lo:lo + LANES]
                acc = None
                for half, sel in ((0, lo_half), (1, hi_half)):
                    h = lo // hd + half
                    col = cum[:, lane0 + h:lane0 + h + 1]
                    rw = cum_t[lane0 + h:lane0 + h + 1, :]
                    seg = jnp.exp(jnp.where(maskb, col - rw, -jnp.inf))
                    term = d((cb * seg).astype(BF16), (xp * sel).astype(BF16))
                    acc = term if acc is None else acc + term
                ys.append(acc)
            sl = slice(g * gw, (g + 1) * gw)
            prev_g = prev_ref[:, sl]
            y_off = d(cg, prev_g.astype(BF16)) * d_in[:, sl]
            st = lax.dot_general(bg, xsd[:, sl], (((0,), (0,)), ((), ())), preferred_element_type=F32)
            prev_ref[:, sl] = prev_g * cd[:, sl] + st
            for pair in range(gw // LANES):
                idx = g * (gw // LANES) + pair
                ys[idx] = ys[idx] + y_off[:, pair * LANES:(pair + 1) * LANES]
        y = jnp.concatenate(ys, axis=1)
        if not reverse:
            y = y + xs * dsk_ref[...]
        y_ref[0, rows, :] = y


def _ssd_direction(xbc, dt_raw, dtb, alog, rep, dskip, mask, *, reverse, ts=512):
    b, l, _ = xbc.shape
    hp = SSD_HEADS * SSD_HEAD_DIM
    gn = SSD_GROUPS * SSD_STATE
    nt = l // ts
    tmap = (lambda j: nt - 1 - j) if reverse else (lambda j: j)
    full = lambda shape: pl.BlockSpec(shape, lambda i, j: (0,) * len(shape))
    return pl.pallas_call(
        functools.partial(_ssd_kernel, reverse=reverse, lane0=SSD_HEADS if reverse else 0,
                          chunks=ts // SSD_CHUNK),
        grid=(b, nt),
        in_specs=[pl.BlockSpec((1, ts, hp), lambda i, j: (i, tmap(j), 0)),
                  pl.BlockSpec((1, ts, gn), lambda i, j: (i, tmap(j), hp // gn)),
                  pl.BlockSpec((1, ts, gn), lambda i, j: (i, tmap(j), hp // gn + 1)),
                  pl.BlockSpec((1, ts, LANES), lambda i, j: (i, tmap(j), 0)),
                  full((1, LANES)), full((1, LANES)), full((LANES, hp)), full((1, hp)),
                  full((SSD_CHUNK, SSD_CHUNK))],
        out_specs=pl.BlockSpec((1, ts, hp), lambda i, j: (i, tmap(j), 0)),
        out_shape=jax.ShapeDtypeStruct((b, l, hp), F32),
        scratch_shapes=[pltpu.VMEM((SSD_STATE, hp), F32)],
        compiler_params=_cparams("parallel", "arbitrary"),
        name="ssd_bwd" if reverse else "ssd_fwd",
    )(xbc, xbc, xbc, dt_raw, dtb, alog, rep, dskip, mask)


def _ssd_branch(xbc, dt_raw, conv_w, conv_b, dt_bias, a_log, d_skip):
    xc = _conv_silu(xbc, conv_w, conv_b[None, :])
    pad_row = lambda v: _pad2(v.reshape(1, -1), 1, LANES)
    dtb, alog = pad_row(dt_bias), pad_row(a_log)
    dskip = jnp.repeat(d_skip, SSD_HEAD_DIM)[None, :]
    hp = SSD_HEADS * SSD_HEAD_DIM
    lane_head = np.arange(hp)[None, :] // SSD_HEAD_DIM
    src = np.arange(LANES)[:, None]
    idx = np.arange(SSD_CHUNK)
    outs = []
    for reverse in (False, True):
        rep = jnp.asarray(src == lane_head + (SSD_HEADS if reverse else 0), dtype=BF16)
        allowed = (idx[None, :] >= idx[:, None]) if reverse else (idx[None, :] <= idx[:, None])
        mask = jnp.asarray(allowed, dtype=BF16)
        outs.append(_ssd_direction(xc, dt_raw, dtb, alog, rep, dskip, mask, reverse=reverse))
    return outs


_LRU_ROWS = 256


def _lru_kernel(x_ref, gt_ref, cw_ref, cb_ref, wa_ref, wx_ref, ba_ref, bx_ref, lam_ref,
                o_ref, pad_ref, af_ref, uf_ref, ab_ref, ub_ref, *, l, heads):
    _fill_padded(pad_ref, x_ref.at[0], l)
    cw, cb = cw_ref[...], cb_ref[...]
    e0 = (-0.5 * LRU_C / math.log(2.0)) * _softplus(-lam_ref[...])
    ba, bx = ba_ref[...], bx_ref[...]
    d = functools.partial(jnp.dot, preferred_element_type=F32)
    a_refs, u_refs = (af_ref, ab_ref), (uf_ref, ub_ref)

    def gates(i, carry):
        r = pl.multiple_of(i * _CONV_ROWS, _CONV_ROWS)
        xc = _conv_rows_dyn(pad_ref, r, cw, cb, 4, 2)
        for hd in range(heads):
            ls = slice(hd * LRU_BLOCK, (hd + 1) * LRU_BLOCK)
            xh = xc[:, ls]
            xb = xh.astype(BF16)
            xh2 = 0.5 * xh
            for dr in range(2):
                tr = jnp.tanh(0.5 * (d(xb, wa_ref[dr, hd]) + ba[dr:dr + 1, ls]))
                ti = jnp.tanh(0.5 * (d(xb, wx_ref[dr, hd]) + bx[dr:dr + 1, ls]))
                e = e0[dr:dr + 1, ls]
                a = jnp.exp2(e + e * tr)
                s = 1.0 - a * a
                root = jnp.where(s > 0.0, s * lax.rsqrt(s), 0.0)
                u = (xh2 + xh2 * ti) * root
                a_refs[dr][hd, pl.ds(r, _CONV_ROWS), :] = a
                u_refs[dr][hd, pl.ds(r, _CONV_ROWS), :] = u
        return carry

    lax.fori_loop(0, l // _CONV_ROWS, gates, 0)

    row = lax.broadcasted_iota(jnp.int32, (SUBLANES, LRU_BLOCK), 0)
    blk = SUBLANES * SUBLANES
    nblk = l // blk

    def sublane_scan(a, u, reverse):
        for s in (1, 2, 4):
            sh = (SUBLANES - s) if reverse else s
            valid = (row < SUBLANES - s) if reverse else (row >= s)
            a_sh = jnp.where(valid, pltpu.roll(a, sh, axis=0), 1.0)
            u_sh = jnp.where(valid, pltpu.roll(u, sh, axis=0), 0.0)
            u = u + a * u_sh
            a = a * a_sh
        return a, u

    def block_scan(a_ref, u_ref, r, h_in, reverse):
        order = range(SUBLANES - 1, -1, -1) if reverse else range(SUBLANES)
        hs, ps = {}, {}
        h = p = None
        for k in order:
            rows = pl.ds(r + k, SUBLANES, stride=SUBLANES)
            a, u = a_ref[rows, :], u_ref[rows, :]
            h = u if h is None else a * h + u
            p = a if p is None else a * p
            hs[k], ps[k] = h, p
        sp, se = sublane_scan(p, h, reverse)
        full = se + sp * h_in
        edge = 0 if reverse else SUBLANES - 1
        h_out = jnp.broadcast_to(full[edge:edge + 1, :], (SUBLANES, LRU_BLOCK))
        if reverse:
            carry = jnp.where(row == SUBLANES - 1, h_in, pltpu.roll(full, SUBLANES - 1, axis=0))
        else:
            carry = jnp.where(row == 0, h_in, pltpu.roll(full, 1, axis=0))
        for k in order:
            u_ref[pl.ds(r + k, SUBLANES, stride=SUBLANES), :] = hs[k] + ps[k] * carry
        return h_out

    def scan(i, carry):
        rf = pl.multiple_of(i * blk, blk)
        rb = pl.multiple_of((nblk - 1 - i) * blk, blk)
        out = []
        for hd in range(heads):
            hf, hb = carry[2 * hd], carry[2 * hd + 1]
            out.append(block_scan(af_ref.at[hd], uf_ref.at[hd], rf, hf, False))
            out.append(block_scan(ab_ref.at[hd], ub_ref.at[hd], rb, hb, True))
        return tuple(out)

    zero = jnp.zeros((SUBLANES, LRU_BLOCK), F32)
    lax.fori_loop(0, nblk, scan, (zero,) * (2 * heads), unroll=2)

    def combine(i, carry):
        r = pl.multiple_of(i * _LRU_ROWS, _LRU_ROWS)
        rows = pl.ds(r, _LRU_ROWS)
        h = jnp.concatenate([uf_ref[hd, rows, :] + ub_ref[hd, rows, :] for hd in range(heads)], axis=1)
        o_ref[0, rows, :] = (h * _silu(gt_ref[0, rows, :].astype(F32))).astype(o_ref.dtype)
        return carry

    lax.fori_loop(0, l // _LRU_ROWS, combine, 0)


def _rglru_mixer(xb, gate, conv_w, conv_b, w_a, w_x, b_a, b_x, lam, heads=2):
    b, l, w = xb.shape
    tw = heads * LRU_BLOCK
    xspec = pl.BlockSpec((1, l, tw), lambda i, j: (i, 0, j))
    wspec = pl.BlockSpec((2, heads, LRU_BLOCK, LRU_BLOCK), lambda i, j: (0, j, 0, 0))
    vspec = pl.BlockSpec((2, tw), lambda i, j: (0, j))
    seq = pltpu.VMEM((heads, l, LRU_BLOCK), F32)
    return pl.pallas_call(
        functools.partial(_lru_kernel, l=l, heads=heads),
        grid=(b, w // tw),
        in_specs=[xspec, xspec,
                  pl.BlockSpec((4, tw), lambda i, j: (0, j)),
                  pl.BlockSpec((1, tw), lambda i, j: (0, j)),
                  wspec, wspec, vspec, vspec, vspec],
        out_specs=xspec,
        out_shape=jax.ShapeDtypeStruct((b, l, w), BF16),
        scratch_shapes=[pltpu.VMEM((l + 2 * _HALO, tw), F32), seq, seq, seq, seq],
        compiler_params=_cparams("parallel", "parallel"),
        name="rglru",
    )(xb, gate, conv_w, conv_b[None, :], w_a.astype(BF16), w_x.astype(BF16), b_a, b_x, lam)


def _rms(x, g, eps):
    return (x * lax.rsqrt(jnp.mean(x * x, axis=-1, keepdims=True) + eps)) * g


def _outproj_even_kernel(x_ref, ya_ref, y0_ref, y1_ref, z_ref, gm_ref, ng_ref, wa_ref, wb_ref, o_ref):
    yb = (y0_ref[0] + y1_ref[0]) * _silu(z_ref[0].astype(F32))
    ng = ng_ref[...]
    gw = yb.shape[1] // SSD_GROUPS
    parts = [_rms(yb[:, g * gw:(g + 1) * gw], ng[:, g * gw:(g + 1) * gw], 1e-5) for g in range(SSD_GROUPS)]
    ybn = jnp.concatenate(parts, axis=1)
    d = functools.partial(jnp.dot, preferred_element_type=F32)
    acc = d(ya_ref[0].astype(BF16), wa_ref[...]) + d(ybn.astype(BF16), wb_ref[...])
    o_ref[0] = x_ref[0] + gm_ref[0] * acc


def _out_projection_even(x, ya, y0, y1, z, gate_mod, norm_g, w_out, tl=512):
    b, l, dm = x.shape
    c = ya.shape[2]
    rows = lambda n: pl.BlockSpec((1, tl, n), lambda i, j: (i, j, 0))
    wa, wb = w_out[:c].astype(BF16), w_out[c:].astype(BF16)
    return pl.pallas_call(
        _outproj_even_kernel,
        grid=(b, l // tl),
        in_specs=[rows(dm), rows(c), rows(c), rows(c), rows(c),
                  pl.BlockSpec((1, 1, dm), lambda i, j: (i, 0, 0)),
                  pl.BlockSpec((1, c), lambda i, j: (0, 0)),
                  _resident(wa.shape, lambda i, j: (0, 0)), _resident(wb.shape, lambda i, j: (0, 0))],
        out_specs=rows(dm),
        out_shape=jax.ShapeDtypeStruct((b, l, dm), F32),
        compiler_params=_cparams("parallel", "parallel"),
        name="out_proj_even",
    )(x, ya, y0, y1, z, gate_mod, norm_g, wa, wb)


def _outproj_odd_kernel(x_ref, y_ref, gm_ref, w_ref, fg_ref, o_ref, *, final):
    acc = jnp.dot(y_ref[0].astype(BF16), w_ref[...], preferred_element_type=F32)
    xn = x_ref[0] + gm_ref[0] * acc
    o_ref[0] = _rms(xn, fg_ref[...], 1e-6) if final else xn


def _out_projection_odd(x, y, gate_mod, w_out, final_g, final, tl=512):
    b, l, dm = x.shape
    w = y.shape[2]
    rows = lambda n: pl.BlockSpec((1, tl, n), lambda i, j: (i, j, 0))
    wb = w_out.astype(BF16)
    return pl.pallas_call(
        functools.partial(_outproj_odd_kernel, final=final),
        grid=(b, l // tl),
        in_specs=[rows(dm), rows(w),
                  pl.BlockSpec((1, 1, dm), lambda i, j: (i, 0, 0)),
                  _resident(wb.shape, lambda i, j: (0, 0)),
                  pl.BlockSpec((1, dm), lambda i, j: (0, 0))],
        out_specs=rows(dm),
        out_shape=jax.ShapeDtypeStruct((b, l, dm), F32),
        compiler_params=_cparams("parallel", "parallel"),
        name="out_proj_odd",
    )(x, y, gate_mod, wb, final_g)


def kernel(x_prompt, x_sample, c_prompt, c_sample, mod_w, mod_b, norm_g, final_g, ev_w_in, ev_w_out, hy_conv_w, hy_conv_b, hy_fw1, hy_fb1, hy_fw2, hy_fb2, hy_fw3, hy_freq, hy_bias, ssd_conv_w, ssd_conv_b, ssd_dt_bias, ssd_A_log, ssd_D, ssd_norm_g, od_w_in, od_w_out, lru_conv_w, lru_conv_b, lru_w_a, lru_b_a, lru_w_x, lru_b_x, lru_lam):
    bp = x_prompt.shape[0]
    x = jnp.concatenate([x_prompt, x_sample], axis=0)
    c = jnp.concatenate([c_prompt, c_sample], axis=0)
    b, l, dm = x.shape
    depth = mod_w.shape[0]
    assert 2 * l == DFT_N1 * DFT_N2

    b_pad = -(-b // SUBLANES) * SUBLANES
    mods = _modulation(jnp.pad(c, ((0, b_pad - b), (0, 0))), mod_w, mod_b)[:, :b]
    shift, scale, gate_mod = (mods[:, :, k * dm:(k + 1) * dm].reshape(depth, b, 1, dm) for k in range(3))

    hyc = hy_bias.shape[1]
    ssw = ssd_norm_g.shape[1]
    xbw = ssd_conv_w.shape[2]
    o1, o2, o3, o4 = 3 * hyc, 4 * hyc, 4 * hyc + ssw, 4 * hyc + ssw + xbw

    for i in range(depth):
        j = i // 2
        g = norm_g[i][None, :]
        if i % 2 == 0:
            w_in = ev_w_in[j].astype(BF16)
            w_dt = jnp.pad(w_in[:, o4:], ((0, 0), (0, LANES - (w_in.shape[1] - o4))))
            hyu, hyg, z, xbc, dt_raw = _in_projection(
                x, shift[i], scale[i], g, [w_in[:, :o1], w_in[:, o1:o2], w_in[:, o2:o3], w_in[:, o3:o4], w_dt],
                [BF16, BF16, BF16, BF16, F32])
            hre, him = _hyena_spectrum(l, hyc, hy_fw1[j], hy_fb1[j], hy_fw2[j], hy_fb2[j], hy_fw3[j], hy_freq[j])
            ya = _hyena_branch(hyu, hyg, hy_conv_w[j], hy_conv_b[j], hre, him, hy_bias[j])
            y0, y1 = _ssd_branch(xbc, dt_raw, ssd_conv_w[j], ssd_conv_b[j], ssd_dt_bias[j], ssd_A_log[j], ssd_D[j])
            x = _out_projection_even(x, ya, y0, y1, z, gate_mod[i], ssd_norm_g[j][None, :], ev_w_out[j])
        else:
            w_in = od_w_in[j].astype(BF16)
            half = w_in.shape[1] // 2
            xb, gt = _in_projection(x, shift[i], scale[i], g, [w_in[:, :half], w_in[:, half:]], [BF16, BF16])
            y = _rglru_mixer(xb, gt, lru_conv_w[j], lru_conv_b[j], lru_w_a[j], lru_w_x[j],
                             lru_b_a[j], lru_b_x[j], lru_lam[j])
            x = _out_projection_odd(x, y, gate_mod[i], od_w_out[j], final_g[None, :], final=(i == depth - 1))
    if depth % 2 == 1:
        raise NotImplementedError("final norm is fused into the last (odd) layer's output projection")
    return x[:bp], x[bp:]
```

```python
import functools
import math

import numpy as np
import jax
import jax.numpy as jnp
from jax import lax
from jax.experimental import pallas as pl
from jax.experimental.pallas import tpu as pltpu

F32 = jnp.float32
BF16 = jnp.bfloat16
HI = lax.Precision.HIGHEST

V7X_VMEM_BYTES = 64 * 1024 * 1024
VMEM_LIMIT = V7X_VMEM_BYTES - 8 * 1024 * 1024
LANES = 128
SUBLANES = 8

HY_EMB = 33
HY_SHORT_DECAY_PCT = 0.3
HY_LONG_DECAY_PCT = 1.5
HY_DECAY_TARGET = 1e-2
SSD_HEADS = 16
SSD_HEAD_DIM = 64
SSD_GROUPS = 4
SSD_STATE = 128
SSD_CHUNK = 128
LRU_HEADS = 16
LRU_BLOCK = 128
LRU_C = 8.0

DFT_N1 = 64
DFT_N2 = 128
DFT_K1 = DFT_N1 // 2 + 1
DFT_K1P = 40


def _cparams(*sem):
    return pltpu.CompilerParams(dimension_semantics=sem, vmem_limit_bytes=VMEM_LIMIT)


def _resident(shape, index_map):
    return pl.BlockSpec(shape, index_map, pipeline_mode=pl.Buffered(1))


def _sigmoid(x):
    return 0.5 + 0.5 * jnp.tanh(0.5 * x)


def _silu(x):
    return x * _sigmoid(x)


def _softplus(x):
    return jnp.maximum(x, 0.0) + jnp.log1p(jnp.exp(-jnp.abs(x)))


def _split3(x):
    hi = x.astype(BF16)
    r1 = x - hi.astype(F32)
    mid = r1.astype(BF16)
    lo = (r1 - mid.astype(F32)).astype(BF16)
    return hi, mid, lo


def _sel_dot(sel, x):
    hi, mid, lo = _split3(x)
    d = functools.partial(jnp.dot, preferred_element_type=F32)
    return d(sel, hi) + d(sel, mid) + d(sel, lo)


def _mod_kernel(c_ref, w_ref, b_ref, o_ref):
    cs = _silu(c_ref[...])
    o_ref[0] = jnp.dot(cs, w_ref[0], precision=HI, preferred_element_type=F32) + b_ref[0]


def _modulation(c_pad, mod_w, mod_b):
    depth, d, d3 = mod_w.shape
    bp = c_pad.shape[0]
    return pl.pallas_call(
        _mod_kernel,
        grid=(depth, d3 // d),
        in_specs=[
            pl.BlockSpec((bp, d), lambda i, j: (0, 0)),
            pl.BlockSpec((1, d, d), lambda i, j: (i, 0, j)),
            pl.BlockSpec((1, 1, d), lambda i, j: (i, 0, j)),
        ],
        out_specs=pl.BlockSpec((1, bp, d), lambda i, j: (i, 0, j)),
        out_shape=jax.ShapeDtypeStruct((depth, bp, d3), F32),
        compiler_params=_cparams("arbitrary", "arbitrary"),
        name="adaln_mod",
    )(c_pad, mod_w, mod_b.reshape(depth, 1, d3))


def _pair_row_maps(bp, nj):
    first = lambda i, j: (jnp.minimum(i, bp - 1), jnp.where(i < bp, j, nj - 1), 0)
    second = lambda i, j: (jnp.maximum(i - bp, 0), jnp.where(i < bp, 0, j), 0)
    return first, second


def _load_rows(x_refs, bp):
    if len(x_refs) == 1:
        return x_refs[0][0]
    return jnp.where(pl.program_id(0) < bp, x_refs[0][0], x_refs[1][0])


def _row_specs(xs, tl):
    l, d = xs[0].shape[1:]
    if len(xs) == 1:
        return [pl.BlockSpec((1, tl, d), lambda i, j: (i, j, 0))]
    return [pl.BlockSpec((1, tl, d), m) for m in _pair_row_maps(xs[0].shape[0], l // tl)]


def _inproj_kernel(*refs, n_x, bp, n_out):
    x_refs, (sh_ref, sc_ref, g_ref), refs = refs[:n_x], refs[n_x:n_x + 3], refs[n_x + 3:]
    w_refs, o_refs = refs[:n_out], refs[n_out:]
    x = _load_rows(x_refs, bp)
    ms = jnp.mean(x * x, axis=-1, keepdims=True)
    hn = (x * lax.rsqrt(ms + 1e-6)) * g_ref[...]
    hn = hn * (1.0 + sc_ref[0]) + sh_ref[0]
    hb = hn.astype(BF16)
    for w_ref, o_ref in zip(w_refs, o_refs):
        o_ref[0] = jnp.dot(hb, w_ref[...], preferred_element_type=F32).astype(o_ref.dtype)


def _in_projection(x, shift, scale, g, weights, dtypes, tl=512):
    b = sum(a.shape[0] for a in x)
    l, d = x[0].shape[1:]
    n_out = len(weights)
    in_specs = _row_specs(x, tl) + [
        pl.BlockSpec((1, 1, d), lambda i, j: (i, 0, 0)),
        pl.BlockSpec((1, 1, d), lambda i, j: (i, 0, 0)),
        pl.BlockSpec((1, d), lambda i, j: (0, 0)),
    ] + [_resident(w.shape, lambda i, j: (0, 0)) for w in weights]
    out_specs = [pl.BlockSpec((1, tl, w.shape[1]), lambda i, j: (i, j, 0)) for w in weights]
    out_shape = [jax.ShapeDtypeStruct((b, l, w.shape[1]), dt) for w, dt in zip(weights, dtypes)]
    return pl.pallas_call(
        functools.partial(_inproj_kernel, n_x=len(x), bp=x[0].shape[0], n_out=n_out),
        grid=(b, l // tl),
        in_specs=in_specs,
        out_specs=out_specs,
        out_shape=out_shape,
        compiler_params=_cparams("arbitrary", "arbitrary"),
        name="in_proj",
    )(*x, shift, scale, g, *weights)


_CONV_ROWS = 256
_HALO = SUBLANES


def _fill_padded(pad_ref, src_ref, l):
    tc = pad_ref.shape[1]
    zeros = jnp.zeros((_HALO, tc), F32)
    pad_ref[pl.ds(0, _HALO), :] = zeros
    pad_ref[pl.ds(l + _HALO, _HALO), :] = zeros

    def body(i, carry):
        r = pl.multiple_of(i * _CONV_ROWS, _CONV_ROWS)
        pad_ref[pl.ds(r + _HALO, _CONV_ROWS), :] = src_ref[pl.ds(r, _CONV_ROWS), :].astype(F32)
        return carry

    lax.fori_loop(0, l // _CONV_ROWS, body, 0)


def _conv_rows_dyn(pad_ref, r, w, bias, taps, left):
    n = _CONV_ROWS + 2 * _HALO
    xe = pad_ref[pl.ds(r, n), :]
    acc = None
    for k in range(taps):
        off = k - left
        xs = xe if off == 0 else pltpu.roll(xe, (-off) % n, axis=0)
        term = w[k:k + 1, :] * xs[_HALO:_HALO + _CONV_ROWS, :]
        acc = term if acc is None else acc + term
    return acc + bias


def _conv_rows(pad_ref, r, w, bias, taps, left):
    acc = None
    for k in range(taps):
        term = w[k:k + 1, :] * pad_ref[pl.ds(r + _HALO + k - left, _CONV_ROWS), :]
        acc = term if acc is None else acc + term
    return acc + bias


def _hy_conv_kernel(x0_ref, x1_ref, v_ref, w0_ref, w1_ref, w2_ref, b0_ref, b1_ref, b2_ref,
                    x0c_ref, vf_ref, p0_ref, p1_ref, p2_ref, *, l):
    _fill_padded(p0_ref, x0_ref.at[0], l)
    _fill_padded(p1_ref, x1_ref.at[0], l)
    _fill_padded(p2_ref, v_ref.at[0], l)
    w0, w1, w2 = w0_ref[...], w1_ref[...], w2_ref[...]
    b0, b1, b2 = b0_ref[...], b1_ref[...], b2_ref[...]

    for r in range(0, l, _CONV_ROWS):
        x0c_ref[0, pl.ds(r, _CONV_ROWS), :] = _conv_rows(p0_ref, r, w0, b0, 3, 1).astype(x0c_ref.dtype)
        x1c = _conv_rows(p1_ref, r, w1, b1, 3, 1)
        vc = _conv_rows(p2_ref, r, w2, b2, 3, 1)
        vf_ref[0, pl.ds(r, _CONV_ROWS), :] = vc * x1c


def _hyena_short_conv(hyu, conv_w, conv_b, tc=128):
    b, l, c3 = hyu.shape
    c = c3 // 3
    nct = c // tc
    xspec = lambda o: pl.BlockSpec((1, l, tc), lambda i, j, o=o: (i, 0, j + o * nct))
    wspec = lambda o: pl.BlockSpec((3, tc), lambda i, j, o=o: (0, j + o * nct))
    bspec = lambda o: pl.BlockSpec((1, tc), lambda i, j, o=o: (0, j + o * nct))
    ospec = pl.BlockSpec((1, l, tc), lambda i, j: (i, 0, j))
    return pl.pallas_call(
        functools.partial(_hy_conv_kernel, l=l),
        grid=(b, nct),
        in_specs=[xspec(0), xspec(1), xspec(2), wspec(0), wspec(1), wspec(2),
                  bspec(0), bspec(1), bspec(2)],
        out_specs=[ospec, ospec],
        out_shape=[jax.ShapeDtypeStruct((b, l, c), BF16), jax.ShapeDtypeStruct((b, l, c), F32)],
        scratch_shapes=[pltpu.VMEM((l + 2 * _HALO, tc), F32)] * 3,
        compiler_params=_cparams("parallel", "parallel"),
        name="hyena_short_conv",
    )(hyu, hyu, hyu, conv_w, conv_w, conv_w, conv_b, conv_b, conv_b)


def _conv_silu_kernel(x_ref, w_ref, b_ref, o_ref, p_ref, *, l, taps, left):
    _fill_padded(p_ref, x_ref.at[0], l)
    w, bias = w_ref[...], b_ref[...]

    for r in range(0, l, _CONV_ROWS):
        o_ref[0, pl.ds(r, _CONV_ROWS), :] = _silu(_conv_rows(p_ref, r, w, bias, taps, left)).astype(o_ref.dtype)


def _conv_silu(x, conv_w, conv_b, tc=256):
    b, l, c = x.shape
    taps = conv_w.shape[0]
    return pl.pallas_call(
        functools.partial(_conv_silu_kernel, l=l, taps=taps, left=taps // 2),
        grid=(b, c // tc),
        in_specs=[pl.BlockSpec((1, l, tc), lambda i, j: (i, 0, j)),
                  pl.BlockSpec((taps, tc), lambda i, j: (0, j)),
                  pl.BlockSpec((1, tc), lambda i, j: (0, j))],
        out_specs=pl.BlockSpec((1, l, tc), lambda i, j: (i, 0, j)),
        out_shape=jax.ShapeDtypeStruct((b, l, c), BF16),
        scratch_shapes=[pltpu.VMEM((l + 2 * _HALO, tc), F32)],
        compiler_params=_cparams("parallel", "parallel"),
        name="ssd_conv_silu",
    )(x, conv_w, conv_b)


def _filter_kernel(z_ref, w1_ref, b1_ref, fr_ref, w2_ref, b2_ref, w3_ref, dl_ref,
                   hfg_ref, nrm_ref, *, tr, c):
    i = pl.program_id(0)
    z = z_ref[...]
    fr = fr_ref[...]
    d = functools.partial(jnp.dot, precision=HI, preferred_element_type=F32)
    h = jnp.sin(fr * (d(z, w1_ref[...]) + b1_ref[...]))
    h = jnp.sin(fr * (d(h, w2_ref[...]) + b2_ref[...]))
    h = d(h, w3_ref[...])
    window = jnp.exp(-z[:, 0:1] * dl_ref[...])
    hf = h[:, :c] * window
    row = lax.broadcasted_iota(jnp.int32, (tr, 1), 0) + i * tr
    g = jnp.where(row == 0, 0.0, h[:, c:] * window)
    hfg_ref[0] = hf
    hfg_ref[1] = g
    part = jnp.sum(jnp.abs(hf) + jnp.abs(g), axis=0, keepdims=True)

    @pl.when(i == 0)
    def _():
        nrm_ref[...] = part

    @pl.when(i != 0)
    def _():
        nrm_ref[...] = nrm_ref[...] + part


def _hyena_filter_taps(zfeat, w1, b1, freq, w2, b2, w3, deltas, tr=512):
    l, zp = zfeat.shape
    hp = w1.shape[1]
    c2 = w3.shape[1]
    c = c2 // 2
    full = lambda shape: pl.BlockSpec(shape, lambda i: (0,) * len(shape))
    return pl.pallas_call(
        functools.partial(_filter_kernel, tr=tr, c=c),
        grid=(l // tr,),
        in_specs=[pl.BlockSpec((tr, zp), lambda i: (i, 0)), full((zp, hp)), full((1, hp)),
                  full((1, hp)), full((hp, hp)), full((1, hp)), full((hp, c2)), full((1, c))],
        out_specs=[pl.BlockSpec((2, tr, c), lambda i: (0, i, 0)), full((1, c))],
        out_shape=[jax.ShapeDtypeStruct((2, l, c), F32), jax.ShapeDtypeStruct((1, c), F32)],
        compiler_params=_cparams("arbitrary"),
        name="hyena_filter_mlp",
    )(zfeat, w1, b1, freq, w2, b2, w3, deltas)


def _dft1_kernel(x_ref, fre_ref, fim_ref, re_ref, im_ref):
    x = x_ref[0]
    re_ref[0] = jnp.dot(fre_ref[...], x, precision=HI, preferred_element_type=F32)
    im_ref[0] = jnp.dot(fim_ref[...], x, precision=HI, preferred_element_type=F32)


def _dft_level1(x2, fre, fim, tlane=4096):
    b, nh, w = x2.shape
    kp = fre.shape[0]
    return pl.pallas_call(
        _dft1_kernel,
        grid=(b, w // tlane),
        in_specs=[pl.BlockSpec((1, nh, tlane), lambda i, j: (i, 0, j)),
                  pl.BlockSpec((kp, nh), lambda i, j: (0, 0)),
                  pl.BlockSpec((kp, nh), lambda i, j: (0, 0))],
        out_specs=[pl.BlockSpec((1, kp, tlane), lambda i, j: (i, 0, j))] * 2,
        out_shape=[jax.ShapeDtypeStruct((b, kp, w), F32)] * 2,
        compiler_params=_cparams("parallel", "parallel"),
        name="dft_level1",
    )(x2, fre, fim)


def _filter_spec_kernel(re_ref, im_ref, g_ref, nrm_ref, hre_ref, him_ref):
    n2 = re_ref.shape[1]
    g = g_ref[...]
    d = functools.partial(jnp.dot, precision=HI, preferred_element_type=F32)
    xf = d(g, jnp.concatenate([re_ref[0], im_ref[0]], axis=0))
    xg = d(g, jnp.concatenate([re_ref[1], im_ref[1]], axis=0))
    inv = 1.0 / nrm_ref[...]
    hre_ref[...] = (xf[:n2] + xg[:n2]) * inv
    him_ref[...] = (xf[n2:] - xg[n2:]) * inv


def _filter_spectrum(b1re, b1im, g32, nrm, tc=512):
    _, _, n2, c = b1re.shape
    k1 = g32.shape[0]
    return pl.pallas_call(
        _filter_spec_kernel,
        grid=(k1, c // tc),
        in_specs=[pl.BlockSpec((2, None, n2, tc), lambda k, j: (0, k, 0, j)),
                  pl.BlockSpec((2, None, n2, tc), lambda k, j: (0, k, 0, j)),
                  pl.BlockSpec((None, 2 * n2, 2 * n2), lambda k, j: (k, 0, 0)),
                  pl.BlockSpec((1, tc), lambda k, j: (0, j))],
        out_specs=[pl.BlockSpec((None, n2, tc), lambda k, j: (k, 0, j))] * 2,
        out_shape=[jax.ShapeDtypeStruct((k1, n2, c), F32)] * 2,
        compiler_params=_cparams("parallel", "parallel"),
        name="hyena_filter_spectrum",
    )(b1re, b1im, g32, nrm)


_LEVEL2_UNROLL = 4


def _fftconv_kernel(vf_ref, x0_ref, gt_ref, f1_ref, a1_ref, g_ref, gi_ref, hre_ref, him_ref, bias_ref,
                    o_ref, s_ref, y_ref):
    nh, ng, sub, tc = vf_ref.shape[1:]
    n2 = ng * sub
    nreal = s_ref.shape[0]
    k1 = nreal // 2 + 1
    d = functools.partial(jnp.dot, preferred_element_type=F32)

    f1 = f1_ref[...]
    for g in range(ng):
        x = vf_ref[0, :, g].reshape(nh * sub, tc).astype(BF16)
        s_ref[:, g] = d(f1, x).reshape(nreal, sub, tc)

    def level2(k, re, im):
        x = d(g_ref[k], jnp.concatenate([re, im], axis=0).astype(BF16))
        xr, xi = x[:n2], x[n2:]
        hr, hi = hre_ref[k], him_ref[k]
        y = jnp.concatenate([xr * hr - xi * hi, xr * hi + xi * hr], axis=0).astype(BF16)
        return d(gi_ref[k], y)

    zero = jnp.zeros((n2, tc), F32)
    for k in (0, k1 - 1):
        z = level2(k, s_ref[k].reshape(n2, tc), zero)
        s_ref[k] = z[:n2].reshape(ng, sub, tc)

    def body(k, carry):
        z = level2(k, s_ref[k].reshape(n2, tc), s_ref[k1 - 1 + k].reshape(n2, tc))
        s_ref[k] = z[:n2].reshape(ng, sub, tc)
        s_ref[k1 - 1 + k] = z[n2:].reshape(ng, sub, tc)
        return carry

    lax.fori_loop(1, k1 - 1, body, 0, unroll=_LEVEL2_UNROLL)

    a1 = a1_ref[...]
    for g in range(ng):
        z = s_ref[:, g].reshape(nreal * sub, tc).astype(BF16)
        y_ref[:, g] = d(a1, z).reshape(nh, sub, tc)

    bias = bias_ref[...]
    step = 2
    rows = step * n2

    def epilogue(i, carry):
        r = pl.multiple_of(i * rows, rows)
        y = y_ref[pl.ds(i * step, step)].reshape(rows, tc)
        vf = vf_ref[0, pl.ds(i * step, step)].reshape(rows, tc)
        x0 = x0_ref[0, pl.ds(r, rows), :].astype(F32)
        gt = gt_ref[0, pl.ds(r, rows), :].astype(F32)
        o_ref[0, pl.ds(r, rows), :] = ((x0 * (y + vf * bias)) * _silu(gt)).astype(o_ref.dtype)
        return carry

    lax.fori_loop(0, nh // step, epilogue, 0)


def _hyena_fftconv(vf, x0c, gate, f1k, a1k, g, ginv, hre, him, bias, tc=256):
    b, l, c = vf.shape
    nh, sub = DFT_N1 // 2, SUBLANES
    ng = DFT_N2 // sub
    kb = g.shape[0]
    n2 = DFT_N2
    rows = pl.BlockSpec((1, l, tc), lambda j, i: (i, 0, j))
    return pl.pallas_call(
        _fftconv_kernel,
        grid=(c // tc, b),
        in_specs=[pl.BlockSpec((1, nh, ng, sub, tc), lambda j, i: (i, 0, 0, 0, j)), rows, rows,
                  _resident(f1k.shape, lambda j, i: (0, 0)), _resident(a1k.shape, lambda j, i: (0, 0)),
                  _resident(g.shape, lambda j, i: (0, 0, 0)), _resident(ginv.shape, lambda j, i: (0, 0, 0)),
                  _resident((kb, n2, tc), lambda j, i: (0, 0, j)), _resident((kb, n2, tc), lambda j, i: (0, 0, j)),
                  pl.BlockSpec((1, tc), lambda j, i: (0, j))],
        out_specs=rows,
        out_shape=jax.ShapeDtypeStruct((b, l, c), BF16),
        scratch_shapes=[pltpu.VMEM((DFT_N1, ng, sub, tc), F32), pltpu.VMEM((nh, ng, sub, tc), F32)],
        compiler_params=_cparams("arbitrary", "arbitrary"),
        name="hyena_fftconv",
    )(vf.reshape(b, nh, ng, sub, c), x0c, gate, f1k, a1k, g, ginv, hre, him, bias)


@functools.lru_cache(maxsize=None)
def _dft_tables():
    n1, n2, k1, kp = DFT_N1, DFT_N2, DFT_K1, DFT_K1P
    n = n1 * n2
    nh = n1 // 2
    kk = np.arange(k1)[:, None]
    nn = np.arange(nh)[None, :]
    ang = 2.0 * np.pi * kk * nn / n1
    fre = np.zeros((kp, nh)); fim = np.zeros((kp, nh))
    fre[:k1] = np.cos(ang); fim[:k1] = -np.sin(ang)
    m = np.arange(n2)[None, None, :]
    k2 = np.arange(n2)[None, :, None]
    ka = np.arange(k1)[:, None, None]
    ph = -2.0 * np.pi * m * (ka / n + k2 / n2)
    gr, gi = np.cos(ph), np.sin(ph)
    g = np.concatenate([np.concatenate([gr, -gi], axis=2), np.concatenate([gi, gr], axis=2)], axis=1)
    grt, git = np.swapaxes(gr, 1, 2), np.swapaxes(gi, 1, 2)
    ginv = np.concatenate([np.concatenate([grt, git], axis=2), np.concatenate([-git, grt], axis=2)], axis=1)
    wk = np.where((np.arange(k1) == 0) | (np.arange(k1) == n1 // 2), 1.0, 2.0)[None, :]
    ang2 = 2.0 * np.pi * np.arange(nh)[:, None] * np.arange(k1)[None, :] / n1
    are = np.zeros((nh, kp)); aim = np.zeros((nh, kp))
    are[:, :k1] = wk * np.cos(ang2) / n
    aim[:, :k1] = -wk * np.sin(ang2) / n
    eye = np.eye(SUBLANES)
    f1k = np.kron(np.concatenate([fre[:k1], fim[1:k1 - 1]], axis=0), eye)
    a1k = np.kron(np.concatenate([are[:, :k1], aim[:, 1:k1 - 1]], axis=1), eye)
    f = lambda a: np.asarray(a, np.float32)
    return f(fre), f(fim), f(g), f(ginv), f(f1k), f(a1k)


def _position_features(l):
    f32 = F32
    pos = jnp.arange(l, dtype=f32)[:, None]
    t = pos / max(l - 1, 1)
    bands = (HY_EMB - 1) // 2
    f = jnp.linspace(1e-4, bands - 1, bands, dtype=f32)[None, :]
    ang = f * pos * (2.0 * math.pi / l)
    z = jnp.concatenate([t, jnp.cos(ang), -jnp.sin(ang)], axis=-1)
    return jnp.pad(z, ((0, 0), (0, LANES - HY_EMB)))


def _pad2(a, rows, cols):
    return jnp.pad(a, ((0, rows - a.shape[0]), (0, cols - a.shape[1])))


def _hyena_spectrum(l, c, fw1, fb1, fw2, fb2, fw3, freq):
    fre, fim, g32, _, _, _ = _dft_tables()
    hp = LANES
    zfeat = _position_features(l)
    w1 = _pad2(fw1, LANES, hp)
    w2 = _pad2(fw2, hp, hp)
    w3 = _pad2(fw3, hp, fw3.shape[1])
    row = lambda v: _pad2(v[None, :], 1, hp)
    deltas = jnp.abs(jnp.linspace(math.log(HY_DECAY_TARGET) / HY_LONG_DECAY_PCT,
                                  math.log(HY_DECAY_TARGET) / HY_SHORT_DECAY_PCT, c, dtype=F32))[None, :]
    hfg, nrm = _hyena_filter_taps(zfeat, w1, row(fb1), row(freq), w2, row(fb2), w3, deltas)
    b1re, b1im = _dft_level1(hfg.reshape(2, DFT_N1 // 2, DFT_N2 * c), jnp.asarray(fre), jnp.asarray(fim))
    shp = (2, DFT_K1P, DFT_N2, c)
    return _filter_spectrum(b1re.reshape(shp), b1im.reshape(shp), jnp.asarray(g32), nrm)


def _hyena_branch(hyu, hy_gate, conv_w, conv_b, hre, him, bias):
    b, l, c3 = hyu.shape
    c = c3 // 3
    _, _, g32, ginv32, f1k, a1k = _dft_tables()
    x0c, vf = _hyena_short_conv(hyu, conv_w, conv_b[None, :])
    bf = lambda a: jnp.asarray(a, dtype=BF16)
    return _hyena_fftconv(vf, x0c, hy_gate, bf(f1k), bf(a1k), bf(g32), bf(ginv32), hre, him, bias[None, :])


def _ssd_kernel(xs_ref, bm_ref, cm_ref, dt_ref, dtb_ref, alog_ref, rep_ref, dsk_ref, mask_ref,
                y_ref, prev_ref, *, reverse, lane0, chunks):
    q, n, hd = SSD_CHUNK, SSD_STATE, SSD_HEAD_DIM
    gw = (SSD_HEADS // SSD_GROUPS) * hd

    @pl.when(pl.program_id(1) == 0)
    def _():
        prev_ref[...] = jnp.zeros_like(prev_ref)

    mask = mask_ref[...]
    maskb = mask > 0
    neg_a = -jnp.exp(alog_ref[...])
    lo_lanes = lax.broadcasted_iota(jnp.int32, (1, LANES), 1) < hd
    d = functools.partial(jnp.dot, preferred_element_type=F32)
    npair = gw // LANES
    edge = 0 if reverse else q - 1

    def per_pair(v, c0, shape):
        b0 = jnp.broadcast_to(v[:, c0:c0 + 1], shape)
        b1 = jnp.broadcast_to(v[:, c0 + 1:c0 + 2], shape)
        return jnp.where(lo_lanes, b0, b1)

    order = range(chunks - 1, -1, -1) if reverse else range(chunks)
    for k in order:
        rows = pl.ds(k * q, q)
        xs_b = xs_ref[0, rows, :]
        xs = xs_b.astype(F32)
        bm = bm_ref[0, rows, :]
        cm = cm_ref[0, rows, :]
        dt = _softplus(dt_ref[0, rows, :] + dtb_ref[...])
        cum = _sel_dot(mask, dt * neg_a)
        cum_t, dt_t = cum.T, dt.T
        cum_e = cum[edge:edge + 1, :]
        e_in = jnp.exp(cum)
        w_out = dt * jnp.exp(cum_e - cum)
        e_all = jnp.exp(cum_e)
        spread = d(jnp.concatenate([e_in, w_out], axis=0).astype(BF16), rep_ref[...])
        e_x = spread[:q]
        xsd_all = (xs * spread[q:]).astype(BF16)
        ys = []
        for g in range(SSD_GROUPS):
            cg = cm[:, g * n:(g + 1) * n].astype(BF16)
            bg = bm[:, g * n:(g + 1) * n].astype(BF16)
            cb = lax.dot_general(cg, bg, (((1,), (1,)), ((), ())), preferred_element_type=F32)
            sl = slice(g * gw, (g + 1) * gw)
            prev_g = prev_ref[:, sl]
            y_off = d(cg, prev_g.astype(BF16))
            cds = []
            for pair in range(npair):
                lo = g * gw + pair * LANES
                c0 = lane0 + lo // hd
                ms = []
                for c in (c0, c0 + 1):
                    seg = jnp.exp(jnp.where(maskb, cum[:, c:c + 1] - cum_t[c:c + 1, :], -jnp.inf))
                    ms.append((cb * seg * dt_t[c:c + 1, :]).astype(BF16))
                xp = xs_b[:, lo:lo + LANES]
                zero = jnp.zeros_like(xp)
                rhs = jnp.concatenate([jnp.where(lo_lanes, xp, zero), jnp.where(lo_lanes, zero, xp)], axis=0)
                yd = d(jnp.concatenate(ms, axis=1), rhs)
                ys.append(yd + y_off[:, pair * LANES:(pair + 1) * LANES] * e_x[:, lo:lo + LANES])
                cds.append(per_pair(e_all, c0, (n, LANES)))
            st = lax.dot_general(bg, xsd_all[:, sl], (((0,), (0,)), ((), ())), preferred_element_type=F32)
            prev_ref[:, sl] = prev_g * jnp.concatenate(cds, axis=1) + st
        y = jnp.concatenate(ys, axis=1)
        if not reverse:
            y = y + xs * dsk_ref[...]
        y_ref[0, rows, :] = y


def _ssd_direction(xbc, dt_raw, dtb, alog, rep, dskip, mask, *, reverse, ts=512):
    b, l, _ = xbc.shape
    hp = SSD_HEADS * SSD_HEAD_DIM
    gn = SSD_GROUPS * SSD_STATE
    nt = l // ts
    tmap = (lambda j: nt - 1 - j) if reverse else (lambda j: j)
    full = lambda shape: pl.BlockSpec(shape, lambda i, j: (0,) * len(shape))
    return pl.pallas_call(
        functools.partial(_ssd_kernel, reverse=reverse, lane0=SSD_HEADS if reverse else 0,
                          chunks=ts // SSD_CHUNK),
        grid=(b, nt),
        in_specs=[pl.BlockSpec((1, ts, hp), lambda i, j: (i, tmap(j), 0)),
                  pl.BlockSpec((1, ts, gn), lambda i, j: (i, tmap(j), hp // gn)),
                  pl.BlockSpec((1, ts, gn), lambda i, j: (i, tmap(j), hp // gn + 1)),
                  pl.BlockSpec((1, ts, LANES), lambda i, j: (i, tmap(j), 0)),
                  full((1, LANES)), full((1, LANES)), full((LANES, hp)), full((1, hp)),
                  full((SSD_CHUNK, SSD_CHUNK))],
        out_specs=pl.BlockSpec((1, ts, hp), lambda i, j: (i, tmap(j), 0)),
        out_shape=jax.ShapeDtypeStruct((b, l, hp), F32),
        scratch_shapes=[pltpu.VMEM((SSD_STATE, hp), F32)],
        compiler_params=_cparams("parallel", "arbitrary"),
        name="ssd_bwd" if reverse else "ssd_fwd",
    )(xbc, xbc, xbc, dt_raw, dtb, alog, rep, dskip, mask)


def _ssd_branch(xbc, dt_raw, conv_w, conv_b, dt_bias, a_log, d_skip):
    xc = _conv_silu(xbc, conv_w, conv_b[None, :])
    pad_row = lambda v: _pad2(v.reshape(1, -1), 1, LANES)
    dtb, alog = pad_row(dt_bias), pad_row(a_log)
    dskip = jnp.repeat(d_skip, SSD_HEAD_DIM)[None, :]
    lane_head = np.arange(SSD_HEADS * SSD_HEAD_DIM)[None, :] // SSD_HEAD_DIM
    src = np.arange(LANES)[:, None]
    idx = np.arange(SSD_CHUNK)
    outs = []
    for reverse in (False, True):
        rep = jnp.asarray(src == lane_head + (SSD_HEADS if reverse else 0), dtype=BF16)
        allowed = (idx[None, :] >= idx[:, None]) if reverse else (idx[None, :] <= idx[:, None])
        mask = jnp.asarray(allowed, dtype=BF16)
        outs.append(_ssd_direction(xc, dt_raw, dtb, alog, rep, dskip, mask, reverse=reverse))
    return outs


_LRU_ROWS = 256


def _lru_kernel(x_ref, gt_ref, cw_ref, cb_ref, wa_ref, wx_ref, ba_ref, bx_ref, lam_ref,
                o_ref, pad_ref, af_ref, uf_ref, ab_ref, ub_ref, *, l, heads):
    _fill_padded(pad_ref, x_ref.at[0], l)
    cw, cb = cw_ref[...], cb_ref[...]
    e0 = (-0.5 * LRU_C / math.log(2.0)) * _softplus(-lam_ref[...])
    ba, bx = ba_ref[...], bx_ref[...]
    d = functools.partial(jnp.dot, preferred_element_type=F32)
    a_refs, u_refs = (af_ref, ab_ref), (uf_ref, ub_ref)

    def gates(i, carry):
        r = pl.multiple_of(i * _CONV_ROWS, _CONV_ROWS)
        xc = _conv_rows_dyn(pad_ref, r, cw, cb, 4, 2)
        for hd in range(heads):
            ls = slice(hd * LRU_BLOCK, (hd + 1) * LRU_BLOCK)
            xh = xc[:, ls]
            xb = xh.astype(BF16)
            xh2 = 0.5 * xh
            for dr in range(2):
                tr = jnp.tanh(d(xb, wa_ref[dr, hd]) + ba[dr:dr + 1, ls])
                ti = jnp.tanh(d(xb, wx_ref[dr, hd]) + bx[dr:dr + 1, ls])
                e = e0[dr:dr + 1, ls]
                a = jnp.exp2(e + e * tr)
                s = 1.0 - a * a
                root = jnp.where(s > 0.0, s * lax.rsqrt(s), 0.0)
                u = (xh2 + xh2 * ti) * root
                a_refs[dr][hd, pl.ds(r, _CONV_ROWS), :] = a
                u_refs[dr][hd, pl.ds(r, _CONV_ROWS), :] = u
        return carry

    lax.fori_loop(0, l // _CONV_ROWS, gates, 0)

    row = lax.broadcasted_iota(jnp.int32, (SUBLANES, LRU_BLOCK), 0)
    blk = SUBLANES * SUBLANES
    nblk = l // blk

    def sublane_scan(a, u, reverse):
        for s in (1, 2, 4):
            sh = (SUBLANES - s) if reverse else s
            valid = (row < SUBLANES - s) if reverse else (row >= s)
            a_sh = jnp.where(valid, pltpu.roll(a, sh, axis=0), 1.0)
            u_sh = jnp.where(valid, pltpu.roll(u, sh, axis=0), 0.0)
            u = u + a * u_sh
            a = a * a_sh
        return a, u

    def block_scan(a_ref, u_ref, r, h_in, reverse):
        order = range(SUBLANES - 1, -1, -1) if reverse else range(SUBLANES)
        hs, ps = {}, {}
        h = p = None
        for k in order:
            rows = pl.ds(r + k, SUBLANES, stride=SUBLANES)
            a, u = a_ref[rows, :], u_ref[rows, :]
            h = u if h is None else a * h + u
            p = a if p is None else a * p
            hs[k], ps[k] = h, p
        sp, se = sublane_scan(p, h, reverse)
        full = se + sp * h_in
        edge = 0 if reverse else SUBLANES - 1
        h_out = jnp.broadcast_to(full[edge:edge + 1, :], (SUBLANES, LRU_BLOCK))
        if reverse:
            carry = jnp.where(row == SUBLANES - 1, h_in, pltpu.roll(full, SUBLANES - 1, axis=0))
        else:
            carry = jnp.where(row == 0, h_in, pltpu.roll(full, 1, axis=0))
        for k in order:
            u_ref[pl.ds(r + k, SUBLANES, stride=SUBLANES), :] = hs[k] + ps[k] * carry
        return h_out

    def scan(i, carry):
        rf = pl.multiple_of(i * blk, blk)
        rb = pl.multiple_of((nblk - 1 - i) * blk, blk)
        out = []
        for hd in range(heads):
            hf, hb = carry[2 * hd], carry[2 * hd + 1]
            out.append(block_scan(af_ref.at[hd], uf_ref.at[hd], rf, hf, False))
            out.append(block_scan(ab_ref.at[hd], ub_ref.at[hd], rb, hb, True))
        return tuple(out)

    zero = jnp.zeros((SUBLANES, LRU_BLOCK), F32)
    lax.fori_loop(0, nblk, scan, (zero,) * (2 * heads), unroll=2)

    def combine(i, carry):
        r = pl.multiple_of(i * _LRU_ROWS, _LRU_ROWS)
        rows = pl.ds(r, _LRU_ROWS)
        h = jnp.concatenate([uf_ref[hd, rows, :] + ub_ref[hd, rows, :] for hd in range(heads)], axis=1)
        o_ref[0, rows, :] = (h * _silu(gt_ref[0, rows, :].astype(F32))).astype(o_ref.dtype)
        return carry

    lax.fori_loop(0, l // _LRU_ROWS, combine, 0)


def _rglru_mixer(xb, gate, conv_w, conv_b, w_a, w_x, b_a, b_x, lam, heads=2):
    b, l, w = xb.shape
    tw = heads * LRU_BLOCK
    xspec = pl.BlockSpec((1, l, tw), lambda i, j: (i, 0, j))
    wspec = pl.BlockSpec((2, heads, LRU_BLOCK, LRU_BLOCK), lambda i, j: (0, j, 0, 0))
    vspec = pl.BlockSpec((2, tw), lambda i, j: (0, j))
    seq = pltpu.VMEM((heads, l, LRU_BLOCK), F32)
    return pl.pallas_call(
        functools.partial(_lru_kernel, l=l, heads=heads),
        grid=(b, w // tw),
        in_specs=[xspec, xspec,
                  pl.BlockSpec((4, tw), lambda i, j: (0, j)),
                  pl.BlockSpec((1, tw), lambda i, j: (0, j)),
                  wspec, wspec, vspec, vspec, vspec],
        out_specs=xspec,
        out_shape=jax.ShapeDtypeStruct((b, l, w), BF16),
        scratch_shapes=[pltpu.VMEM((l + 2 * _HALO, tw), F32), seq, seq, seq, seq],
        compiler_params=_cparams("parallel", "parallel"),
        name="rglru",
    )(xb, gate, conv_w, conv_b[None, :], (0.5 * w_a).astype(BF16), (0.5 * w_x).astype(BF16),
      0.5 * b_a, 0.5 * b_x, lam)


def _rms(x, g, eps):
    return (x * lax.rsqrt(jnp.mean(x * x, axis=-1, keepdims=True) + eps)) * g


def _outproj_even_kernel(*refs, n_x, bp):
    x_refs, (ya_ref, y0_ref, y1_ref, z_ref, gm_ref, ng_ref, wa_ref, wb_ref, o_ref) = refs[:n_x], refs[n_x:]
    yb = (y0_ref[0] + y1_ref[0]) * _silu(z_ref[0].astype(F32))
    ng = ng_ref[...]
    gw = yb.shape[1] // SSD_GROUPS
    parts = [_rms(yb[:, g * gw:(g + 1) * gw], ng[:, g * gw:(g + 1) * gw], 1e-5) for g in range(SSD_GROUPS)]
    ybn = jnp.concatenate(parts, axis=1)
    d = functools.partial(jnp.dot, preferred_element_type=F32)
    acc = d(ya_ref[0].astype(BF16), wa_ref[...]) + d(ybn.astype(BF16), wb_ref[...])
    o_ref[0] = _load_rows(x_refs, bp) + gm_ref[0] * acc


def _out_projection_even(x, ya, y0, y1, z, gate_mod, norm_g, w_out, tl=512):
    b, l, c = ya.shape
    dm = x[0].shape[2]
    rows = lambda n: pl.BlockSpec((1, tl, n), lambda i, j: (i, j, 0))
    wa, wb = w_out[:c].astype(BF16), w_out[c:].astype(BF16)
    return pl.pallas_call(
        functools.partial(_outproj_even_kernel, n_x=len(x), bp=x[0].shape[0]),
        grid=(b, l // tl),
        in_specs=_row_specs(x, tl) + [rows(c), rows(c), rows(c), rows(c),
                                      pl.BlockSpec((1, 1, dm), lambda i, j: (i, 0, 0)),
                                      pl.BlockSpec((1, c), lambda i, j: (0, 0)),
                                      _resident(wa.shape, lambda i, j: (0, 0)),
                                      _resident(wb.shape, lambda i, j: (0, 0))],
        out_specs=rows(dm),
        out_shape=jax.ShapeDtypeStruct((b, l, dm), F32),
        compiler_params=_cparams("arbitrary", "arbitrary"),
        name="out_proj_even",
    )(*x, ya, y0, y1, z, gate_mod, norm_g, wa, wb)


def _outproj_odd_kernel(x_ref, y_ref, gm_ref, w_ref, fg_ref, *o_refs, bp):
    acc = jnp.dot(y_ref[0].astype(BF16), w_ref[...], preferred_element_type=F32)
    xn = x_ref[0] + gm_ref[0] * acc
    if len(o_refs) == 1:
        o_refs[0][0] = xn
        return
    out = _rms(xn, fg_ref[...], 1e-6)

    @pl.when(pl.program_id(0) < bp)
    def _():
        o_refs[0][0] = out

    @pl.when(pl.program_id(0) >= bp)
    def _():
        o_refs[1][0] = out


def _out_projection_odd(x, y, gate_mod, w_out, final_g, split_at=None, tl=512):
    b, l, dm = x.shape
    w = y.shape[2]
    rows = lambda n: pl.BlockSpec((1, tl, n), lambda i, j: (i, j, 0))
    wb = w_out.astype(BF16)
    if split_at is None:
        out_specs = [rows(dm)]
        out_shape = [jax.ShapeDtypeStruct((b, l, dm), F32)]
    else:
        out_specs = [pl.BlockSpec((1, tl, dm), m) for m in _pair_row_maps(split_at, l // tl)]
        out_shape = [jax.ShapeDtypeStruct((n, l, dm), F32) for n in (split_at, b - split_at)]
    out = pl.pallas_call(
        functools.partial(_outproj_odd_kernel, bp=split_at),
        grid=(b, l // tl),
        in_specs=[rows(dm), rows(w),
                  pl.BlockSpec((1, 1, dm), lambda i, j: (i, 0, 0)),
                  _resident(wb.shape, lambda i, j: (0, 0)),
                  pl.BlockSpec((1, dm), lambda i, j: (0, 0))],
        out_specs=out_specs,
        out_shape=out_shape,
        compiler_params=_cparams("arbitrary", "arbitrary"),
        name="out_proj_odd",
    )(x, y, gate_mod, wb, final_g)
    return out[0] if split_at is None else tuple(out)


def kernel(x_prompt, x_sample, c_prompt, c_sample, mod_w, mod_b, norm_g, final_g, ev_w_in, ev_w_out, hy_conv_w, hy_conv_b, hy_fw1, hy_fb1, hy_fw2, hy_fb2, hy_fw3, hy_freq, hy_bias, ssd_conv_w, ssd_conv_b, ssd_dt_bias, ssd_A_log, ssd_D, ssd_norm_g, od_w_in, od_w_out, lru_conv_w, lru_conv_b, lru_w_a, lru_b_a, lru_w_x, lru_b_x, lru_lam):
    bp, l, dm = x_prompt.shape
    assert x_sample.shape[1:] == (l, dm), "the two request groups are stacked on the batch axis"
    x = (x_prompt, x_sample)
    c = jnp.concatenate([c_prompt, c_sample], axis=0)
    b = c.shape[0]
    depth = mod_w.shape[0]
    assert 2 * l == DFT_N1 * DFT_N2
    assert depth % 2 == 0, "the final norm is fused into the last (odd) layer's output projection"

    b_pad = -(-b // SUBLANES) * SUBLANES
    mods = _modulation(jnp.pad(c, ((0, b_pad - b), (0, 0))), mod_w, mod_b)[:, :b]
    shift, scale, gate_mod = (mods[:, :, k * dm:(k + 1) * dm].reshape(depth, b, 1, dm) for k in range(3))

    hyc = hy_bias.shape[1]
    ssw = ssd_norm_g.shape[1]
    xbw = ssd_conv_w.shape[2]
    o1, o2, o3, o4 = 3 * hyc, 4 * hyc, 4 * hyc + ssw, 4 * hyc + ssw + xbw

    for i in range(depth):
        j = i // 2
        g = norm_g[i][None, :]
        if i % 2 == 0:
            w_in = ev_w_in[j].astype(BF16)
            w_dt = jnp.pad(w_in[:, o4:], ((0, 0), (0, LANES - (w_in.shape[1] - o4))))
            hyu, hyg, z, xbc, dt_raw = _in_projection(
                x, shift[i], scale[i], g, [w_in[:, :o1], w_in[:, o1:o2], w_in[:, o2:o3], w_in[:, o3:o4], w_dt],
                [BF16, BF16, BF16, BF16, F32])
            hre, him = _hyena_spectrum(l, hyc, hy_fw1[j], hy_fb1[j], hy_fw2[j], hy_fb2[j], hy_fw3[j], hy_freq[j])
            ya = _hyena_branch(hyu, hyg, hy_conv_w[j], hy_conv_b[j], hre, him, hy_bias[j])
            y0, y1 = _ssd_branch(xbc, dt_raw, ssd_conv_w[j], ssd_conv_b[j], ssd_dt_bias[j], ssd_A_log[j], ssd_D[j])
            x = (_out_projection_even(x, ya, y0, y1, z, gate_mod[i], ssd_norm_g[j][None, :], ev_w_out[j]),)
        else:
            w_in = od_w_in[j].astype(BF16)
            half = w_in.shape[1] // 2
            xb, gt = _in_projection(x, shift[i], scale[i], g, [w_in[:, :half], w_in[:, half:]], [BF16, BF16])
            y = _rglru_mixer(xb, gt, lru_conv_w[j], lru_conv_b[j], lru_w_a[j], lru_w_x[j],
                             lru_b_a[j], lru_b_x[j], lru_lam[j])
            if i == depth - 1:
                return _out_projection_odd(x[0], y, gate_mod[i], od_w_out[j], final_g[None, :], split_at=bp)
            x = (_out_projection_odd(x[0], y, gate_mod[i], od_w_out[j], final_g[None, :]),)
```

```python
import functools
import math

import numpy as np
import jax
import jax.numpy as jnp
from jax import lax
from jax.experimental import pallas as pl
from jax.experimental.pallas import tpu as pltpu

F32 = jnp.float32
BF16 = jnp.bfloat16
HI = lax.Precision.HIGHEST

V7X_VMEM_BYTES = 64 * 1024 * 1024
VMEM_LIMIT = V7X_VMEM_BYTES - 8 * 1024 * 1024
LANES = 128
SUBLANES = 8

HY_EMB = 33
HY_SHORT_DECAY_PCT = 0.3
HY_LONG_DECAY_PCT = 1.5
HY_DECAY_TARGET = 1e-2
SSD_HEADS = 16
SSD_HEAD_DIM = 64
SSD_GROUPS = 4
SSD_STATE = 128
SSD_CHUNK = 128
LRU_HEADS = 16
LRU_BLOCK = 128
LRU_C = 8.0

DFT_N1 = 64
DFT_N2 = 128
DFT_K1 = DFT_N1 // 2 + 1


def _cparams(*sem):
    return pltpu.CompilerParams(dimension_semantics=sem, vmem_limit_bytes=VMEM_LIMIT)


def _resident(shape, index_map):
    return pl.BlockSpec(shape, index_map, pipeline_mode=pl.Buffered(1))


def _sigmoid(x):
    return 0.5 + 0.5 * jnp.tanh(0.5 * x)


def _silu(x):
    return x * _sigmoid(x)


def _softplus(x):
    return jnp.maximum(x, 0.0) + jnp.log1p(jnp.exp(-jnp.abs(x)))


def _split3(x):
    hi = x.astype(BF16)
    r1 = x - hi.astype(F32)
    mid = r1.astype(BF16)
    lo = (r1 - mid.astype(F32)).astype(BF16)
    return hi, mid, lo


def _sel_dot(sel, x):
    hi, mid, lo = _split3(x)
    d = functools.partial(jnp.dot, preferred_element_type=F32)
    return d(sel, hi) + d(sel, mid) + d(sel, lo)


def _mod_kernel(c_ref, w_ref, b_ref, o_ref):
    cs = _silu(c_ref[...])
    o_ref[0] = jnp.dot(cs, w_ref[0], precision=HI, preferred_element_type=F32) + b_ref[0]


def _modulation(c_pad, mod_w, mod_b):
    depth, d, d3 = mod_w.shape
    bp = c_pad.shape[0]
    return pl.pallas_call(
        _mod_kernel,
        grid=(depth, d3 // d),
        in_specs=[
            pl.BlockSpec((bp, d), lambda i, j: (0, 0)),
            pl.BlockSpec((1, d, d), lambda i, j: (i, 0, j)),
            pl.BlockSpec((1, 1, d), lambda i, j: (i, 0, j)),
        ],
        out_specs=pl.BlockSpec((1, bp, d), lambda i, j: (i, 0, j)),
        out_shape=jax.ShapeDtypeStruct((depth, bp, d3), F32),
        compiler_params=_cparams("arbitrary", "arbitrary"),
        name="adaln_mod",
    )(c_pad, mod_w, mod_b.reshape(depth, 1, d3))


def _pair_row_maps(bp, nj):
    first = lambda i, j: (jnp.minimum(i, bp - 1), jnp.where(i < bp, j, nj - 1), 0)
    second = lambda i, j: (jnp.maximum(i - bp, 0), jnp.where(i < bp, 0, j), 0)
    return first, second


def _load_rows(x_refs, bp):
    if len(x_refs) == 1:
        return x_refs[0][0]
    return jnp.where(pl.program_id(0) < bp, x_refs[0][0], x_refs[1][0])


def _row_specs(xs, tl):
    l, d = xs[0].shape[1:]
    if len(xs) == 1:
        return [pl.BlockSpec((1, tl, d), lambda i, j: (i, j, 0))]
    return [pl.BlockSpec((1, tl, d), m) for m in _pair_row_maps(xs[0].shape[0], l // tl)]


def _inproj_kernel(*refs, n_x, bp, n_out):
    x_refs, (sh_ref, sc_ref, g_ref), refs = refs[:n_x], refs[n_x:n_x + 3], refs[n_x + 3:]
    w_refs, o_refs = refs[:n_out], refs[n_out:]
    hb = _norm_modulate(_load_rows(x_refs, bp), g_ref[...], sh_ref[0], sc_ref[0]).astype(BF16)
    for w_ref, o_ref in zip(w_refs, o_refs):
        o_ref[0] = jnp.dot(hb, w_ref[...], preferred_element_type=F32).astype(o_ref.dtype)


def _in_projection(x, shift, scale, g, weights, dtypes, tl=512):
    b = sum(a.shape[0] for a in x)
    l, d = x[0].shape[1:]
    n_out = len(weights)
    in_specs = _row_specs(x, tl) + [
        pl.BlockSpec((1, 1, d), lambda i, j: (i, 0, 0)),
        pl.BlockSpec((1, 1, d), lambda i, j: (i, 0, 0)),
        pl.BlockSpec((1, d), lambda i, j: (0, 0)),
    ] + [_resident(w.shape, lambda i, j: (0, 0)) for w in weights]
    out_specs = [pl.BlockSpec((1, tl, w.shape[1]), lambda i, j: (i, j, 0)) for w in weights]
    out_shape = [jax.ShapeDtypeStruct((b, l, w.shape[1]), dt) for w, dt in zip(weights, dtypes)]
    return pl.pallas_call(
        functools.partial(_inproj_kernel, n_x=len(x), bp=x[0].shape[0], n_out=n_out),
        grid=(b, l // tl),
        in_specs=in_specs,
        out_specs=out_specs,
        out_shape=out_shape,
        compiler_params=_cparams("arbitrary", "arbitrary"),
        name="in_proj",
    )(*x, shift, scale, g, *weights)


_CONV_ROWS = 256
_HALO = SUBLANES


def _fill_padded(pad_ref, src_ref, l):
    tc = pad_ref.shape[1]
    zeros = jnp.zeros((_HALO, tc), F32)
    pad_ref[pl.ds(0, _HALO), :] = zeros
    pad_ref[pl.ds(l + _HALO, _HALO), :] = zeros

    def body(i, carry):
        r = pl.multiple_of(i * _CONV_ROWS, _CONV_ROWS)
        pad_ref[pl.ds(r + _HALO, _CONV_ROWS), :] = src_ref[pl.ds(r, _CONV_ROWS), :].astype(F32)
        return carry

    lax.fori_loop(0, l // _CONV_ROWS, body, 0)


def _conv_rows_dyn(pad_ref, r, w, bias, taps, left):
    n = _CONV_ROWS + 2 * _HALO
    xe = pad_ref[pl.ds(r, n), :]
    acc = None
    for k in range(taps):
        off = k - left
        xs = xe if off == 0 else pltpu.roll(xe, (-off) % n, axis=0)
        term = w[k:k + 1, :] * xs[_HALO:_HALO + _CONV_ROWS, :]
        acc = term if acc is None else acc + term
    return acc + bias


def _conv_rows(pad_ref, r, w, bias, taps, left):
    acc = None
    for k in range(taps):
        term = w[k:k + 1, :] * pad_ref[pl.ds(r + _HALO + k - left, _CONV_ROWS), :]
        acc = term if acc is None else acc + term
    return acc + bias


def _hy_conv_kernel(x0_ref, x1_ref, v_ref, w0_ref, w1_ref, w2_ref, b0_ref, b1_ref, b2_ref,
                    x0c_ref, vf_ref, p0_ref, p1_ref, p2_ref, *, l):
    _fill_padded(p0_ref, x0_ref.at[0], l)
    _fill_padded(p1_ref, x1_ref.at[0], l)
    _fill_padded(p2_ref, v_ref.at[0], l)
    w0, w1, w2 = w0_ref[...], w1_ref[...], w2_ref[...]
    b0, b1, b2 = b0_ref[...], b1_ref[...], b2_ref[...]

    for r in range(0, l, _CONV_ROWS):
        x0c_ref[0, pl.ds(r, _CONV_ROWS), :] = _conv_rows(p0_ref, r, w0, b0, 3, 1).astype(x0c_ref.dtype)
        x1c = _conv_rows(p1_ref, r, w1, b1, 3, 1)
        vc = _conv_rows(p2_ref, r, w2, b2, 3, 1)
        vf_ref[0, pl.ds(r, _CONV_ROWS), :] = vc * x1c


def _hyena_short_conv(hyu, conv_w, conv_b, tc=256):
    b, l, c3 = hyu.shape
    c = c3 // 3
    nct = c // tc
    xspec = lambda o: pl.BlockSpec((1, l, tc), lambda i, j, o=o: (i, 0, j + o * nct))
    wspec = lambda o: pl.BlockSpec((3, tc), lambda i, j, o=o: (0, j + o * nct))
    bspec = lambda o: pl.BlockSpec((1, tc), lambda i, j, o=o: (0, j + o * nct))
    ospec = pl.BlockSpec((1, l, tc), lambda i, j: (i, 0, j))
    return pl.pallas_call(
        functools.partial(_hy_conv_kernel, l=l),
        grid=(b, nct),
        in_specs=[xspec(0), xspec(1), xspec(2), wspec(0), wspec(1), wspec(2),
                  bspec(0), bspec(1), bspec(2)],
        out_specs=[ospec, ospec],
        out_shape=[jax.ShapeDtypeStruct((b, l, c), BF16), jax.ShapeDtypeStruct((b, l, c), F32)],
        scratch_shapes=[pltpu.VMEM((l + 2 * _HALO, tc), F32)] * 3,
        compiler_params=_cparams("parallel", "parallel"),
        name="hyena_short_conv",
    )(hyu, hyu, hyu, conv_w, conv_w, conv_w, conv_b, conv_b, conv_b)


def _conv_silu_kernel(x_ref, w_ref, b_ref, o_ref, p_ref, *, l, taps, left):
    _fill_padded(p_ref, x_ref.at[0], l)
    w, bias = w_ref[...], b_ref[...]

    for r in range(0, l, _CONV_ROWS):
        o_ref[0, pl.ds(r, _CONV_ROWS), :] = _silu(_conv_rows(p_ref, r, w, bias, taps, left)).astype(o_ref.dtype)


def _conv_silu(x, conv_w, conv_b, tc=256):
    b, l, c = x.shape
    taps = conv_w.shape[0]
    return pl.pallas_call(
        functools.partial(_conv_silu_kernel, l=l, taps=taps, left=taps // 2),
        grid=(b, c // tc),
        in_specs=[pl.BlockSpec((1, l, tc), lambda i, j: (i, 0, j)),
                  pl.BlockSpec((taps, tc), lambda i, j: (0, j)),
                  pl.BlockSpec((1, tc), lambda i, j: (0, j))],
        out_specs=pl.BlockSpec((1, l, tc), lambda i, j: (i, 0, j)),
        out_shape=jax.ShapeDtypeStruct((b, l, c), BF16),
        scratch_shapes=[pltpu.VMEM((l + 2 * _HALO, tc), F32)],
        compiler_params=_cparams("parallel", "parallel"),
        name="ssd_conv_silu",
    )(x, conv_w, conv_b)


def _filter_kernel(z_ref, w1_ref, b1_ref, fr_ref, w2_ref, b2_ref, w3_ref, dl_ref,
                   hfg_ref, nrm_ref, *, tr, c):
    i = pl.program_id(0)
    z = z_ref[...]
    fr = fr_ref[...]
    d = functools.partial(jnp.dot, precision=HI, preferred_element_type=F32)
    h = jnp.sin(fr * (d(z, w1_ref[...]) + b1_ref[...]))
    h = jnp.sin(fr * (d(h, w2_ref[...]) + b2_ref[...]))
    h = d(h, w3_ref[...])
    window = jnp.exp(-z[:, 0:1] * dl_ref[...])
    hf = h[:, :c] * window
    row = lax.broadcasted_iota(jnp.int32, (tr, 1), 0) + i * tr
    g = jnp.where(row == 0, 0.0, h[:, c:] * window)
    hfg_ref[0] = hf
    hfg_ref[1] = g
    part = jnp.sum(jnp.abs(hf) + jnp.abs(g), axis=0, keepdims=True)

    @pl.when(i == 0)
    def _():
        nrm_ref[...] = part

    @pl.when(i != 0)
    def _():
        nrm_ref[...] = nrm_ref[...] + part


def _hyena_filter_taps(zfeat, w1, b1, freq, w2, b2, w3, deltas, tr=512):
    l, zp = zfeat.shape
    hp = w1.shape[1]
    c2 = w3.shape[1]
    c = c2 // 2
    full = lambda shape: pl.BlockSpec(shape, lambda i: (0,) * len(shape))
    return pl.pallas_call(
        functools.partial(_filter_kernel, tr=tr, c=c),
        grid=(l // tr,),
        in_specs=[pl.BlockSpec((tr, zp), lambda i: (i, 0)), full((zp, hp)), full((1, hp)),
                  full((1, hp)), full((hp, hp)), full((1, hp)), full((hp, c2)), full((1, c))],
        out_specs=[pl.BlockSpec((2, tr, c), lambda i: (0, i, 0)), full((1, c))],
        out_shape=[jax.ShapeDtypeStruct((2, l, c), F32), jax.ShapeDtypeStruct((1, c), F32)],
        compiler_params=_cparams("arbitrary"),
        name="hyena_filter_mlp",
    )(zfeat, w1, b1, freq, w2, b2, w3, deltas)


def _filter_spec_kernel(h_ref, f1_ref, g_ref, nrm_ref, hre_ref, him_ref, s_ref):
    _, nh, ng, sub, tc = h_ref.shape
    n2 = ng * sub
    nreal = s_ref.shape[1]
    k1 = nreal // 2 + 1
    d = functools.partial(jnp.dot, preferred_element_type=F32)
    f1 = f1_ref[...]
    for q in range(2):
        for g in range(ng):
            x = h_ref[q, :, g].reshape(nh * sub, tc).astype(BF16)
            s_ref[q, :, g] = d(f1, x).reshape(nreal, sub, tc)
    inv = 1.0 / nrm_ref[...]
    zero = jnp.zeros((n2, tc), F32)

    def spectrum(k, has_im):
        xs = []
        for q in range(2):
            re = s_ref[q, k].reshape(n2, tc)
            im = s_ref[q, k1 - 1 + k].reshape(n2, tc) if has_im else zero
            xs.append(d(g_ref[k], jnp.concatenate([re, im], axis=0).astype(BF16)))
        xf, xg = xs
        hre_ref[k] = (xf[:n2] + xg[:n2]) * inv
        him_ref[k] = (xf[n2:] - xg[n2:]) * inv

    spectrum(0, False)
    spectrum(k1 - 1, False)

    def body(k, carry):
        spectrum(k, True)
        return carry

    lax.fori_loop(1, k1 - 1, body, 0, unroll=_LEVEL2_UNROLL)


def _filter_spectrum(hfg, f1k, g, nrm, tc=128):
    _, l, c = hfg.shape
    nh, sub = DFT_N1 // 2, SUBLANES
    ng = DFT_N2 // sub
    k1, n2 = g.shape[0], DFT_N2
    return pl.pallas_call(
        _filter_spec_kernel,
        grid=(c // tc,),
        in_specs=[pl.BlockSpec((2, nh, ng, sub, tc), lambda j: (0, 0, 0, 0, j)),
                  _resident(f1k.shape, lambda j: (0, 0)), _resident(g.shape, lambda j: (0, 0, 0)),
                  pl.BlockSpec((1, tc), lambda j: (0, j))],
        out_specs=[pl.BlockSpec((k1, n2, tc), lambda j: (0, 0, j))] * 2,
        out_shape=[jax.ShapeDtypeStruct((k1, n2, c), F32)] * 2,
        scratch_shapes=[pltpu.VMEM((2, DFT_N1, ng, sub, tc), F32)],
        compiler_params=_cparams("arbitrary"),
        name="hyena_filter_spectrum",
    )(hfg.reshape(2, nh, ng, sub, c), f1k, g, nrm)


_LEVEL2_UNROLL = 4


def _fftconv_kernel(vf_ref, x0_ref, gt_ref, f1_ref, a1_ref, g_ref, gi_ref, hre_ref, him_ref, bias_ref,
                    o_ref, s_ref, y_ref):
    nh, ng, sub, tc = vf_ref.shape[1:]
    n2 = ng * sub
    nreal = s_ref.shape[0]
    k1 = nreal // 2 + 1
    d = functools.partial(jnp.dot, preferred_element_type=F32)

    f1 = f1_ref[...]
    for g in range(ng):
        x = vf_ref[0, :, g].reshape(nh * sub, tc).astype(BF16)
        s_ref[:, g] = d(f1, x).reshape(nreal, sub, tc)

    def level2(k, re, im):
        x = d(g_ref[k], jnp.concatenate([re, im], axis=0).astype(BF16))
        xr, xi = x[:n2], x[n2:]
        hr, hi = hre_ref[k], him_ref[k]
        y = jnp.concatenate([xr * hr - xi * hi, xr * hi + xi * hr], axis=0).astype(BF16)
        return d(gi_ref[k], y)

    zero = jnp.zeros((n2, tc), F32)
    for k in (0, k1 - 1):
        z = level2(k, s_ref[k].reshape(n2, tc), zero)
        s_ref[k] = z[:n2].reshape(ng, sub, tc)

    def body(k, carry):
        z = level2(k, s_ref[k].reshape(n2, tc), s_ref[k1 - 1 + k].reshape(n2, tc))
        s_ref[k] = z[:n2].reshape(ng, sub, tc)
        s_ref[k1 - 1 + k] = z[n2:].reshape(ng, sub, tc)
        return carry

    lax.fori_loop(1, k1 - 1, body, 0, unroll=_LEVEL2_UNROLL)

    a1 = a1_ref[...]
    for g in range(ng):
        z = s_ref[:, g].reshape(nreal * sub, tc).astype(BF16)
        y_ref[:, g] = d(a1, z).reshape(nh, sub, tc)

    bias = bias_ref[...]
    step = 2
    rows = step * n2

    def epilogue(i, carry):
        r = pl.multiple_of(i * rows, rows)
        y = y_ref[pl.ds(i * step, step)].reshape(rows, tc)
        vf = vf_ref[0, pl.ds(i * step, step)].reshape(rows, tc)
        x0 = x0_ref[0, pl.ds(r, rows), :].astype(F32)
        gt = gt_ref[0, pl.ds(r, rows), :].astype(F32)
        o_ref[0, pl.ds(r, rows), :] = ((x0 * (y + vf * bias)) * _silu(gt)).astype(o_ref.dtype)
        return carry

    lax.fori_loop(0, nh // step, epilogue, 0)


def _hyena_fftconv(vf, x0c, gate, f1k, a1k, g, ginv, hre, him, bias, tc=256):
    b, l, c = vf.shape
    nh, sub = DFT_N1 // 2, SUBLANES
    ng = DFT_N2 // sub
    kb = g.shape[0]
    n2 = DFT_N2
    rows = pl.BlockSpec((1, l, tc), lambda j, i: (i, 0, j))
    return pl.pallas_call(
        _fftconv_kernel,
        grid=(c // tc, b),
        in_specs=[pl.BlockSpec((1, nh, ng, sub, tc), lambda j, i: (i, 0, 0, 0, j)), rows, rows,
                  _resident(f1k.shape, lambda j, i: (0, 0)), _resident(a1k.shape, lambda j, i: (0, 0)),
                  _resident(g.shape, lambda j, i: (0, 0, 0)), _resident(ginv.shape, lambda j, i: (0, 0, 0)),
                  _resident((kb, n2, tc), lambda j, i: (0, 0, j)), _resident((kb, n2, tc), lambda j, i: (0, 0, j)),
                  pl.BlockSpec((1, tc), lambda j, i: (0, j))],
        out_specs=rows,
        out_shape=jax.ShapeDtypeStruct((b, l, c), BF16),
        scratch_shapes=[pltpu.VMEM((DFT_N1, ng, sub, tc), F32), pltpu.VMEM((nh, ng, sub, tc), F32)],
        compiler_params=_cparams("arbitrary", "arbitrary"),
        name="hyena_fftconv",
    )(vf.reshape(b, nh, ng, sub, c), x0c, gate, f1k, a1k, g, ginv, hre, him, bias)


@functools.lru_cache(maxsize=None)
def _dft_tables():
    n1, n2, k1 = DFT_N1, DFT_N2, DFT_K1
    n = n1 * n2
    nh = n1 // 2
    ang = 2.0 * np.pi * np.arange(k1)[:, None] * np.arange(nh)[None, :] / n1
    fre, fim = np.cos(ang), -np.sin(ang)
    m = np.arange(n2)[None, None, :]
    k2 = np.arange(n2)[None, :, None]
    ka = np.arange(k1)[:, None, None]
    ph = -2.0 * np.pi * m * (ka / n + k2 / n2)
    gr, gi = np.cos(ph), np.sin(ph)
    g = np.concatenate([np.concatenate([gr, -gi], axis=2), np.concatenate([gi, gr], axis=2)], axis=1)
    grt, git = np.swapaxes(gr, 1, 2), np.swapaxes(gi, 1, 2)
    ginv = np.concatenate([np.concatenate([grt, git], axis=2), np.concatenate([-git, grt], axis=2)], axis=1)
    wk = np.where((np.arange(k1) == 0) | (np.arange(k1) == n1 // 2), 1.0, 2.0)[None, :]
    are, aim = wk * fre.T / n, wk * fim.T / n
    eye = np.eye(SUBLANES)
    f1k = np.kron(np.concatenate([fre, fim[1:k1 - 1]], axis=0), eye)
    a1k = np.kron(np.concatenate([are, aim[:, 1:k1 - 1]], axis=1), eye)
    return tuple(np.asarray(a, np.float32).astype(BF16) for a in (g, ginv, f1k, a1k))


def _position_features(l):
    f32 = F32
    pos = jnp.arange(l, dtype=f32)[:, None]
    t = pos / max(l - 1, 1)
    bands = (HY_EMB - 1) // 2
    f = jnp.linspace(1e-4, bands - 1, bands, dtype=f32)[None, :]
    ang = f * pos * (2.0 * math.pi / l)
    z = jnp.concatenate([t, jnp.cos(ang), -jnp.sin(ang)], axis=-1)
    return jnp.pad(z, ((0, 0), (0, LANES - HY_EMB)))


def _pad2(a, rows, cols):
    return jnp.pad(a, ((0, rows - a.shape[0]), (0, cols - a.shape[1])))


def _hyena_spectrum(l, c, fw1, fb1, fw2, fb2, fw3, freq):
    g, _, f1k, _ = _dft_tables()
    hp = LANES
    zfeat = _position_features(l)
    w1 = _pad2(fw1, LANES, hp)
    w2 = _pad2(fw2, hp, hp)
    w3 = _pad2(fw3, hp, fw3.shape[1])
    row = lambda v: _pad2(v[None, :], 1, hp)
    deltas = jnp.abs(jnp.linspace(math.log(HY_DECAY_TARGET) / HY_LONG_DECAY_PCT,
                                  math.log(HY_DECAY_TARGET) / HY_SHORT_DECAY_PCT, c, dtype=F32))[None, :]
    hfg, nrm = _hyena_filter_taps(zfeat, w1, row(fb1), row(freq), w2, row(fb2), w3, deltas)
    return _filter_spectrum(hfg, jnp.asarray(f1k), jnp.asarray(g), nrm)


def _hyena_branch(hyu, hy_gate, conv_w, conv_b, hre, him, bias):
    b, l, c3 = hyu.shape
    c = c3 // 3
    g, ginv, f1k, a1k = (jnp.asarray(t) for t in _dft_tables())
    x0c, vf = _hyena_short_conv(hyu, conv_w, conv_b[None, :])
    return _hyena_fftconv(vf, x0c, hy_gate, f1k, a1k, g, ginv, hre, him, bias[None, :])


def _ssd_kernel(xs_ref, bm_ref, cm_ref, dt_ref, dtb_ref, alog_ref, rep_ref, dsk_ref, mask_ref,
                *rest, reverse, lane0, chunks):
    yo_ref = rest[0] if len(rest) == 3 else None
    y_ref, prev_ref = rest[-2:]
    q, n, hd = SSD_CHUNK, SSD_STATE, SSD_HEAD_DIM
    gw = (SSD_HEADS // SSD_GROUPS) * hd

    @pl.when(pl.program_id(1) == 0)
    def _():
        prev_ref[...] = jnp.zeros_like(prev_ref)

    mask = mask_ref[...]
    maskb = mask > 0
    neg_a = -jnp.exp(alog_ref[...])
    lo_lanes = lax.broadcasted_iota(jnp.int32, (1, LANES), 1) < hd
    d = functools.partial(jnp.dot, preferred_element_type=F32)
    npair = gw // LANES
    edge = 0 if reverse else q - 1

    def per_pair(v, c0, shape):
        b0 = jnp.broadcast_to(v[:, c0:c0 + 1], shape)
        b1 = jnp.broadcast_to(v[:, c0 + 1:c0 + 2], shape)
        return jnp.where(lo_lanes, b0, b1)

    order = range(chunks - 1, -1, -1) if reverse else range(chunks)
    for k in order:
        rows = pl.ds(k * q, q)
        xs_b = xs_ref[0, rows, :]
        xs = xs_b.astype(F32)
        bm = bm_ref[0, rows, :]
        cm = cm_ref[0, rows, :]
        dt = _softplus(dt_ref[0, rows, :] + dtb_ref[...])
        cum = _sel_dot(mask, dt * neg_a)
        cum_t, dt_t = cum.T, dt.T
        cum_e = cum[edge:edge + 1, :]
        e_in = jnp.exp(cum)
        w_out = dt * jnp.exp(cum_e - cum)
        e_all = jnp.exp(cum_e)
        spread = d(jnp.concatenate([e_in, w_out], axis=0).astype(BF16), rep_ref[...])
        e_x = spread[:q]
        xsd_all = (xs * spread[q:]).astype(BF16)
        ys = []
        for g in range(SSD_GROUPS):
            cg = cm[:, g * n:(g + 1) * n].astype(BF16)
            bg = bm[:, g * n:(g + 1) * n].astype(BF16)
            cb = lax.dot_general(cg, bg, (((1,), (1,)), ((), ())), preferred_element_type=F32)
            sl = slice(g * gw, (g + 1) * gw)
            prev_g = prev_ref[:, sl]
            y_off = d(cg, prev_g.astype(BF16))
            cds = []
            for pair in range(npair):
                lo = g * gw + pair * LANES
                c0 = lane0 + lo // hd
                ms = []
                for c in (c0, c0 + 1):
                    seg = jnp.exp(jnp.where(maskb, cum[:, c:c + 1] - cum_t[c:c + 1, :], -jnp.inf))
                    ms.append((cb * seg * dt_t[c:c + 1, :]).astype(BF16))
                xp = xs_b[:, lo:lo + LANES]
                zero = jnp.zeros_like(xp)
                rhs = jnp.concatenate([jnp.where(lo_lanes, xp, zero), jnp.where(lo_lanes, zero, xp)], axis=0)
                yd = d(jnp.concatenate(ms, axis=1), rhs)
                ys.append(yd + y_off[:, pair * LANES:(pair + 1) * LANES] * e_x[:, lo:lo + LANES])
                cds.append(per_pair(e_all, c0, (n, LANES)))
            st = lax.dot_general(bg, xsd_all[:, sl], (((0,), (0,)), ((), ())), preferred_element_type=F32)
            prev_ref[:, sl] = prev_g * jnp.concatenate(cds, axis=1) + st
        y = jnp.concatenate(ys, axis=1)
        if not reverse:
            y = y + xs * dsk_ref[...]
        if yo_ref is not None:
            y = y + yo_ref[0, rows, :].astype(F32)
        y_ref[0, rows, :] = y.astype(y_ref.dtype)


def _ssd_direction(xbc, dt_raw, dtb, alog, rep, dskip, mask, y_other=None, *, reverse, ts=512):
    b, l, _ = xbc.shape
    hp = SSD_HEADS * SSD_HEAD_DIM
    gn = SSD_GROUPS * SSD_STATE
    nt = l // ts
    tmap = (lambda j: nt - 1 - j) if reverse else (lambda j: j)
    full = lambda shape: pl.BlockSpec(shape, lambda i, j: (0,) * len(shape))
    yspec = pl.BlockSpec((1, ts, hp), lambda i, j: (i, tmap(j), 0))
    extra = [] if y_other is None else [y_other]
    return pl.pallas_call(
        functools.partial(_ssd_kernel, reverse=reverse, lane0=SSD_HEADS if reverse else 0,
                          chunks=ts // SSD_CHUNK),
        grid=(b, nt),
        in_specs=[yspec,
                  pl.BlockSpec((1, ts, gn), lambda i, j: (i, tmap(j), hp // gn)),
                  pl.BlockSpec((1, ts, gn), lambda i, j: (i, tmap(j), hp // gn + 1)),
                  pl.BlockSpec((1, ts, LANES), lambda i, j: (i, tmap(j), 0)),
                  full((1, LANES)), full((1, LANES)), full((LANES, hp)), full((1, hp)),
                  full((SSD_CHUNK, SSD_CHUNK))] + [yspec] * len(extra),
        out_specs=yspec,
        out_shape=jax.ShapeDtypeStruct((b, l, hp), BF16),
        scratch_shapes=[pltpu.VMEM((SSD_STATE, hp), F32)],
        compiler_params=_cparams("parallel", "arbitrary"),
        name="ssd_bwd" if reverse else "ssd_fwd",
    )(xbc, xbc, xbc, dt_raw, dtb, alog, rep, dskip, mask, *extra)


def _ssd_branch(xbc, dt_raw, conv_w, conv_b, dt_bias, a_log, d_skip):
    xc = _conv_silu(xbc, conv_w, conv_b[None, :])
    pad_row = lambda v: _pad2(v.reshape(1, -1), 1, LANES)
    dtb, alog = pad_row(dt_bias), pad_row(a_log)
    dskip = jnp.repeat(d_skip, SSD_HEAD_DIM)[None, :]
    lane_head = np.arange(SSD_HEADS * SSD_HEAD_DIM)[None, :] // SSD_HEAD_DIM
    src = np.arange(LANES)[:, None]
    idx = np.arange(SSD_CHUNK)
    y = None
    for reverse in (False, True):
        rep = jnp.asarray(src == lane_head + (SSD_HEADS if reverse else 0), dtype=BF16)
        allowed = (idx[None, :] >= idx[:, None]) if reverse else (idx[None, :] <= idx[:, None])
        mask = jnp.asarray(allowed, dtype=BF16)
        y = _ssd_direction(xc, dt_raw, dtb, alog, rep, dskip, mask, y, reverse=reverse)
    return y


_LRU_ROWS = 256


def _lru_kernel(x_ref, gt_ref, cw_ref, cb_ref, wa_ref, wx_ref, ba_ref, bx_ref, lam_ref,
                o_ref, pad_ref, *seq_refs, l, heads):
    _fill_padded(pad_ref, x_ref.at[0], l)
    d = functools.partial(jnp.dot, preferred_element_type=F32)
    cw, cb = cw_ref[...], cb_ref[...]
    e0 = (-0.5 * LRU_C / math.log(2.0)) * _softplus(-lam_ref[...])
    ba, bx = ba_ref[...], bx_ref[...]
    a_refs, u_refs, h_refs = ([[seq_refs[(q * 2 + dr) * heads + hd] for hd in range(heads)] for dr in range(2)]
                              for q in range(3))

    def gates(i, carry):
        r = pl.multiple_of(i * _CONV_ROWS, _CONV_ROWS)
        xc = _conv_rows_dyn(pad_ref, r, cw, cb, 4, 2)
        for hd in range(heads):
            ls = slice(hd * LRU_BLOCK, (hd + 1) * LRU_BLOCK)
            xh = xc[:, ls]
            xb = xh.astype(BF16)
            xh2 = 0.5 * xh
            for dr in range(2):
                tr = jnp.tanh(d(xb, wa_ref[dr, hd]) + ba[dr:dr + 1, ls])
                ti = jnp.tanh(d(xb, wx_ref[dr, hd]) + bx[dr:dr + 1, ls])
                e = e0[dr:dr + 1, ls]
                a = jnp.exp2(e + e * tr)
                s = 1.0 - a * a
                root = jnp.where(s > 0.0, s * lax.rsqrt(s), 0.0)
                u = (xh2 + xh2 * ti) * root
                a_refs[dr][hd][pl.ds(r, _CONV_ROWS), :] = a
                u_refs[dr][hd][pl.ds(r, _CONV_ROWS), :] = u
        return carry

    lax.fori_loop(0, l // _CONV_ROWS, gates, 0)

    row = lax.broadcasted_iota(jnp.int32, (SUBLANES, LRU_BLOCK), 0)
    blk = SUBLANES * SUBLANES
    nblk = l // blk

    def sublane_scan(a, u, reverse):
        for s in (1, 2, 4):
            sh = (SUBLANES - s) if reverse else s
            valid = (row < SUBLANES - s) if reverse else (row >= s)
            a_sh = jnp.where(valid, pltpu.roll(a, sh, axis=0), 1.0)
            u_sh = jnp.where(valid, pltpu.roll(u, sh, axis=0), 0.0)
            u = u + a * u_sh
            a = a * a_sh
        return a, u

    def block_scan(a_ref, u_ref, h_ref, r, h_in, reverse):
        order = range(SUBLANES - 1, -1, -1) if reverse else range(SUBLANES)
        hs, ps = {}, {}
        h = p = None
        for k in order:
            rows = pl.ds(r + k, SUBLANES, stride=SUBLANES)
            a, u = a_ref[rows, :], u_ref[rows, :]
            h = u if h is None else a * h + u
            p = a if p is None else a * p
            hs[k], ps[k] = h, p
        sp, se = sublane_scan(p, h, reverse)
        full = se + sp * h_in
        edge = 0 if reverse else SUBLANES - 1
        h_out = jnp.broadcast_to(full[edge:edge + 1, :], (SUBLANES, LRU_BLOCK))
        if reverse:
            carry = jnp.where(row == SUBLANES - 1, h_in, pltpu.roll(full, SUBLANES - 1, axis=0))
        else:
            carry = jnp.where(row == 0, h_in, pltpu.roll(full, 1, axis=0))
        for k in order:
            h_ref[pl.ds(r + k, SUBLANES, stride=SUBLANES), :] = hs[k] + ps[k] * carry
        return h_out

    def scan(i, carry):
        rf = pl.multiple_of(i * blk, blk)
        rb = pl.multiple_of((nblk - 1 - i) * blk, blk)
        out = []
        for hd in range(heads):
            hf, hb = carry[2 * hd], carry[2 * hd + 1]
            out.append(block_scan(a_refs[0][hd], u_refs[0][hd], h_refs[0][hd], rf, hf, False))
            out.append(block_scan(a_refs[1][hd], u_refs[1][hd], h_refs[1][hd], rb, hb, True))
        return tuple(out)

    zero = jnp.zeros((SUBLANES, LRU_BLOCK), F32)
    lax.fori_loop(0, nblk, scan, (zero,) * (2 * heads), unroll=2)

    def combine(i, carry):
        r = pl.multiple_of(i * _LRU_ROWS, _LRU_ROWS)
        rows = pl.ds(r, _LRU_ROWS)
        h = jnp.concatenate([h_refs[0][hd][rows, :] + h_refs[1][hd][rows, :] for hd in range(heads)], axis=1)
        o_ref[0, rows, :] = (h * _silu(gt_ref[0, rows, :].astype(F32))).astype(o_ref.dtype)
        return carry

    lax.fori_loop(0, l // _LRU_ROWS, combine, 0)


def _rglru_mixer(xb, gate, conv_w, conv_b, w_a, w_x, b_a, b_x, lam, heads=2):
    b, l, w = xb.shape
    tw = heads * LRU_BLOCK
    xspec = pl.BlockSpec((1, l, tw), lambda i, j: (i, 0, j))
    wspec = pl.BlockSpec((2, heads, LRU_BLOCK, LRU_BLOCK), lambda i, j: (0, j, 0, 0))
    vspec = pl.BlockSpec((2, tw), lambda i, j: (0, j))
    seqs = [pltpu.VMEM((l, LRU_BLOCK), F32)] * (3 * 2 * heads)
    return pl.pallas_call(
        functools.partial(_lru_kernel, l=l, heads=heads),
        grid=(b, w // tw),
        in_specs=[xspec, xspec,
                  pl.BlockSpec((4, tw), lambda i, j: (0, j)),
                  pl.BlockSpec((1, tw), lambda i, j: (0, j)),
                  wspec, wspec, vspec, vspec, vspec],
        out_specs=xspec,
        out_shape=jax.ShapeDtypeStruct((b, l, w), BF16),
        scratch_shapes=[pltpu.VMEM((l + 2 * _HALO, tw), F32)] + seqs,
        compiler_params=_cparams("parallel", "parallel"),
        name="rglru",
    )(xb, gate, conv_w, conv_b[None, :], (0.5 * w_a).astype(BF16), (0.5 * w_x).astype(BF16),
      0.5 * b_a, 0.5 * b_x, lam)


def _rms(x, g, eps):
    return (x * lax.rsqrt(jnp.mean(x * x, axis=-1, keepdims=True) + eps)) * g


def _norm_modulate(x, g, shift, scale):
    ms = jnp.mean(x * x, axis=-1, keepdims=True)
    return ((x * lax.rsqrt(ms + 1e-6)) * g) * (1.0 + scale) + shift


def _outproj_even_kernel(*refs, n_x, bp):
    x_refs, (ya_ref, ys_ref, z_ref, gm_ref, ng_ref, wa_ref, wb_ref, o_ref) = refs[:n_x], refs[n_x:]
    yb = ys_ref[0].astype(F32) * _silu(z_ref[0].astype(F32))
    ng = ng_ref[...]
    gw = yb.shape[1] // SSD_GROUPS
    parts = [_rms(yb[:, g * gw:(g + 1) * gw], ng[:, g * gw:(g + 1) * gw], 1e-5) for g in range(SSD_GROUPS)]
    ybn = jnp.concatenate(parts, axis=1)
    d = functools.partial(jnp.dot, preferred_element_type=F32)
    acc = d(ya_ref[0].astype(BF16), wa_ref[...]) + d(ybn.astype(BF16), wb_ref[...])
    o_ref[0] = _load_rows(x_refs, bp) + gm_ref[0] * acc


def _out_projection_even(x, ya, ys, z, gate_mod, norm_g, w_out, tl=512):
    b, l, c = ya.shape
    dm = x[0].shape[2]
    rows = lambda n: pl.BlockSpec((1, tl, n), lambda i, j: (i, j, 0))
    wa, wb = w_out[:c].astype(BF16), w_out[c:].astype(BF16)
    return pl.pallas_call(
        functools.partial(_outproj_even_kernel, n_x=len(x), bp=x[0].shape[0]),
        grid=(b, l // tl),
        in_specs=_row_specs(x, tl) + [rows(c), rows(c), rows(c),
                                      pl.BlockSpec((1, 1, dm), lambda i, j: (i, 0, 0)),
                                      pl.BlockSpec((1, c), lambda i, j: (0, 0)),
                                      _resident(wa.shape, lambda i, j: (0, 0)),
                                      _resident(wb.shape, lambda i, j: (0, 0))],
        out_specs=rows(dm),
        out_shape=jax.ShapeDtypeStruct((b, l, dm), F32),
        compiler_params=_cparams("arbitrary", "arbitrary"),
        name="out_proj_even",
    )(*x, ya, ys, z, gate_mod, norm_g, wa, wb)


def _outproj_odd_kernel(x_ref, y_ref, gm_ref, w_ref, fg_ref, *o_refs, bp):
    acc = jnp.dot(y_ref[0].astype(BF16), w_ref[...], preferred_element_type=F32)
    xn = x_ref[0] + gm_ref[0] * acc
    if len(o_refs) == 1:
        o_refs[0][0] = xn
        return
    out = _rms(xn, fg_ref[...], 1e-6)

    @pl.when(pl.program_id(0) < bp)
    def _():
        o_refs[0][0] = out

    @pl.when(pl.program_id(0) >= bp)
    def _():
        o_refs[1][0] = out


def _out_projection_odd(x, y, gate_mod, w_out, final_g, split_at=None, tl=512):
    b, l, dm = x.shape
    w = y.shape[2]
    rows = lambda n: pl.BlockSpec((1, tl, n), lambda i, j: (i, j, 0))
    wb = w_out.astype(BF16)
    if split_at is None:
        out_specs = [rows(dm)]
        out_shape = [jax.ShapeDtypeStruct((b, l, dm), F32)]
    else:
        out_specs = [pl.BlockSpec((1, tl, dm), m) for m in _pair_row_maps(split_at, l // tl)]
        out_shape = [jax.ShapeDtypeStruct((n, l, dm), F32) for n in (split_at, b - split_at)]
    out = pl.pallas_call(
        functools.partial(_outproj_odd_kernel, bp=split_at),
        grid=(b, l // tl),
        in_specs=[rows(dm), rows(w),
                  pl.BlockSpec((1, 1, dm), lambda i, j: (i, 0, 0)),
                  _resident(wb.shape, lambda i, j: (0, 0)),
                  pl.BlockSpec((1, dm), lambda i, j: (0, 0))],
        out_specs=out_specs,
        out_shape=out_shape,
        compiler_params=_cparams("arbitrary", "arbitrary"),
        name="out_proj_odd",
    )(x, y, gate_mod, wb, final_g)
    return out[0] if split_at is None else tuple(out)


def kernel(x_prompt, x_sample, c_prompt, c_sample, mod_w, mod_b, norm_g, final_g, ev_w_in, ev_w_out, hy_conv_w, hy_conv_b, hy_fw1, hy_fb1, hy_fw2, hy_fb2, hy_fw3, hy_freq, hy_bias, ssd_conv_w, ssd_conv_b, ssd_dt_bias, ssd_A_log, ssd_D, ssd_norm_g, od_w_in, od_w_out, lru_conv_w, lru_conv_b, lru_w_a, lru_b_a, lru_w_x, lru_b_x, lru_lam):
    bp, l, dm = x_prompt.shape
    assert x_sample.shape[1:] == (l, dm), "the two request groups are stacked on the batch axis"
    x = (x_prompt, x_sample)
    c = jnp.concatenate([c_prompt, c_sample], axis=0)
    b = c.shape[0]
    depth = mod_w.shape[0]
    assert 2 * l == DFT_N1 * DFT_N2
    assert depth % 2 == 0, "the final norm is fused into the last (odd) layer's output projection"

    b_pad = -(-b // SUBLANES) * SUBLANES
    mods = _modulation(jnp.pad(c, ((0, b_pad - b), (0, 0))), mod_w, mod_b)[:, :b]
    shift, scale, gate_mod = (mods[:, :, k * dm:(k + 1) * dm].reshape(depth, b, 1, dm) for k in range(3))

    hyc = hy_bias.shape[1]
    ssw = ssd_norm_g.shape[1]
    xbw = ssd_conv_w.shape[2]
    o1, o2, o3, o4 = 3 * hyc, 4 * hyc, 4 * hyc + ssw, 4 * hyc + ssw + xbw

    for i in range(depth):
        j = i // 2
        g = norm_g[i][None, :]
        if i % 2 == 0:
            w_in = ev_w_in[j].astype(BF16)
            w_dt = jnp.pad(w_in[:, o4:], ((0, 0), (0, LANES - (w_in.shape[1] - o4))))
            hyu, hyg, z, xbc, dt_raw = _in_projection(
                x, shift[i], scale[i], g, [w_in[:, :o1], w_in[:, o1:o2], w_in[:, o2:o3], w_in[:, o3:o4], w_dt],
                [BF16, BF16, BF16, BF16, F32])
            hre, him = _hyena_spectrum(l, hyc, hy_fw1[j], hy_fb1[j], hy_fw2[j], hy_fb2[j], hy_fw3[j], hy_freq[j])
            ya = _hyena_branch(hyu, hyg, hy_conv_w[j], hy_conv_b[j], hre, him, hy_bias[j])
            ys = _ssd_branch(xbc, dt_raw, ssd_conv_w[j], ssd_conv_b[j], ssd_dt_bias[j], ssd_A_log[j], ssd_D[j])
            x = (_out_projection_even(x, ya, ys, z, gate_mod[i], ssd_norm_g[j][None, :], ev_w_out[j]),)
        else:
            w_in = od_w_in[j].astype(BF16)
            half = w_in.shape[1] // 2
            xb, gt = _in_projection(x, shift[i], scale[i], g, [w_in[:, :half], w_in[:, half:]], [BF16, BF16])
            y = _rglru_mixer(xb, gt, lru_conv_w[j], lru_conv_b[j], lru_w_a[j], lru_w_x[j],
                             lru_b_a[j], lru_b_x[j], lru_lam[j])
            if i == depth - 1:
                return _out_projection_odd(x[0], y, gate_mod[i], od_w_out[j], final_g[None, :], split_at=bp)
            x = (_out_projection_odd(x[0], y, gate_mod[i], od_w_out[j], final_g[None, :]),)
```

```python
import functools
import math

import numpy as np
import jax
import jax.numpy as jnp
from jax import lax
from jax.experimental import pallas as pl
from jax.experimental.pallas import tpu as pltpu

F32 = jnp.float32
BF16 = jnp.bfloat16
HI = lax.Precision.HIGHEST

V7X_VMEM_BYTES = 64 * 1024 * 1024
VMEM_LIMIT = V7X_VMEM_BYTES - 8 * 1024 * 1024
LANES = 128
SUBLANES = 8

HY_EMB = 33
HY_SHORT_DECAY_PCT = 0.3
HY_LONG_DECAY_PCT = 1.5
HY_DECAY_TARGET = 1e-2
SSD_HEADS = 16
SSD_HEAD_DIM = 64
SSD_GROUPS = 4
SSD_STATE = 128
SSD_CHUNK = 128
LRU_HEADS = 16
LRU_BLOCK = 128
LRU_C = 8.0

DFT_N1 = 64
DFT_N2 = 128
DFT_K1 = DFT_N1 // 2 + 1


def _cparams(*sem):
    return pltpu.CompilerParams(dimension_semantics=sem, vmem_limit_bytes=VMEM_LIMIT)


def _resident(shape, index_map):
    return pl.BlockSpec(shape, index_map, pipeline_mode=pl.Buffered(1))


def _sigmoid(x):
    return 0.5 + 0.5 * jnp.tanh(0.5 * x)


def _silu(x):
    return x * _sigmoid(x)


def _softplus(x):
    return jnp.maximum(x, 0.0) + jnp.log1p(jnp.exp(-jnp.abs(x)))


def _split3(x):
    hi = x.astype(BF16)
    r1 = x - hi.astype(F32)
    mid = r1.astype(BF16)
    lo = (r1 - mid.astype(F32)).astype(BF16)
    return hi, mid, lo


def _sel_dot(sel, x):
    hi, mid, lo = _split3(x)
    d = functools.partial(jnp.dot, preferred_element_type=F32)
    return d(sel, hi) + d(sel, mid) + d(sel, lo)


def _mod_kernel(c_ref, w_ref, b_ref, o_ref):
    cs = _silu(c_ref[...])
    o_ref[0] = jnp.dot(cs, w_ref[0], precision=HI, preferred_element_type=F32) + b_ref[0]


def _modulation(c_pad, mod_w, mod_b):
    depth, d, d3 = mod_w.shape
    bp = c_pad.shape[0]
    return pl.pallas_call(
        _mod_kernel,
        grid=(depth, d3 // d),
        in_specs=[
            pl.BlockSpec((bp, d), lambda i, j: (0, 0)),
            pl.BlockSpec((1, d, d), lambda i, j: (i, 0, j)),
            pl.BlockSpec((1, 1, d), lambda i, j: (i, 0, j)),
        ],
        out_specs=pl.BlockSpec((1, bp, d), lambda i, j: (i, 0, j)),
        out_shape=jax.ShapeDtypeStruct((depth, bp, d3), F32),
        compiler_params=_cparams("arbitrary", "arbitrary"),
        name="adaln_mod",
    )(c_pad, mod_w, mod_b.reshape(depth, 1, d3))


def _pair_row_maps(bp, nj):
    first = lambda i, j: (jnp.minimum(i, bp - 1), jnp.where(i < bp, j, nj - 1), 0)
    second = lambda i, j: (jnp.maximum(i - bp, 0), jnp.where(i < bp, 0, j), 0)
    return first, second


def _load_rows(x_refs, bp):
    if len(x_refs) == 1:
        return x_refs[0][0]
    return jnp.where(pl.program_id(0) < bp, x_refs[0][0], x_refs[1][0])


def _row_specs(xs, tl):
    l, d = xs[0].shape[1:]
    if len(xs) == 1:
        return [pl.BlockSpec((1, tl, d), lambda i, j: (i, j, 0))]
    return [pl.BlockSpec((1, tl, d), m) for m in _pair_row_maps(xs[0].shape[0], l // tl)]


def _inproj_kernel(*refs, n_x, bp, n_out):
    x_refs, (sh_ref, sc_ref, g_ref), refs = refs[:n_x], refs[n_x:n_x + 3], refs[n_x + 3:]
    w_refs, o_refs = refs[:n_out], refs[n_out:]
    hb = _norm_modulate(_load_rows(x_refs, bp), g_ref[...], sh_ref[0], sc_ref[0]).astype(BF16)
    for w_ref, o_ref in zip(w_refs, o_refs):
        o_ref[0] = jnp.dot(hb, w_ref[...], preferred_element_type=F32).astype(o_ref.dtype)


def _in_projection(x, shift, scale, g, weights, dtypes, tl=512):
    b = sum(a.shape[0] for a in x)
    l, d = x[0].shape[1:]
    n_out = len(weights)
    in_specs = _row_specs(x, tl) + [
        pl.BlockSpec((1, 1, d), lambda i, j: (i, 0, 0)),
        pl.BlockSpec((1, 1, d), lambda i, j: (i, 0, 0)),
        pl.BlockSpec((1, d), lambda i, j: (0, 0)),
    ] + [_resident(w.shape, lambda i, j: (0, 0)) for w in weights]
    out_specs = [pl.BlockSpec((1, tl, w.shape[1]), lambda i, j: (i, j, 0)) for w in weights]
    out_shape = [jax.ShapeDtypeStruct((b, l, w.shape[1]), dt) for w, dt in zip(weights, dtypes)]
    return pl.pallas_call(
        functools.partial(_inproj_kernel, n_x=len(x), bp=x[0].shape[0], n_out=n_out),
        grid=(b, l // tl),
        in_specs=in_specs,
        out_specs=out_specs,
        out_shape=out_shape,
        compiler_params=_cparams("arbitrary", "arbitrary"),
        name="in_proj",
    )(*x, shift, scale, g, *weights)


_CONV_ROWS = 256
_HALO = SUBLANES


def _fill_padded(pad_ref, src_ref, l):
    tc = pad_ref.shape[1]
    zeros = jnp.zeros((_HALO, tc), F32)
    pad_ref[pl.ds(0, _HALO), :] = zeros
    pad_ref[pl.ds(l + _HALO, _HALO), :] = zeros

    def body(i, carry):
        r = pl.multiple_of(i * _CONV_ROWS, _CONV_ROWS)
        pad_ref[pl.ds(r + _HALO, _CONV_ROWS), :] = src_ref[pl.ds(r, _CONV_ROWS), :].astype(F32)
        return carry

    lax.fori_loop(0, l // _CONV_ROWS, body, 0)


def _conv_rows(pad_ref, r, w, bias, taps, left):
    acc = None
    for k in range(taps):
        term = w[k:k + 1, :] * pad_ref[pl.ds(r + _HALO + k - left, _CONV_ROWS), :]
        acc = term if acc is None else acc + term
    return acc + bias


def _hy_conv_kernel(x0_ref, x1_ref, v_ref, w0_ref, w1_ref, w2_ref, b0_ref, b1_ref, b2_ref,
                    x0c_ref, vf_ref, p0_ref, p1_ref, p2_ref, *, l):
    _fill_padded(p0_ref, x0_ref.at[0], l)
    _fill_padded(p1_ref, x1_ref.at[0], l)
    _fill_padded(p2_ref, v_ref.at[0], l)
    w0, w1, w2 = w0_ref[...], w1_ref[...], w2_ref[...]
    b0, b1, b2 = b0_ref[...], b1_ref[...], b2_ref[...]

    for r in range(0, l, _CONV_ROWS):
        x0c_ref[0, pl.ds(r, _CONV_ROWS), :] = _conv_rows(p0_ref, r, w0, b0, 3, 1).astype(x0c_ref.dtype)
        x1c = _conv_rows(p1_ref, r, w1, b1, 3, 1)
        vc = _conv_rows(p2_ref, r, w2, b2, 3, 1)
        vf_ref[0, pl.ds(r, _CONV_ROWS), :] = vc * x1c


def _hyena_short_conv(hyu, conv_w, conv_b, tc=128):
    b, l, c3 = hyu.shape
    c = c3 // 3
    nct = c // tc
    xspec = lambda o: pl.BlockSpec((1, l, tc), lambda i, j, o=o: (i, 0, j + o * nct))
    wspec = lambda o: pl.BlockSpec((3, tc), lambda i, j, o=o: (0, j + o * nct))
    bspec = lambda o: pl.BlockSpec((1, tc), lambda i, j, o=o: (0, j + o * nct))
    ospec = pl.BlockSpec((1, l, tc), lambda i, j: (i, 0, j))
    return pl.pallas_call(
        functools.partial(_hy_conv_kernel, l=l),
        grid=(b, nct),
        in_specs=[xspec(0), xspec(1), xspec(2), wspec(0), wspec(1), wspec(2),
                  bspec(0), bspec(1), bspec(2)],
        out_specs=[ospec, ospec],
        out_shape=[jax.ShapeDtypeStruct((b, l, c), BF16), jax.ShapeDtypeStruct((b, l, c), F32)],
        scratch_shapes=[pltpu.VMEM((l + 2 * _HALO, tc), F32)] * 3,
        compiler_params=_cparams("parallel", "parallel"),
        name="hyena_short_conv",
    )(hyu, hyu, hyu, conv_w, conv_w, conv_w, conv_b, conv_b, conv_b)


def _conv_silu_kernel(x_ref, w_ref, b_ref, o_ref, p_ref, *, l, taps, left):
    _fill_padded(p_ref, x_ref.at[0], l)
    w, bias = w_ref[...], b_ref[...]

    for r in range(0, l, _CONV_ROWS):
        o_ref[0, pl.ds(r, _CONV_ROWS), :] = _silu(_conv_rows(p_ref, r, w, bias, taps, left)).astype(o_ref.dtype)


def _conv_silu(x, conv_w, conv_b, tc=256):
    b, l, c = x.shape
    taps = conv_w.shape[0]
    return pl.pallas_call(
        functools.partial(_conv_silu_kernel, l=l, taps=taps, left=taps // 2),
        grid=(b, c // tc),
        in_specs=[pl.BlockSpec((1, l, tc), lambda i, j: (i, 0, j)),
                  pl.BlockSpec((taps, tc), lambda i, j: (0, j)),
                  pl.BlockSpec((1, tc), lambda i, j: (0, j))],
        out_specs=pl.BlockSpec((1, l, tc), lambda i, j: (i, 0, j)),
        out_shape=jax.ShapeDtypeStruct((b, l, c), BF16),
        scratch_shapes=[pltpu.VMEM((l + 2 * _HALO, tc), F32)],
        compiler_params=_cparams("parallel", "parallel"),
        name="ssd_conv_silu",
    )(x, conv_w, conv_b)


def _filter_kernel(z_ref, w1_ref, b1_ref, fr_ref, w2_ref, b2_ref, w3_ref, dl_ref,
                   hfg_ref, nrm_ref, *, tr, c):
    i = pl.program_id(0)
    z = z_ref[...]
    fr = fr_ref[...]
    d = functools.partial(jnp.dot, precision=HI, preferred_element_type=F32)
    h = jnp.sin(fr * (d(z, w1_ref[...]) + b1_ref[...]))
    h = jnp.sin(fr * (d(h, w2_ref[...]) + b2_ref[...]))
    h = d(h, w3_ref[...])
    window = jnp.exp(-z[:, 0:1] * dl_ref[...])
    hf = h[:, :c] * window
    row = lax.broadcasted_iota(jnp.int32, (tr, 1), 0) + i * tr
    g = jnp.where(row == 0, 0.0, h[:, c:] * window)
    hfg_ref[0] = hf
    hfg_ref[1] = g
    part = jnp.sum(jnp.abs(hf) + jnp.abs(g), axis=0, keepdims=True)

    @pl.when(i == 0)
    def _():
        nrm_ref[...] = part

    @pl.when(i != 0)
    def _():
        nrm_ref[...] = nrm_ref[...] + part


def _hyena_filter_taps(zfeat, w1, b1, freq, w2, b2, w3, deltas, tr=512):
    l, zp = zfeat.shape
    hp = w1.shape[1]
    c2 = w3.shape[1]
    c = c2 // 2
    full = lambda shape: pl.BlockSpec(shape, lambda i: (0,) * len(shape))
    return pl.pallas_call(
        functools.partial(_filter_kernel, tr=tr, c=c),
        grid=(l // tr,),
        in_specs=[pl.BlockSpec((tr, zp), lambda i: (i, 0)), full((zp, hp)), full((1, hp)),
                  full((1, hp)), full((hp, hp)), full((1, hp)), full((hp, c2)), full((1, c))],
        out_specs=[pl.BlockSpec((2, tr, c), lambda i: (0, i, 0)), full((1, c))],
        out_shape=[jax.ShapeDtypeStruct((2, l, c), F32), jax.ShapeDtypeStruct((1, c), F32)],
        compiler_params=_cparams("arbitrary"),
        name="hyena_filter_mlp",
    )(zfeat, w1, b1, freq, w2, b2, w3, deltas)


def _filter_spec_kernel(h_ref, f1_ref, g_ref, nrm_ref, hre_ref, him_ref, s_ref):
    _, nh, ng, sub, tc = h_ref.shape
    n2 = ng * sub
    nreal = s_ref.shape[1]
    k1 = nreal // 2 + 1
    d = functools.partial(jnp.dot, preferred_element_type=F32)
    f1 = f1_ref[...]
    for q in range(2):
        for g in range(ng):
            x = h_ref[q, :, g].reshape(nh * sub, tc).astype(BF16)
            s_ref[q, :, g] = d(f1, x).reshape(nreal, sub, tc)
    inv = 1.0 / nrm_ref[...]
    zero = jnp.zeros((n2, tc), F32)

    def spectrum(k, has_im):
        xs = []
        for q in range(2):
            re = s_ref[q, k].reshape(n2, tc)
            im = s_ref[q, k1 - 1 + k].reshape(n2, tc) if has_im else zero
            xs.append(d(g_ref[k], jnp.concatenate([re, im], axis=0).astype(BF16)))
        xf, xg = xs
        hre_ref[k] = (xf[:n2] + xg[:n2]) * inv
        him_ref[k] = (xf[n2:] - xg[n2:]) * inv

    spectrum(0, False)
    spectrum(k1 - 1, False)

    def body(k, carry):
        spectrum(k, True)
        return carry

    lax.fori_loop(1, k1 - 1, body, 0, unroll=_LEVEL2_UNROLL)


def _filter_spectrum(hfg, f1k, g, nrm, tc=128):
    _, l, c = hfg.shape
    nh, sub = DFT_N1 // 2, SUBLANES
    ng = DFT_N2 // sub
    k1, n2 = g.shape[0], DFT_N2
    return pl.pallas_call(
        _filter_spec_kernel,
        grid=(c // tc,),
        in_specs=[pl.BlockSpec((2, nh, ng, sub, tc), lambda j: (0, 0, 0, 0, j)),
                  _resident(f1k.shape, lambda j: (0, 0)), _resident(g.shape, lambda j: (0, 0, 0)),
                  pl.BlockSpec((1, tc), lambda j: (0, j))],
        out_specs=[pl.BlockSpec((k1, n2, tc), lambda j: (0, 0, j))] * 2,
        out_shape=[jax.ShapeDtypeStruct((k1, n2, c), F32)] * 2,
        scratch_shapes=[pltpu.VMEM((2, DFT_N1, ng, sub, tc), F32)],
        compiler_params=_cparams("arbitrary"),
        name="hyena_filter_spectrum",
    )(hfg.reshape(2, nh, ng, sub, c), f1k, g, nrm)


_LEVEL2_UNROLL = 16


def _fftconv_kernel(vf_ref, x0_ref, gt_ref, f1_ref, a1_ref, g_ref, gi_ref, hre_ref, him_ref, bias_ref,
                    o_ref, s_ref, y_ref):
    nh, ng, sub, tc = vf_ref.shape[1:]
    n2 = ng * sub
    nreal = s_ref.shape[0]
    k1 = nreal // 2 + 1
    d = functools.partial(jnp.dot, preferred_element_type=F32)

    f1 = f1_ref[...]
    for g in range(ng):
        x = vf_ref[0, :, g].reshape(nh * sub, tc).astype(BF16)
        s_ref[:, g] = d(f1, x).reshape(nreal, sub, tc)

    def level2(k, re, im):
        x = d(g_ref[k], jnp.concatenate([re, im], axis=0).astype(BF16))
        xr, xi = x[:n2], x[n2:]
        hr, hi = hre_ref[k], him_ref[k]
        y = jnp.concatenate([xr * hr - xi * hi, xr * hi + xi * hr], axis=0).astype(BF16)
        return d(gi_ref[k], y)

    zero = jnp.zeros((n2, tc), F32)
    for k in (0, k1 - 1):
        z = level2(k, s_ref[k].reshape(n2, tc), zero)
        s_ref[k] = z[:n2].reshape(ng, sub, tc)

    def body(k, carry):
        z = level2(k, s_ref[k].reshape(n2, tc), s_ref[k1 - 1 + k].reshape(n2, tc))
        s_ref[k] = z[:n2].reshape(ng, sub, tc)
        s_ref[k1 - 1 + k] = z[n2:].reshape(ng, sub, tc)
        return carry

    lax.fori_loop(1, k1 - 1, body, 0, unroll=_LEVEL2_UNROLL)

    a1 = a1_ref[...]
    for g in range(ng):
        z = s_ref[:, g].reshape(nreal * sub, tc).astype(BF16)
        y_ref[:, g] = d(a1, z).reshape(nh, sub, tc)

    bias = bias_ref[...]
    step = 2
    rows = step * n2

    def epilogue(i, carry):
        r = pl.multiple_of(i * rows, rows)
        y = y_ref[pl.ds(i * step, step)].reshape(rows, tc)
        vf = vf_ref[0, pl.ds(i * step, step)].reshape(rows, tc)
        x0 = x0_ref[0, pl.ds(r, rows), :].astype(F32)
        gt = gt_ref[0, pl.ds(r, rows), :].astype(F32)
        o_ref[0, pl.ds(r, rows), :] = ((x0 * (y + vf * bias)) * _silu(gt)).astype(o_ref.dtype)
        return carry

    lax.fori_loop(0, nh // step, epilogue, 0)


def _hyena_fftconv(vf, x0c, gate, f1k, a1k, g, ginv, hre, him, bias, tc=256):
    b, l, c = vf.shape
    nh, sub = DFT_N1 // 2, SUBLANES
    ng = DFT_N2 // sub
    kb = g.shape[0]
    n2 = DFT_N2
    rows = pl.BlockSpec((1, l, tc), lambda j, i: (i, 0, j))
    return pl.pallas_call(
        _fftconv_kernel,
        grid=(c // tc, b),
        in_specs=[pl.BlockSpec((1, nh, ng, sub, tc), lambda j, i: (i, 0, 0, 0, j)), rows, rows,
                  _resident(f1k.shape, lambda j, i: (0, 0)), _resident(a1k.shape, lambda j, i: (0, 0)),
                  _resident(g.shape, lambda j, i: (0, 0, 0)), _resident(ginv.shape, lambda j, i: (0, 0, 0)),
                  _resident((kb, n2, tc), lambda j, i: (0, 0, j)), _resident((kb, n2, tc), lambda j, i: (0, 0, j)),
                  pl.BlockSpec((1, tc), lambda j, i: (0, j))],
        out_specs=rows,
        out_shape=jax.ShapeDtypeStruct((b, l, c), BF16),
        scratch_shapes=[pltpu.VMEM((DFT_N1, ng, sub, tc), F32), pltpu.VMEM((nh, ng, sub, tc), F32)],
        compiler_params=_cparams("arbitrary", "arbitrary"),
        name="hyena_fftconv",
    )(vf.reshape(b, nh, ng, sub, c), x0c, gate, f1k, a1k, g, ginv, hre, him, bias)


@functools.lru_cache(maxsize=None)
def _dft_tables():
    n1, n2, k1 = DFT_N1, DFT_N2, DFT_K1
    n = n1 * n2
    nh = n1 // 2
    ang = 2.0 * np.pi * np.arange(k1)[:, None] * np.arange(nh)[None, :] / n1
    fre, fim = np.cos(ang), -np.sin(ang)
    m = np.arange(n2)[None, None, :]
    k2 = np.arange(n2)[None, :, None]
    ka = np.arange(k1)[:, None, None]
    ph = -2.0 * np.pi * m * (ka / n + k2 / n2)
    gr, gi = np.cos(ph), np.sin(ph)
    g = np.concatenate([np.concatenate([gr, -gi], axis=2), np.concatenate([gi, gr], axis=2)], axis=1)
    grt, git = np.swapaxes(gr, 1, 2), np.swapaxes(gi, 1, 2)
    ginv = np.concatenate([np.concatenate([grt, git], axis=2), np.concatenate([-git, grt], axis=2)], axis=1)
    wk = np.where((np.arange(k1) == 0) | (np.arange(k1) == n1 // 2), 1.0, 2.0)[None, :]
    are, aim = wk * fre.T / n, wk * fim.T / n
    eye = np.eye(SUBLANES)
    f1k = np.kron(np.concatenate([fre, fim[1:k1 - 1]], axis=0), eye)
    a1k = np.kron(np.concatenate([are, aim[:, 1:k1 - 1]], axis=1), eye)
    return tuple(np.asarray(a, np.float32).astype(BF16) for a in (g, ginv, f1k, a1k))


def _position_features(l):
    f32 = F32
    pos = jnp.arange(l, dtype=f32)[:, None]
    t = pos / max(l - 1, 1)
    bands = (HY_EMB - 1) // 2
    f = jnp.linspace(1e-4, bands - 1, bands, dtype=f32)[None, :]
    ang = f * pos * (2.0 * math.pi / l)
    z = jnp.concatenate([t, jnp.cos(ang), -jnp.sin(ang)], axis=-1)
    return jnp.pad(z, ((0, 0), (0, LANES - HY_EMB)))


def _pad2(a, rows, cols):
    return jnp.pad(a, ((0, rows - a.shape[0]), (0, cols - a.shape[1])))


def _hyena_spectrum(l, c, fw1, fb1, fw2, fb2, fw3, freq):
    g, _, f1k, _ = _dft_tables()
    hp = LANES
    zfeat = _position_features(l)
    w1 = _pad2(fw1, LANES, hp)
    w2 = _pad2(fw2, hp, hp)
    w3 = _pad2(fw3, hp, fw3.shape[1])
    row = lambda v: _pad2(v[None, :], 1, hp)
    deltas = jnp.abs(jnp.linspace(math.log(HY_DECAY_TARGET) / HY_LONG_DECAY_PCT,
                                  math.log(HY_DECAY_TARGET) / HY_SHORT_DECAY_PCT, c, dtype=F32))[None, :]
    hfg, nrm = _hyena_filter_taps(zfeat, w1, row(fb1), row(freq), w2, row(fb2), w3, deltas)
    return _filter_spectrum(hfg, jnp.asarray(f1k), jnp.asarray(g), nrm)


def _hyena_branch(hyu, hy_gate, conv_w, conv_b, hre, him, bias):
    b, l, c3 = hyu.shape
    c = c3 // 3
    g, ginv, f1k, a1k = (jnp.asarray(t) for t in _dft_tables())
    x0c, vf = _hyena_short_conv(hyu, conv_w, conv_b[None, :])
    return _hyena_fftconv(vf, x0c, hy_gate, f1k, a1k, g, ginv, hre, him, bias[None, :])


def _ssd_kernel(xs_ref, bm_ref, cm_ref, dt_ref, dtb_ref, alog_ref, rep_ref, dsk_ref, mask_ref,
                *rest, reverse, lane0, chunks):
    yo_ref = rest[0] if len(rest) == 3 else None
    y_ref, prev_ref = rest[-2:]
    q, n, hd = SSD_CHUNK, SSD_STATE, SSD_HEAD_DIM
    gw = (SSD_HEADS // SSD_GROUPS) * hd

    @pl.when(pl.program_id(1) == 0)
    def _():
        prev_ref[...] = jnp.zeros_like(prev_ref)

    mask = mask_ref[...]
    maskb = mask > 0
    neg_a = -jnp.exp(alog_ref[...])
    lo_lanes = lax.broadcasted_iota(jnp.int32, (1, LANES), 1) < hd
    d = functools.partial(jnp.dot, preferred_element_type=F32)
    npair = gw // LANES
    edge = 0 if reverse else q - 1

    def per_pair(v, c0, shape):
        b0 = jnp.broadcast_to(v[:, c0:c0 + 1], shape)
        b1 = jnp.broadcast_to(v[:, c0 + 1:c0 + 2], shape)
        return jnp.where(lo_lanes, b0, b1)

    order = range(chunks - 1, -1, -1) if reverse else range(chunks)
    for k in order:
        rows = pl.ds(k * q, q)
        xs_b = xs_ref[0, rows, :]
        bm = bm_ref[0, rows, :]
        cm = cm_ref[0, rows, :]
        dt = _softplus(dt_ref[0, rows, :] + dtb_ref[...])
        cum = _sel_dot(mask, dt * neg_a)
        src_t = (cum - jnp.log(dt)).T
        cum_e = cum[edge:edge + 1, :]
        e_in = jnp.exp(cum)
        w_out = dt * jnp.exp(cum_e - cum)
        e_all = jnp.exp(cum_e)
        spread = d(jnp.concatenate([e_in, w_out], axis=0).astype(BF16), rep_ref[...])
        e_x = spread[:q]
        xsd_all = xs_b * spread[q:].astype(BF16)
        ys = []
        for g in range(SSD_GROUPS):
            cg = cm[:, g * n:(g + 1) * n].astype(BF16)
            bg = bm[:, g * n:(g + 1) * n].astype(BF16)
            cb = lax.dot_general(cg, bg, (((1,), (1,)), ((), ())), preferred_element_type=F32)
            sl = slice(g * gw, (g + 1) * gw)
            prev_g = prev_ref[:, sl]
            y_off = d(cg, prev_g.astype(BF16))
            cds = []
            for pair in range(npair):
                lo = g * gw + pair * LANES
                c0 = lane0 + lo // hd
                ms = []
                for c in (c0, c0 + 1):
                    seg_dt = jnp.exp(jnp.where(maskb, cum[:, c:c + 1] - src_t[c:c + 1, :], -jnp.inf))
                    ms.append((cb * seg_dt).astype(BF16))
                xp = xs_b[:, lo:lo + LANES]
                zero = jnp.zeros_like(xp)
                rhs = jnp.concatenate([jnp.where(lo_lanes, xp, zero), jnp.where(lo_lanes, zero, xp)], axis=0)
                yd = d(jnp.concatenate(ms, axis=1), rhs)
                ys.append(yd + y_off[:, pair * LANES:(pair + 1) * LANES] * e_x[:, lo:lo + LANES])
                cds.append(per_pair(e_all, c0, (n, LANES)))
            st = lax.dot_general(bg, xsd_all[:, sl], (((0,), (0,)), ((), ())), preferred_element_type=F32)
            prev_ref[:, sl] = prev_g * jnp.concatenate(cds, axis=1) + st
        y = jnp.concatenate(ys, axis=1)
        if not reverse:
            y = y + xs_b.astype(F32) * dsk_ref[...]
        if yo_ref is not None:
            y = y + yo_ref[0, rows, :].astype(F32)
        y_ref[0, rows, :] = y.astype(y_ref.dtype)


def _ssd_direction(xbc, dt_raw, dtb, alog, rep, dskip, mask, y_other=None, *, reverse, ts=512):
    b, l, _ = xbc.shape
    hp = SSD_HEADS * SSD_HEAD_DIM
    gn = SSD_GROUPS * SSD_STATE
    nt = l // ts
    tmap = (lambda j: nt - 1 - j) if reverse else (lambda j: j)
    full = lambda shape: pl.BlockSpec(shape, lambda i, j: (0,) * len(shape))
    yspec = pl.BlockSpec((1, ts, hp), lambda i, j: (i, tmap(j), 0))
    extra = [] if y_other is None else [y_other]
    return pl.pallas_call(
        functools.partial(_ssd_kernel, reverse=reverse, lane0=SSD_HEADS if reverse else 0,
                          chunks=ts // SSD_CHUNK),
        grid=(b, nt),
        in_specs=[yspec,
                  pl.BlockSpec((1, ts, gn), lambda i, j: (i, tmap(j), hp // gn)),
                  pl.BlockSpec((1, ts, gn), lambda i, j: (i, tmap(j), hp // gn + 1)),
                  pl.BlockSpec((1, ts, LANES), lambda i, j: (i, tmap(j), 0)),
                  full((1, LANES)), full((1, LANES)), full((LANES, hp)), full((1, hp)),
                  full((SSD_CHUNK, SSD_CHUNK))] + [yspec] * len(extra),
        out_specs=yspec,
        out_shape=jax.ShapeDtypeStruct((b, l, hp), BF16),
        scratch_shapes=[pltpu.VMEM((SSD_STATE, hp), F32)],
        compiler_params=_cparams("parallel", "arbitrary"),
        name="ssd_bwd" if reverse else "ssd_fwd",
    )(xbc, xbc, xbc, dt_raw, dtb, alog, rep, dskip, mask, *extra)


def _ssd_branch(xbc, dt_raw, conv_w, conv_b, dt_bias, a_log, d_skip):
    xc = _conv_silu(xbc, conv_w, conv_b[None, :])
    pad_row = lambda v: _pad2(v.reshape(1, -1), 1, LANES)
    dtb, alog = pad_row(dt_bias), pad_row(a_log)
    dskip = jnp.repeat(d_skip, SSD_HEAD_DIM)[None, :]
    lane_head = np.arange(SSD_HEADS * SSD_HEAD_DIM)[None, :] // SSD_HEAD_DIM
    src = np.arange(LANES)[:, None]
    idx = np.arange(SSD_CHUNK)
    y = None
    for reverse in (False, True):
        rep = jnp.asarray(src == lane_head + (SSD_HEADS if reverse else 0), dtype=BF16)
        allowed = (idx[None, :] >= idx[:, None]) if reverse else (idx[None, :] <= idx[:, None])
        mask = jnp.asarray(allowed, dtype=BF16)
        y = _ssd_direction(xc, dt_raw, dtb, alog, rep, dskip, mask, y, reverse=reverse)
    return y


_LRU_ROWS = 256


def _lru_kernel(x_ref, gt_ref, cw_ref, cb_ref, wa_ref, wx_ref, ba_ref, bx_ref, lam_ref,
                o_ref, pad_ref, *seq_refs, l, heads):
    blk = SUBLANES * SUBLANES
    zeros = jnp.zeros((_HALO, LRU_BLOCK), F32)
    for hd in range(heads):
        pad_ref[hd, pl.ds(0, _HALO), :] = zeros
        pad_ref[hd, pl.ds(l + _HALO, _HALO), :] = zeros

    def fill(i, carry):
        r = pl.multiple_of(i * _CONV_ROWS, _CONV_ROWS)
        x = x_ref[0, pl.ds(r, _CONV_ROWS), :].astype(F32)
        for hd in range(heads):
            pad_ref[hd, pl.ds(r + _HALO, _CONV_ROWS), :] = x[:, hd * LRU_BLOCK:(hd + 1) * LRU_BLOCK]
        return carry

    lax.fori_loop(0, l // _CONV_ROWS, fill, 0)
    d = functools.partial(jnp.dot, preferred_element_type=F32)
    cw, cb = cw_ref[...], cb_ref[...]

    def conv_segment_order(hd, r):
        ls = slice(hd * LRU_BLOCK, (hd + 1) * LRU_BLOCK)
        w = [cw[k:k + 1, ls] for k in range(4)]
        bias = cb[:, ls]
        pieces = []
        for base in range(0, _CONV_ROWS, blk):
            tap = {m: pad_ref[hd, pl.ds(r + base + _HALO + m, SUBLANES, stride=SUBLANES), :]
                   for m in range(-2, SUBLANES + 1)}
            for j in range(SUBLANES):
                pieces.append(bias + w[0] * tap[j - 2] + w[1] * tap[j - 1] + w[2] * tap[j] + w[3] * tap[j + 1])
        return jnp.concatenate(pieces, axis=0)
    e0 = (-0.5 * LRU_C / math.log(2.0)) * _softplus(-lam_ref[...])
    ba, bx = ba_ref[...], bx_ref[...]
    a_refs, u_refs, h_refs = ([[seq_refs[(q * 2 + dr) * heads + hd] for hd in range(heads)] for dr in range(2)]
                              for q in range(3))

    def gates(i, carry):
        r = pl.multiple_of(i * _CONV_ROWS, _CONV_ROWS)
        for hd in range(heads):
            ls = slice(hd * LRU_BLOCK, (hd + 1) * LRU_BLOCK)
            xh = conv_segment_order(hd, r)
            xb = xh.astype(BF16)
            xh2 = 0.5 * xh
            for dr in range(2):
                tr = jnp.tanh(d(xb, wa_ref[dr, hd]) + ba[dr:dr + 1, ls])
                ti = jnp.tanh(d(xb, wx_ref[dr, hd]) + bx[dr:dr + 1, ls])
                e = e0[dr:dr + 1, ls]
                a = jnp.exp2(e + e * tr)
                s = 1.0 - a * a
                root = jnp.where(s > 0.0, s * lax.rsqrt(s), 0.0)
                u = (xh2 + xh2 * ti) * root
                a_refs[dr][hd][pl.ds(r, _CONV_ROWS), :] = a
                u_refs[dr][hd][pl.ds(r, _CONV_ROWS), :] = u
        return carry

    lax.fori_loop(0, l // _CONV_ROWS, gates, 0)

    row = lax.broadcasted_iota(jnp.int32, (SUBLANES, LRU_BLOCK), 0)
    nblk = l // blk

    def sublane_scan(a, u, reverse):
        for s in (1, 2, 4):
            sh = (SUBLANES - s) if reverse else s
            valid = (row < SUBLANES - s) if reverse else (row >= s)
            a_sh = jnp.where(valid, pltpu.roll(a, sh, axis=0), 1.0)
            u_sh = jnp.where(valid, pltpu.roll(u, sh, axis=0), 0.0)
            u = u + a * u_sh
            a = a * a_sh
        return a, u

    def block_scan(a_ref, u_ref, h_ref, r, h_in, reverse):
        order = range(SUBLANES - 1, -1, -1) if reverse else range(SUBLANES)
        hs, ps = {}, {}
        h = p = None
        for k in order:
            rows = pl.ds(r + SUBLANES * k, SUBLANES)
            a, u = a_ref[rows, :], u_ref[rows, :]
            h = u if h is None else a * h + u
            p = a if p is None else a * p
            hs[k], ps[k] = h, p
        sp, se = sublane_scan(p, h, reverse)
        full = se + sp * h_in
        edge = 0 if reverse else SUBLANES - 1
        h_out = jnp.broadcast_to(full[edge:edge + 1, :], (SUBLANES, LRU_BLOCK))
        if reverse:
            carry = jnp.where(row == SUBLANES - 1, h_in, pltpu.roll(full, SUBLANES - 1, axis=0))
        else:
            carry = jnp.where(row == 0, h_in, pltpu.roll(full, 1, axis=0))
        for k in order:
            h_ref[pl.ds(r + k, SUBLANES, stride=SUBLANES), :] = hs[k] + ps[k] * carry
        return h_out

    def scan(i, carry):
        rf = pl.multiple_of(i * blk, blk)
        rb = pl.multiple_of((nblk - 1 - i) * blk, blk)
        out = []
        for hd in range(heads):
            hf, hb = carry[2 * hd], carry[2 * hd + 1]
            out.append(block_scan(a_refs[0][hd], u_refs[0][hd], h_refs[0][hd], rf, hf, False))
            out.append(block_scan(a_refs[1][hd], u_refs[1][hd], h_refs[1][hd], rb, hb, True))
        return tuple(out)

    zero = jnp.zeros((SUBLANES, LRU_BLOCK), F32)
    lax.fori_loop(0, nblk, scan, (zero,) * (2 * heads), unroll=2)

    def combine(i, carry):
        r = pl.multiple_of(i * _LRU_ROWS, _LRU_ROWS)
        rows = pl.ds(r, _LRU_ROWS)
        h = jnp.concatenate([h_refs[0][hd][rows, :] + h_refs[1][hd][rows, :] for hd in range(heads)], axis=1)
        o_ref[0, rows, :] = (h * _silu(gt_ref[0, rows, :].astype(F32))).astype(o_ref.dtype)
        return carry

    lax.fori_loop(0, l // _LRU_ROWS, combine, 0)


def _rglru_mixer(xb, gate, conv_w, conv_b, w_a, w_x, b_a, b_x, lam, heads=2):
    b, l, w = xb.shape
    tw = heads * LRU_BLOCK
    xspec = pl.BlockSpec((1, l, tw), lambda i, j: (i, 0, j))
    wspec = pl.BlockSpec((2, heads, LRU_BLOCK, LRU_BLOCK), lambda i, j: (0, j, 0, 0))
    vspec = pl.BlockSpec((2, tw), lambda i, j: (0, j))
    seqs = [pltpu.VMEM((l, LRU_BLOCK), F32)] * (3 * 2 * heads)
    return pl.pallas_call(
        functools.partial(_lru_kernel, l=l, heads=heads),
        grid=(b, w // tw),
        in_specs=[xspec, xspec,
                  pl.BlockSpec((4, tw), lambda i, j: (0, j)),
                  pl.BlockSpec((1, tw), lambda i, j: (0, j)),
                  wspec, wspec, vspec, vspec, vspec],
        out_specs=xspec,
        out_shape=jax.ShapeDtypeStruct((b, l, w), BF16),
        scratch_shapes=[pltpu.VMEM((heads, l + 2 * _HALO, LRU_BLOCK), F32)] + seqs,
        compiler_params=_cparams("parallel", "parallel"),
        name="rglru",
    )(xb, gate, conv_w, conv_b[None, :], (0.5 * w_a).astype(BF16), (0.5 * w_x).astype(BF16),
      0.5 * b_a, 0.5 * b_x, lam)


def _rms(x, g, eps):
    return (x * lax.rsqrt(jnp.mean(x * x, axis=-1, keepdims=True) + eps)) * g


def _norm_modulate(x, g, shift, scale):
    ms = jnp.mean(x * x, axis=-1, keepdims=True)
    return ((x * lax.rsqrt(ms + 1e-6)) * g) * (1.0 + scale) + shift


def _outproj_even_kernel(*refs, n_x, bp):
    x_refs, (ya_ref, ys_ref, z_ref, gm_ref, ng_ref, wa_ref, wb_ref, o_ref) = refs[:n_x], refs[n_x:]
    yb = ys_ref[0].astype(F32) * _silu(z_ref[0].astype(F32))
    ng = ng_ref[...]
    gw = yb.shape[1] // SSD_GROUPS
    parts = [_rms(yb[:, g * gw:(g + 1) * gw], ng[:, g * gw:(g + 1) * gw], 1e-5) for g in range(SSD_GROUPS)]
    ybn = jnp.concatenate(parts, axis=1)
    d = functools.partial(jnp.dot, preferred_element_type=F32)
    acc = d(ya_ref[0].astype(BF16), wa_ref[...]) + d(ybn.astype(BF16), wb_ref[...])
    o_ref[0] = _load_rows(x_refs, bp) + gm_ref[0] * acc


def _out_projection_even(x, ya, ys, z, gate_mod, norm_g, w_out, tl=512):
    b, l, c = ya.shape
    dm = x[0].shape[2]
    rows = lambda n: pl.BlockSpec((1, tl, n), lambda i, j: (i, j, 0))
    wa, wb = w_out[:c].astype(BF16), w_out[c:].astype(BF16)
    return pl.pallas_call(
        functools.partial(_outproj_even_kernel, n_x=len(x), bp=x[0].shape[0]),
        grid=(b, l // tl),
        in_specs=_row_specs(x, tl) + [rows(c), rows(c), rows(c),
                                      pl.BlockSpec((1, 1, dm), lambda i, j: (i, 0, 0)),
                                      pl.BlockSpec((1, c), lambda i, j: (0, 0)),
                                      _resident(wa.shape, lambda i, j: (0, 0)),
                                      _resident(wb.shape, lambda i, j: (0, 0))],
        out_specs=rows(dm),
        out_shape=jax.ShapeDtypeStruct((b, l, dm), F32),
        compiler_params=_cparams("arbitrary", "arbitrary"),
        name="out_proj_even",
    )(*x, ya, ys, z, gate_mod, norm_g, wa, wb)


def _outproj_odd_kernel(x_ref, y_ref, gm_ref, w_ref, fg_ref, *o_refs, bp):
    acc = jnp.dot(y_ref[0].astype(BF16), w_ref[...], preferred_element_type=F32)
    xn = x_ref[0] + gm_ref[0] * acc
    if len(o_refs) == 1:
        o_refs[0][0] = xn
        return
    out = _rms(xn, fg_ref[...], 1e-6)

    @pl.when(pl.program_id(0) < bp)
    def _():
        o_refs[0][0] = out

    @pl.when(pl.program_id(0) >= bp)
    def _():
        o_refs[1][0] = out


def _out_projection_odd(x, y, gate_mod, w_out, final_g, split_at=None, tl=512):
    b, l, dm = x.shape
    w = y.shape[2]
    rows = lambda n: pl.BlockSpec((1, tl, n), lambda i, j: (i, j, 0))
    wb = w_out.astype(BF16)
    if split_at is None:
        out_specs = [rows(dm)]
        out_shape = [jax.ShapeDtypeStruct((b, l, dm), F32)]
    else:
        out_specs = [pl.BlockSpec((1, tl, dm), m) for m in _pair_row_maps(split_at, l // tl)]
        out_shape = [jax.ShapeDtypeStruct((n, l, dm), F32) for n in (split_at, b - split_at)]
    out = pl.pallas_call(
        functools.partial(_outproj_odd_kernel, bp=split_at),
        grid=(b, l // tl),
        in_specs=[rows(dm), rows(w),
                  pl.BlockSpec((1, 1, dm), lambda i, j: (i, 0, 0)),
                  _resident(wb.shape, lambda i, j: (0, 0)),
                  pl.BlockSpec((1, dm), lambda i, j: (0, 0))],
        out_specs=out_specs,
        out_shape=out_shape,
        compiler_params=_cparams("arbitrary", "arbitrary"),
        name="out_proj_odd",
    )(x, y, gate_mod, wb, final_g)
    return out[0] if split_at is None else tuple(out)


def kernel(x_prompt, x_sample, c_prompt, c_sample, mod_w, mod_b, norm_g, final_g, ev_w_in, ev_w_out, hy_conv_w, hy_conv_b, hy_fw1, hy_fb1, hy_fw2, hy_fb2, hy_fw3, hy_freq, hy_bias, ssd_conv_w, ssd_conv_b, ssd_dt_bias, ssd_A_log, ssd_D, ssd_norm_g, od_w_in, od_w_out, lru_conv_w, lru_conv_b, lru_w_a, lru_b_a, lru_w_x, lru_b_x, lru_lam):
    bp, l, dm = x_prompt.shape
    assert x_sample.shape[1:] == (l, dm), "the two request groups are stacked on the batch axis"
    x = (x_prompt, x_sample)
    c = jnp.concatenate([c_prompt, c_sample], axis=0)
    b = c.shape[0]
    depth = mod_w.shape[0]
    assert 2 * l == DFT_N1 * DFT_N2
    assert depth % 2 == 0, "the final norm is fused into the last (odd) layer's output projection"

    b_pad = -(-b // SUBLANES) * SUBLANES
    mods = _modulation(jnp.pad(c, ((0, b_pad - b), (0, 0))), mod_w, mod_b)[:, :b]
    shift, scale, gate_mod = (mods[:, :, k * dm:(k + 1) * dm].reshape(depth, b, 1, dm) for k in range(3))

    hyc = hy_bias.shape[1]
    ssw = ssd_norm_g.shape[1]
    xbw = ssd_conv_w.shape[2]
    o1, o2, o3, o4 = 3 * hyc, 4 * hyc, 4 * hyc + ssw, 4 * hyc + ssw + xbw

    for i in range(depth):
        j = i // 2
        g = norm_g[i][None, :]
        if i % 2 == 0:
            w_in = ev_w_in[j].astype(BF16)
            w_dt = jnp.pad(w_in[:, o4:], ((0, 0), (0, LANES - (w_in.shape[1] - o4))))
            hyu, hyg, z, xbc, dt_raw = _in_projection(
                x, shift[i], scale[i], g, [w_in[:, :o1], w_in[:, o1:o2], w_in[:, o2:o3], w_in[:, o3:o4], w_dt],
                [BF16, BF16, BF16, BF16, F32])
            hre, him = _hyena_spectrum(l, hyc, hy_fw1[j], hy_fb1[j], hy_fw2[j], hy_fb2[j], hy_fw3[j], hy_freq[j])
            ya = _hyena_branch(hyu, hyg, hy_conv_w[j], hy_conv_b[j], hre, him, hy_bias[j])
            ys = _ssd_branch(xbc, dt_raw, ssd_conv_w[j], ssd_conv_b[j], ssd_dt_bias[j], ssd_A_log[j], ssd_D[j])
            x = (_out_projection_even(x, ya, ys, z, gate_mod[i], ssd_norm_g[j][None, :], ev_w_out[j]),)
        else:
            w_in = od_w_in[j].astype(BF16)
            half = w_in.shape[1] // 2
            xb, gt = _in_projection(x, shift[i], scale[i], g, [w_in[:, :half], w_in[:, half:]], [BF16, BF16])
            y = _rglru_mixer(xb, gt, lru_conv_w[j], lru_conv_b[j], lru_w_a[j], lru_w_x[j],
                             lru_b_a[j], lru_b_x[j], lru_lam[j])
            if i == depth - 1:
                return _out_projection_odd(x[0], y, gate_mod[i], od_w_out[j], final_g[None, :], split_at=bp)
            x = (_out_projection_odd(x[0], y, gate_mod[i], od_w_out[j], final_g[None, :]),)
```

```python
import functools
import math

import numpy as np
import jax
import jax.numpy as jnp
from jax import lax
from jax.experimental import pallas as pl
from jax.experimental.pallas import tpu as pltpu

F32 = jnp.float32
BF16 = jnp.bfloat16
HI = lax.Precision.HIGHEST

V7X_VMEM_BYTES = 64 * 1024 * 1024
VMEM_LIMIT = V7X_VMEM_BYTES - 8 * 1024 * 1024
LANES = 128
SUBLANES = 8

HY_EMB = 33
HY_SHORT_DECAY_PCT = 0.3
HY_LONG_DECAY_PCT = 1.5
HY_DECAY_TARGET = 1e-2
SSD_HEADS = 16
SSD_HEAD_DIM = 64
SSD_GROUPS = 4
SSD_STATE = 128
SSD_CHUNK = 128
LRU_HEADS = 16
LRU_BLOCK = 128
LRU_C = 8.0

DFT_N1 = 64
DFT_N2 = 128
DFT_K1 = DFT_N1 // 2 + 1


def _cparams(*sem):
    return pltpu.CompilerParams(dimension_semantics=sem, vmem_limit_bytes=VMEM_LIMIT)


def _resident(shape, index_map):
    return pl.BlockSpec(shape, index_map, pipeline_mode=pl.Buffered(1))


def _sigmoid(x):
    return 0.5 + 0.5 * jnp.tanh(0.5 * x)


def _silu(x):
    return x * _sigmoid(x)


def _softplus(x):
    return jnp.maximum(x, 0.0) + jnp.log1p(jnp.exp(-jnp.abs(x)))


def _split3(x):
    hi = x.astype(BF16)
    r1 = x - hi.astype(F32)
    mid = r1.astype(BF16)
    lo = (r1 - mid.astype(F32)).astype(BF16)
    return hi, mid, lo


def _sel_dot(sel, x):
    hi, mid, lo = _split3(x)
    d = functools.partial(jnp.dot, preferred_element_type=F32)
    return d(sel, hi) + d(sel, mid) + d(sel, lo)


def _mod_kernel(c_ref, w_ref, b_ref, o_ref):
    cs = _silu(c_ref[...])
    o_ref[0] = jnp.dot(cs, w_ref[0], precision=HI, preferred_element_type=F32) + b_ref[0]


def _modulation(c_pad, mod_w, mod_b):
    depth, d, d3 = mod_w.shape
    bp = c_pad.shape[0]
    return pl.pallas_call(
        _mod_kernel,
        grid=(depth, d3 // d),
        in_specs=[
            pl.BlockSpec((bp, d), lambda i, j: (0, 0)),
            pl.BlockSpec((1, d, d), lambda i, j: (i, 0, j)),
            pl.BlockSpec((1, 1, d), lambda i, j: (i, 0, j)),
        ],
        out_specs=pl.BlockSpec((1, bp, d), lambda i, j: (i, 0, j)),
        out_shape=jax.ShapeDtypeStruct((depth, bp, d3), F32),
        compiler_params=_cparams("arbitrary", "arbitrary"),
        name="adaln_mod",
    )(c_pad, mod_w, mod_b.reshape(depth, 1, d3))


def _pair_row_maps(bp, nj):
    first = lambda i, j: (jnp.minimum(i, bp - 1), jnp.where(i < bp, j, nj - 1), 0)
    second = lambda i, j: (jnp.maximum(i - bp, 0), jnp.where(i < bp, 0, j), 0)
    return first, second


def _load_rows(x_refs, bp):
    if len(x_refs) == 1:
        return x_refs[0][0]
    return jnp.where(pl.program_id(0) < bp, x_refs[0][0], x_refs[1][0])


def _row_specs(xs, tl):
    l, d = xs[0].shape[1:]
    if len(xs) == 1:
        return [pl.BlockSpec((1, tl, d), lambda i, j: (i, j, 0))]
    return [pl.BlockSpec((1, tl, d), m) for m in _pair_row_maps(xs[0].shape[0], l // tl)]


def _inproj_kernel(*refs, n_x, bp, n_out):
    x_refs, (sh_ref, sc_ref, g_ref), refs = refs[:n_x], refs[n_x:n_x + 3], refs[n_x + 3:]
    w_refs, o_refs = refs[:n_out], refs[n_out:]
    hb = _norm_modulate(_load_rows(x_refs, bp), g_ref[...], sh_ref[0], sc_ref[0]).astype(BF16)
    for w_ref, o_ref in zip(w_refs, o_refs):
        o_ref[0] = jnp.dot(hb, w_ref[...], preferred_element_type=F32).astype(o_ref.dtype)


def _in_projection(x, shift, scale, g, weights, dtypes, tl=512):
    b = sum(a.shape[0] for a in x)
    l, d = x[0].shape[1:]
    n_out = len(weights)
    in_specs = _row_specs(x, tl) + [
        pl.BlockSpec((1, 1, d), lambda i, j: (i, 0, 0)),
        pl.BlockSpec((1, 1, d), lambda i, j: (i, 0, 0)),
        pl.BlockSpec((1, d), lambda i, j: (0, 0)),
    ] + [_resident(w.shape, lambda i, j: (0, 0)) for w in weights]
    out_specs = [pl.BlockSpec((1, tl, w.shape[1]), lambda i, j: (i, j, 0)) for w in weights]
    out_shape = [jax.ShapeDtypeStruct((b, l, w.shape[1]), dt) for w, dt in zip(weights, dtypes)]
    return pl.pallas_call(
        functools.partial(_inproj_kernel, n_x=len(x), bp=x[0].shape[0], n_out=n_out),
        grid=(b, l // tl),
        in_specs=in_specs,
        out_specs=out_specs,
        out_shape=out_shape,
        compiler_params=_cparams("arbitrary", "arbitrary"),
        name="in_proj",
    )(*x, shift, scale, g, *weights)


_CONV_ROWS = 256
_HALO = SUBLANES


def _conv_rows(pad_ref, r, w, bias, taps, left, halo=_HALO):
    acc = None
    for k in range(taps):
        term = w[k:k + 1, :] * pad_ref[pl.ds(r + halo + k - left, _CONV_ROWS), :]
        acc = term if acc is None else acc + term
    return acc + bias


_ROW_HALO = 16


def _hy_conv_kernel(*refs, tr):
    ins, (w_ref, b_ref, x0c_ref, vf_ref), pads = refs[:9], refs[9:13], refs[13:]
    j, last = pl.program_id(1), pl.num_programs(1) - 1
    c = x0c_ref.shape[2]
    conv = []
    for o in range(3):
        prev_ref, main_ref, next_ref = ins[3 * o:3 * o + 3]
        pad = pads[o]
        head = jnp.where(j > 0, prev_ref[0].astype(F32), 0.0)
        tail = jnp.where(j < last, next_ref[0].astype(F32), 0.0)
        for s in range(c // LANES):
            ls = slice(s * LANES, (s + 1) * LANES)
            pad[s, pl.ds(0, _ROW_HALO), :] = head[:, ls]
            pad[s, pl.ds(tr + _ROW_HALO, _ROW_HALO), :] = tail[:, ls]
            for r in range(0, tr, _CONV_ROWS):
                pad[s, pl.ds(_ROW_HALO + r, _CONV_ROWS), :] = main_ref[0, pl.ds(r, _CONV_ROWS), ls].astype(F32)
        conv.append((pad, o * c))

    for r in range(0, tr, _CONV_ROWS):
        for s in range(c // LANES):
            ls = slice(s * LANES, (s + 1) * LANES)
            x0c, x1c, vc = (_conv_rows(p.at[s], r, w_ref[:, off + s * LANES:off + (s + 1) * LANES],
                                       b_ref[:, off + s * LANES:off + (s + 1) * LANES], 3, 1, _ROW_HALO)
                            for p, off in conv)
            x0c_ref[0, pl.ds(r, _CONV_ROWS), ls] = x0c.astype(x0c_ref.dtype)
            vf_ref[0, pl.ds(r, _CONV_ROWS), ls] = vc * x1c


def _hyena_short_conv(hyu, conv_w, conv_b, tr=512):
    b, l, c3 = hyu.shape
    c = c3 // 3
    per_tile = tr // _ROW_HALO
    nh = l // _ROW_HALO
    in_specs = []
    for o in range(3):
        in_specs += [
            pl.BlockSpec((1, _ROW_HALO, c), lambda i, j, o=o: (i, jnp.maximum(j * per_tile - 1, 0), o)),
            pl.BlockSpec((1, tr, c), lambda i, j, o=o: (i, j, o)),
            pl.BlockSpec((1, _ROW_HALO, c), lambda i, j, o=o: (i, jnp.minimum((j + 1) * per_tile, nh - 1), o)),
        ]
    in_specs += [pl.BlockSpec((3, c3), lambda i, j: (0, 0)), pl.BlockSpec((1, c3), lambda i, j: (0, 0))]
    ospec = pl.BlockSpec((1, tr, c), lambda i, j: (i, j, 0))
    return pl.pallas_call(
        functools.partial(_hy_conv_kernel, tr=tr),
        grid=(b, l // tr),
        in_specs=in_specs,
        out_specs=[ospec, ospec],
        out_shape=[jax.ShapeDtypeStruct((b, l, c), BF16), jax.ShapeDtypeStruct((b, l, c), F32)],
        scratch_shapes=[pltpu.VMEM((c // LANES, tr + 2 * _ROW_HALO, LANES), F32)] * 3,
        compiler_params=_cparams("parallel", "parallel"),
        name="hyena_short_conv",
    )(*([hyu] * 9), conv_w, conv_b)


_SEG_BLOCK = SUBLANES * SUBLANES


def _fill_padded_slabs(pad_ref, x_ref, l):
    n = pad_ref.shape[0]
    zeros = jnp.zeros((_HALO, LANES), F32)
    for s in range(n):
        pad_ref[s, pl.ds(0, _HALO), :] = zeros
        pad_ref[s, pl.ds(l + _HALO, _HALO), :] = zeros

    def body(i, carry):
        r = pl.multiple_of(i * _CONV_ROWS, _CONV_ROWS)
        x = x_ref[pl.ds(r, _CONV_ROWS), :].astype(F32)
        for s in range(n):
            pad_ref[s, pl.ds(r + _HALO, _CONV_ROWS), :] = x[:, s * LANES:(s + 1) * LANES]
        return carry

    lax.fori_loop(0, l // _CONV_ROWS, body, 0)


def _conv_block_segments(pad_slab, base, w, bias, taps, left):
    tap = {m: pad_slab[pl.ds(base + _HALO + m, SUBLANES, stride=SUBLANES), :]
           for m in range(-left, SUBLANES + taps - 1 - left)}
    pieces = []
    for j in range(SUBLANES):
        acc = bias
        for k in range(taps):
            acc = acc + w[k] * tap[j + k - left]
        pieces.append(acc)
    return pieces


def _conv_silu_kernel(x_ref, w_ref, b_ref, o_ref, p_ref, *, l, taps, left):
    _fill_padded_slabs(p_ref, x_ref.at[0], l)
    n = p_ref.shape[0]
    w, bias = w_ref[...], b_ref[...]

    for r in range(0, l, _CONV_ROWS):
        parts = [_conv_rows(p_ref.at[s], r, w[:, s * LANES:(s + 1) * LANES], bias[:, s * LANES:(s + 1) * LANES],
                            taps, left) for s in range(n)]
        o_ref[0, pl.ds(r, _CONV_ROWS), :] = _silu(jnp.concatenate(parts, axis=1)).astype(o_ref.dtype)


def _conv_silu(x, conv_w, conv_b, tc=256):
    b, l, c = x.shape
    taps = conv_w.shape[0]
    return pl.pallas_call(
        functools.partial(_conv_silu_kernel, l=l, taps=taps, left=taps // 2),
        grid=(b, c // tc),
        in_specs=[pl.BlockSpec((1, l, tc), lambda i, j: (i, 0, j)),
                  pl.BlockSpec((taps, tc), lambda i, j: (0, j)),
                  pl.BlockSpec((1, tc), lambda i, j: (0, j))],
        out_specs=pl.BlockSpec((1, l, tc), lambda i, j: (i, 0, j)),
        out_shape=jax.ShapeDtypeStruct((b, l, c), BF16),
        scratch_shapes=[pltpu.VMEM((tc // LANES, l + 2 * _HALO, LANES), F32)],
        compiler_params=_cparams("parallel", "parallel"),
        name="ssd_conv_silu",
    )(x, conv_w, conv_b)


def _filter_kernel(z_ref, w1_ref, b1_ref, fr_ref, w2_ref, b2_ref, w3_ref, dl_ref,
                   hfg_ref, nrm_ref, *, tr, c):
    i = pl.program_id(0)
    z = z_ref[...]
    fr = fr_ref[...]
    d = functools.partial(jnp.dot, precision=HI, preferred_element_type=F32)
    h = jnp.sin(fr * (d(z, w1_ref[...]) + b1_ref[...]))
    h = jnp.sin(fr * (d(h, w2_ref[...]) + b2_ref[...]))
    h = jnp.dot(h.astype(BF16), w3_ref[...].astype(BF16), preferred_element_type=F32)
    window = jnp.exp(-z[:, 0:1] * dl_ref[...])
    hf = h[:, :c] * window
    row = lax.broadcasted_iota(jnp.int32, (tr, 1), 0) + i * tr
    g = jnp.where(row == 0, 0.0, h[:, c:] * window)
    hfg_ref[0] = hf
    hfg_ref[1] = g
    part = jnp.sum(jnp.abs(hf) + jnp.abs(g), axis=0, keepdims=True)

    @pl.when(i == 0)
    def _():
        nrm_ref[...] = part

    @pl.when(i != 0)
    def _():
        nrm_ref[...] = nrm_ref[...] + part


def _hyena_filter_taps(zfeat, w1, b1, freq, w2, b2, w3, deltas, tr=512):
    l, zp = zfeat.shape
    hp = w1.shape[1]
    c2 = w3.shape[1]
    c = c2 // 2
    full = lambda shape: pl.BlockSpec(shape, lambda i: (0,) * len(shape))
    return pl.pallas_call(
        functools.partial(_filter_kernel, tr=tr, c=c),
        grid=(l // tr,),
        in_specs=[pl.BlockSpec((tr, zp), lambda i: (i, 0)), full((zp, hp)), full((1, hp)),
                  full((1, hp)), full((hp, hp)), full((1, hp)), full((hp, c2)), full((1, c))],
        out_specs=[pl.BlockSpec((2, tr, c), lambda i: (0, i, 0)), full((1, c))],
        out_shape=[jax.ShapeDtypeStruct((2, l, c), F32), jax.ShapeDtypeStruct((1, c), F32)],
        compiler_params=_cparams("arbitrary"),
        name="hyena_filter_mlp",
    )(zfeat, w1, b1, freq, w2, b2, w3, deltas)


def _filter_spec_kernel(h_ref, f1_ref, g_ref, nrm_ref, hre_ref, him_ref, s_ref):
    _, nh, ng, sub, tc = h_ref.shape
    n2 = ng * sub
    nreal = s_ref.shape[1]
    k1 = nreal // 2 + 1
    d = functools.partial(jnp.dot, preferred_element_type=F32)
    f1 = f1_ref[...]
    for q in range(2):
        for g in range(ng):
            x = h_ref[q, :, g].reshape(nh * sub, tc).astype(BF16)
            s_ref[q, :, g] = d(f1, x).reshape(nreal, sub, tc)
    inv = 1.0 / nrm_ref[...]
    zero = jnp.zeros((n2, tc), F32)

    def spectrum(k, has_im):
        xs = []
        for q in range(2):
            re = s_ref[q, k].reshape(n2, tc)
            im = s_ref[q, k1 - 1 + k].reshape(n2, tc) if has_im else zero
            xs.append(d(g_ref[k], jnp.concatenate([re, im], axis=0).astype(BF16)))
        xf, xg = xs
        hre_ref[k] = (xf[:n2] + xg[:n2]) * inv
        him_ref[k] = (xf[n2:] - xg[n2:]) * inv

    spectrum(0, False)
    spectrum(k1 - 1, False)

    def body(k, carry):
        spectrum(k, True)
        return carry

    lax.fori_loop(1, k1 - 1, body, 0, unroll=_LEVEL2_UNROLL)


def _filter_spectrum(hfg, f1k, g, nrm, tc=128):
    _, l, c = hfg.shape
    nh, sub = DFT_N1 // 2, SUBLANES
    ng = DFT_N2 // sub
    k1, n2 = g.shape[0], DFT_N2
    return pl.pallas_call(
        _filter_spec_kernel,
        grid=(c // tc,),
        in_specs=[pl.BlockSpec((2, nh, ng, sub, tc), lambda j: (0, 0, 0, 0, j)),
                  _resident(f1k.shape, lambda j: (0, 0)), _resident(g.shape, lambda j: (0, 0, 0)),
                  pl.BlockSpec((1, tc), lambda j: (0, j))],
        out_specs=[pl.BlockSpec((k1, n2, tc), lambda j: (0, 0, j))] * 2,
        out_shape=[jax.ShapeDtypeStruct((k1, n2, c), F32)] * 2,
        scratch_shapes=[pltpu.VMEM((2, DFT_N1, ng, sub, tc), F32)],
        compiler_params=_cparams("arbitrary"),
        name="hyena_filter_spectrum",
    )(hfg.reshape(2, nh, ng, sub, c), f1k, g, nrm)


_LEVEL2_UNROLL = 16


def _fftconv_kernel(vf_ref, x0_ref, gt_ref, f1_ref, a1_ref, g_ref, gi_ref, hre_ref, him_ref, bias_ref,
                    o_ref, s_ref, y_ref):
    nh, ng, sub, tc = vf_ref.shape[1:]
    n2 = ng * sub
    nreal = s_ref.shape[0]
    k1 = nreal // 2 + 1
    d = functools.partial(jnp.dot, preferred_element_type=F32)

    f1 = f1_ref[...]
    for g in range(ng):
        x = vf_ref[0, :, g].reshape(nh * sub, tc).astype(BF16)
        s_ref[:, g] = d(f1, x).reshape(nreal, sub, tc)

    def level2(k, re, im):
        x = d(g_ref[k], jnp.concatenate([re, im], axis=0).astype(BF16))
        xr, xi = x[:n2], x[n2:]
        hr, hi = hre_ref[k], him_ref[k]
        y = jnp.concatenate([xr * hr - xi * hi, xr * hi + xi * hr], axis=0).astype(BF16)
        return d(gi_ref[k], y)

    zero = jnp.zeros((n2, tc), F32)
    for k in (0, k1 - 1):
        z = level2(k, s_ref[k].reshape(n2, tc), zero)
        s_ref[k] = z[:n2].reshape(ng, sub, tc)

    def body(k, carry):
        z = level2(k, s_ref[k].reshape(n2, tc), s_ref[k1 - 1 + k].reshape(n2, tc))
        s_ref[k] = z[:n2].reshape(ng, sub, tc)
        s_ref[k1 - 1 + k] = z[n2:].reshape(ng, sub, tc)
        return carry

    lax.fori_loop(1, k1 - 1, body, 0, unroll=_LEVEL2_UNROLL)

    a1 = a1_ref[...]
    for g in range(ng):
        z = s_ref[:, g].reshape(nreal * sub, tc).astype(BF16)
        y_ref[:, g] = d(a1, z).reshape(nh, sub, tc)

    bias = bias_ref[...]
    step = 2
    rows = step * n2

    def epilogue(i, carry):
        r = pl.multiple_of(i * rows, rows)
        y = y_ref[pl.ds(i * step, step)].reshape(rows, tc)
        vf = vf_ref[0, pl.ds(i * step, step)].reshape(rows, tc)
        x0 = x0_ref[0, pl.ds(r, rows), :].astype(F32)
        gt = gt_ref[0, pl.ds(r, rows), :].astype(F32)
        o_ref[0, pl.ds(r, rows), :] = ((x0 * (y + vf * bias)) * _silu(gt)).astype(o_ref.dtype)
        return carry

    lax.fori_loop(0, nh // step, epilogue, 0)


def _hyena_fftconv(vf, x0c, gate, f1k, a1k, g, ginv, hre, him, bias, tc=256):
    b, l, c = vf.shape
    nh, sub = DFT_N1 // 2, SUBLANES
    ng = DFT_N2 // sub
    kb = g.shape[0]
    n2 = DFT_N2
    rows = pl.BlockSpec((1, l, tc), lambda j, i: (i, 0, j))
    return pl.pallas_call(
        _fftconv_kernel,
        grid=(c // tc, b),
        in_specs=[pl.BlockSpec((1, nh, ng, sub, tc), lambda j, i: (i, 0, 0, 0, j)), rows, rows,
                  _resident(f1k.shape, lambda j, i: (0, 0)), _resident(a1k.shape, lambda j, i: (0, 0)),
                  _resident(g.shape, lambda j, i: (0, 0, 0)), _resident(ginv.shape, lambda j, i: (0, 0, 0)),
                  _resident((kb, n2, tc), lambda j, i: (0, 0, j)), _resident((kb, n2, tc), lambda j, i: (0, 0, j)),
                  pl.BlockSpec((1, tc), lambda j, i: (0, j))],
        out_specs=rows,
        out_shape=jax.ShapeDtypeStruct((b, l, c), BF16),
        scratch_shapes=[pltpu.VMEM((DFT_N1, ng, sub, tc), F32), pltpu.VMEM((nh, ng, sub, tc), F32)],
        compiler_params=_cparams("arbitrary", "arbitrary"),
        name="hyena_fftconv",
    )(vf.reshape(b, nh, ng, sub, c), x0c, gate, f1k, a1k, g, ginv, hre, him, bias)


@functools.lru_cache(maxsize=None)
def _dft_tables():
    n1, n2, k1 = DFT_N1, DFT_N2, DFT_K1
    n = n1 * n2
    nh = n1 // 2
    ang = 2.0 * np.pi * np.arange(k1)[:, None] * np.arange(nh)[None, :] / n1
    fre, fim = np.cos(ang), -np.sin(ang)
    m = np.arange(n2)[None, None, :]
    k2 = np.arange(n2)[None, :, None]
    ka = np.arange(k1)[:, None, None]
    ph = -2.0 * np.pi * m * (ka / n + k2 / n2)
    gr, gi = np.cos(ph), np.sin(ph)
    g = np.concatenate([np.concatenate([gr, -gi], axis=2), np.concatenate([gi, gr], axis=2)], axis=1)
    grt, git = np.swapaxes(gr, 1, 2), np.swapaxes(gi, 1, 2)
    ginv = np.concatenate([np.concatenate([grt, git], axis=2), np.concatenate([-git, grt], axis=2)], axis=1)
    wk = np.where((np.arange(k1) == 0) | (np.arange(k1) == n1 // 2), 1.0, 2.0)[None, :]
    are, aim = wk * fre.T / n, wk * fim.T / n
    eye = np.eye(SUBLANES)
    f1k = np.kron(np.concatenate([fre, fim[1:k1 - 1]], axis=0), eye)
    a1k = np.kron(np.concatenate([are, aim[:, 1:k1 - 1]], axis=1), eye)
    return tuple(np.asarray(a, np.float32).astype(BF16) for a in (g, ginv, f1k, a1k))


def _position_features(l):
    f32 = F32
    pos = jnp.arange(l, dtype=f32)[:, None]
    t = pos / max(l - 1, 1)
    bands = (HY_EMB - 1) // 2
    f = jnp.linspace(1e-4, bands - 1, bands, dtype=f32)[None, :]
    ang = f * pos * (2.0 * math.pi / l)
    z = jnp.concatenate([t, jnp.cos(ang), -jnp.sin(ang)], axis=-1)
    return jnp.pad(z, ((0, 0), (0, LANES - HY_EMB)))


def _pad2(a, rows, cols):
    return jnp.pad(a, ((0, rows - a.shape[0]), (0, cols - a.shape[1])))


def _hyena_spectrum(l, c, fw1, fb1, fw2, fb2, fw3, freq):
    g, _, f1k, _ = _dft_tables()
    hp = LANES
    zfeat = _position_features(l)
    w1 = _pad2(fw1, LANES, hp)
    w2 = _pad2(fw2, hp, hp)
    w3 = _pad2(fw3, hp, fw3.shape[1])
    row = lambda v: _pad2(v[None, :], 1, hp)
    deltas = jnp.abs(jnp.linspace(math.log(HY_DECAY_TARGET) / HY_LONG_DECAY_PCT,
                                  math.log(HY_DECAY_TARGET) / HY_SHORT_DECAY_PCT, c, dtype=F32))[None, :]
    hfg, nrm = _hyena_filter_taps(zfeat, w1, row(fb1), row(freq), w2, row(fb2), w3, deltas)
    return _filter_spectrum(hfg, jnp.asarray(f1k), jnp.asarray(g), nrm)


def _hyena_branch(hyu, hy_gate, conv_w, conv_b, hre, him, bias):
    b, l, c3 = hyu.shape
    c = c3 // 3
    g, ginv, f1k, a1k = (jnp.asarray(t) for t in _dft_tables())
    x0c, vf = _hyena_short_conv(hyu, conv_w, conv_b[None, :])
    return _hyena_fftconv(vf, x0c, hy_gate, f1k, a1k, g, ginv, hre, him, bias[None, :])


def _ssd_kernel(xs_ref, bm_ref, cm_ref, dt_ref, dtb_ref, alog_ref, rep_ref, dsk_ref, mask_ref,
                *rest, reverse, lane0, chunks):
    yo_ref = rest[0] if len(rest) == 3 else None
    y_ref, prev_ref = rest[-2:]
    q, n, hd = SSD_CHUNK, SSD_STATE, SSD_HEAD_DIM
    gw = (SSD_HEADS // SSD_GROUPS) * hd

    @pl.when(pl.program_id(1) == 0)
    def _():
        prev_ref[...] = jnp.zeros_like(prev_ref)

    mask = mask_ref[...]
    maskb = mask > 0
    neg_a = -jnp.exp(alog_ref[...])
    lo_lanes = lax.broadcasted_iota(jnp.int32, (1, LANES), 1) < hd
    d = functools.partial(jnp.dot, preferred_element_type=F32)
    npair = gw // LANES
    edge = 0 if reverse else q - 1

    def per_pair(v, c0, shape):
        b0 = jnp.broadcast_to(v[:, c0:c0 + 1], shape)
        b1 = jnp.broadcast_to(v[:, c0 + 1:c0 + 2], shape)
        return jnp.where(lo_lanes, b0, b1)

    order = range(chunks - 1, -1, -1) if reverse else range(chunks)
    for k in order:
        rows = pl.ds(k * q, q)
        xs_b = xs_ref[0, rows, :]
        bm = bm_ref[0, rows, :]
        cm = cm_ref[0, rows, :]
        dt = _softplus(dt_ref[0, rows, :] + dtb_ref[...])
        cum = _sel_dot(mask, dt * neg_a)
        src_t = (cum - jnp.log(dt)).T
        cum_e = cum[edge:edge + 1, :]
        e_in = jnp.exp(cum)
        w_out = dt * jnp.exp(cum_e - cum)
        e_all = jnp.exp(cum_e)
        spread = d(jnp.concatenate([e_in, w_out], axis=0).astype(BF16), rep_ref[...])
        e_x = spread[:q]
        xsd_all = xs_b * spread[q:].astype(BF16)
        ys = []
        for g in range(SSD_GROUPS):
            cg = cm[:, g * n:(g + 1) * n].astype(BF16)
            bg = bm[:, g * n:(g + 1) * n].astype(BF16)
            cb = lax.dot_general(cg, bg, (((1,), (1,)), ((), ())), preferred_element_type=F32)
            sl = slice(g * gw, (g + 1) * gw)
            prev_g = prev_ref[:, sl]
            y_off = d(cg, prev_g.astype(BF16))
            cds = []
            for pair in range(npair):
                lo = g * gw + pair * LANES
                c0 = lane0 + lo // hd
                ms = []
                for c in (c0, c0 + 1):
                    seg_dt = jnp.exp(jnp.where(maskb, cum[:, c:c + 1] - src_t[c:c + 1, :], -jnp.inf))
                    ms.append((cb * seg_dt).astype(BF16))
                xp = xs_b[:, lo:lo + LANES]
                zero = jnp.zeros_like(xp)
                rhs = jnp.concatenate([jnp.where(lo_lanes, xp, zero), jnp.where(lo_lanes, zero, xp)], axis=0)
                yd = d(jnp.concatenate(ms, axis=1), rhs)
                ys.append(yd + y_off[:, pair * LANES:(pair + 1) * LANES] * e_x[:, lo:lo + LANES])
                cds.append(per_pair(e_all, c0, (n, LANES)))
            st = lax.dot_general(bg, xsd_all[:, sl], (((0,), (0,)), ((), ())), preferred_element_type=F32)
            prev_ref[:, sl] = prev_g * jnp.concatenate(cds, axis=1) + st
        y = jnp.concatenate(ys, axis=1)
        if not reverse:
            y = y + xs_b.astype(F32) * dsk_ref[...]
        if yo_ref is not None:
            y = y + yo_ref[0, rows, :].astype(F32)
        y_ref[0, rows, :] = y.astype(y_ref.dtype)


def _ssd_direction(xbc, dt_raw, dtb, alog, rep, dskip, mask, y_other=None, *, reverse, ts=1024):
    b, l, _ = xbc.shape
    hp = SSD_HEADS * SSD_HEAD_DIM
    gn = SSD_GROUPS * SSD_STATE
    nt = l // ts
    tmap = (lambda j: nt - 1 - j) if reverse else (lambda j: j)
    full = lambda shape: pl.BlockSpec(shape, lambda i, j: (0,) * len(shape))
    yspec = pl.BlockSpec((1, ts, hp), lambda i, j: (i, tmap(j), 0))
    extra = [] if y_other is None else [y_other]
    return pl.pallas_call(
        functools.partial(_ssd_kernel, reverse=reverse, lane0=SSD_HEADS if reverse else 0,
                          chunks=ts // SSD_CHUNK),
        grid=(b, nt),
        in_specs=[yspec,
                  pl.BlockSpec((1, ts, gn), lambda i, j: (i, tmap(j), hp // gn)),
                  pl.BlockSpec((1, ts, gn), lambda i, j: (i, tmap(j), hp // gn + 1)),
                  pl.BlockSpec((1, ts, LANES), lambda i, j: (i, tmap(j), 0)),
                  full((1, LANES)), full((1, LANES)), full((LANES, hp)), full((1, hp)),
                  full((SSD_CHUNK, SSD_CHUNK))] + [yspec] * len(extra),
        out_specs=yspec,
        out_shape=jax.ShapeDtypeStruct((b, l, hp), BF16),
        scratch_shapes=[pltpu.VMEM((SSD_STATE, hp), F32)],
        compiler_params=_cparams("parallel", "arbitrary"),
        name="ssd_bwd" if reverse else "ssd_fwd",
    )(xbc, xbc, xbc, dt_raw, dtb, alog, rep, dskip, mask, *extra)


def _ssd_branch(xbc, dt_raw, conv_w, conv_b, dt_bias, a_log, d_skip):
    xc = _conv_silu(xbc, conv_w, conv_b[None, :])
    pad_row = lambda v: _pad2(v.reshape(1, -1), 1, LANES)
    dtb, alog = pad_row(dt_bias), pad_row(a_log)
    dskip = jnp.repeat(d_skip, SSD_HEAD_DIM)[None, :]
    lane_head = np.arange(SSD_HEADS * SSD_HEAD_DIM)[None, :] // SSD_HEAD_DIM
    src = np.arange(LANES)[:, None]
    idx = np.arange(SSD_CHUNK)
    y = None
    for reverse in (False, True):
        rep = jnp.asarray(src == lane_head + (SSD_HEADS if reverse else 0), dtype=BF16)
        allowed = (idx[None, :] >= idx[:, None]) if reverse else (idx[None, :] <= idx[:, None])
        mask = jnp.asarray(allowed, dtype=BF16)
        y = _ssd_direction(xc, dt_raw, dtb, alog, rep, dskip, mask, y, reverse=reverse)
    return y


_LRU_ROWS = 256


def _lru_kernel(x_ref, gt_ref, cw_ref, cb_ref, wa_ref, wx_ref, ba_ref, bx_ref, lam_ref,
                o_ref, pad_ref, *seq_refs, l, heads):
    blk = _SEG_BLOCK
    _fill_padded_slabs(pad_ref, x_ref.at[0], l)
    d = functools.partial(jnp.dot, preferred_element_type=F32)
    cw, cb = cw_ref[...], cb_ref[...]
    taps = cw.shape[0]

    def conv_segment_order(hd, r):
        ls = slice(hd * LRU_BLOCK, (hd + 1) * LRU_BLOCK)
        w = [cw[k:k + 1, ls] for k in range(taps)]
        pieces = []
        for base in range(0, _CONV_ROWS, blk):
            pieces += _conv_block_segments(pad_ref.at[hd], r + base, w, cb[:, ls], taps, taps // 2)
        return jnp.concatenate(pieces, axis=0)
    e0 = (-0.5 * LRU_C / math.log(2.0)) * _softplus(-lam_ref[...])
    ba, bx = ba_ref[...], bx_ref[...]
    a_refs, u_refs, h_refs = ([[seq_refs[(q * 2 + dr) * heads + hd] for hd in range(heads)] for dr in range(2)]
                              for q in range(3))

    def gates(i, carry):
        r = pl.multiple_of(i * _CONV_ROWS, _CONV_ROWS)
        for hd in range(heads):
            ls = slice(hd * LRU_BLOCK, (hd + 1) * LRU_BLOCK)
            xh = conv_segment_order(hd, r)
            xb = xh.astype(BF16)
            xh2 = 0.5 * xh
            for dr in range(2):
                tr = jnp.tanh(d(xb, wa_ref[dr, hd]) + ba[dr:dr + 1, ls])
                ti = jnp.tanh(d(xb, wx_ref[dr, hd]) + bx[dr:dr + 1, ls])
                e = e0[dr:dr + 1, ls]
                a = jnp.exp2(e + e * tr)
                s = 1.0 - a * a
                root = jnp.where(s > 0.0, s * lax.rsqrt(s), 0.0)
                u = (xh2 + xh2 * ti) * root
                a_refs[dr][hd][pl.ds(r, _CONV_ROWS), :] = a
                u_refs[dr][hd][pl.ds(r, _CONV_ROWS), :] = u
        return carry

    lax.fori_loop(0, l // _CONV_ROWS, gates, 0)

    row = lax.broadcasted_iota(jnp.int32, (SUBLANES, LRU_BLOCK), 0)
    nblk = l // blk

    def sublane_scan(a, u, reverse):
        for s in (1, 2, 4):
            sh = (SUBLANES - s) if reverse else s
            valid = (row < SUBLANES - s) if reverse else (row >= s)
            a_sh = jnp.where(valid, pltpu.roll(a, sh, axis=0), 1.0)
            u_sh = jnp.where(valid, pltpu.roll(u, sh, axis=0), 0.0)
            u = u + a * u_sh
            a = a * a_sh
        return a, u

    def block_scan(a_ref, u_ref, h_ref, r, h_in, reverse):
        order = range(SUBLANES - 1, -1, -1) if reverse else range(SUBLANES)
        hs, ps = {}, {}
        h = p = None
        for k in order:
            rows = pl.ds(r + SUBLANES * k, SUBLANES)
            a, u = a_ref[rows, :], u_ref[rows, :]
            h = u if h is None else a * h + u
            p = a if p is None else a * p
            hs[k], ps[k] = h, p
        sp, se = sublane_scan(p, h, reverse)
        full = se + sp * h_in
        edge = 0 if reverse else SUBLANES - 1
        h_out = jnp.broadcast_to(full[edge:edge + 1, :], (SUBLANES, LRU_BLOCK))
        if reverse:
            carry = jnp.where(row == SUBLANES - 1, h_in, pltpu.roll(full, SUBLANES - 1, axis=0))
        else:
            carry = jnp.where(row == 0, h_in, pltpu.roll(full, 1, axis=0))
        for k in order:
            h_ref[pl.ds(r + k, SUBLANES, stride=SUBLANES), :] = hs[k] + ps[k] * carry
        return h_out

    def scan(i, carry):
        rf = pl.multiple_of(i * blk, blk)
        rb = pl.multiple_of((nblk - 1 - i) * blk, blk)
        out = []
        for hd in range(heads):
            hf, hb = carry[2 * hd], carry[2 * hd + 1]
            out.append(block_scan(a_refs[0][hd], u_refs[0][hd], h_refs[0][hd], rf, hf, False))
            out.append(block_scan(a_refs[1][hd], u_refs[1][hd], h_refs[1][hd], rb, hb, True))
        return tuple(out)

    zero = jnp.zeros((SUBLANES, LRU_BLOCK), F32)
    lax.fori_loop(0, nblk, scan, (zero,) * (2 * heads), unroll=2)

    def combine(i, carry):
        r = pl.multiple_of(i * _LRU_ROWS, _LRU_ROWS)
        rows = pl.ds(r, _LRU_ROWS)
        h = jnp.concatenate([h_refs[0][hd][rows, :] + h_refs[1][hd][rows, :] for hd in range(heads)], axis=1)
        o_ref[0, rows, :] = (h * _silu(gt_ref[0, rows, :].astype(F32))).astype(o_ref.dtype)
        return carry

    lax.fori_loop(0, l // _LRU_ROWS, combine, 0)


def _rglru_mixer(xb, gate, conv_w, conv_b, w_a, w_x, b_a, b_x, lam, heads=2):
    b, l, w = xb.shape
    tw = heads * LRU_BLOCK
    xspec = pl.BlockSpec((1, l, tw), lambda i, j: (i, 0, j))
    wspec = pl.BlockSpec((2, heads, LRU_BLOCK, LRU_BLOCK), lambda i, j: (0, j, 0, 0))
    vspec = pl.BlockSpec((2, tw), lambda i, j: (0, j))
    seqs = [pltpu.VMEM((l, LRU_BLOCK), F32)] * (3 * 2 * heads)
    return pl.pallas_call(
        functools.partial(_lru_kernel, l=l, heads=heads),
        grid=(b, w // tw),
        in_specs=[xspec, xspec,
                  pl.BlockSpec((4, tw), lambda i, j: (0, j)),
                  pl.BlockSpec((1, tw), lambda i, j: (0, j)),
                  wspec, wspec, vspec, vspec, vspec],
        out_specs=xspec,
        out_shape=jax.ShapeDtypeStruct((b, l, w), BF16),
        scratch_shapes=[pltpu.VMEM((heads, l + 2 * _HALO, LRU_BLOCK), F32)] + seqs,
        compiler_params=_cparams("parallel", "parallel"),
        name="rglru",
    )(xb, gate, conv_w, conv_b[None, :], (0.5 * w_a).astype(BF16), (0.5 * w_x).astype(BF16),
      0.5 * b_a, 0.5 * b_x, lam)


def _rms(x, g, eps):
    return (x * lax.rsqrt(jnp.mean(x * x, axis=-1, keepdims=True) + eps)) * g


def _norm_modulate(x, g, shift, scale):
    ms = jnp.mean(x * x, axis=-1, keepdims=True)
    return ((x * lax.rsqrt(ms + 1e-6)) * g) * (1.0 + scale) + shift


def _outproj_even_kernel(*refs, n_x, bp):
    x_refs, (ya_ref, ys_ref, z_ref, gm_ref, ng_ref, wa_ref, wb_ref, o_ref) = refs[:n_x], refs[n_x:]
    yb = ys_ref[0].astype(F32) * _silu(z_ref[0].astype(F32))
    ng = ng_ref[...]
    gw = yb.shape[1] // SSD_GROUPS
    parts = [_rms(yb[:, g * gw:(g + 1) * gw], ng[:, g * gw:(g + 1) * gw], 1e-5) for g in range(SSD_GROUPS)]
    ybn = jnp.concatenate(parts, axis=1)
    d = functools.partial(jnp.dot, preferred_element_type=F32)
    acc = d(ya_ref[0].astype(BF16), wa_ref[...]) + d(ybn.astype(BF16), wb_ref[...])
    o_ref[0] = _load_rows(x_refs, bp) + gm_ref[0] * acc


def _out_projection_even(x, ya, ys, z, gate_mod, norm_g, w_out, tl=512):
    b, l, c = ya.shape
    dm = x[0].shape[2]
    rows = lambda n: pl.BlockSpec((1, tl, n), lambda i, j: (i, j, 0))
    wa, wb = w_out[:c].astype(BF16), w_out[c:].astype(BF16)
    return pl.pallas_call(
        functools.partial(_outproj_even_kernel, n_x=len(x), bp=x[0].shape[0]),
        grid=(b, l // tl),
        in_specs=_row_specs(x, tl) + [rows(c), rows(c), rows(c),
                                      pl.BlockSpec((1, 1, dm), lambda i, j: (i, 0, 0)),
                                      pl.BlockSpec((1, c), lambda i, j: (0, 0)),
                                      _resident(wa.shape, lambda i, j: (0, 0)),
                                      _resident(wb.shape, lambda i, j: (0, 0))],
        out_specs=rows(dm),
        out_shape=jax.ShapeDtypeStruct((b, l, dm), F32),
        compiler_params=_cparams("arbitrary", "arbitrary"),
        name="out_proj_even",
    )(*x, ya, ys, z, gate_mod, norm_g, wa, wb)


def _outproj_odd_kernel(x_ref, y_ref, gm_ref, w_ref, fg_ref, *o_refs, bp):
    acc = jnp.dot(y_ref[0].astype(BF16), w_ref[...], preferred_element_type=F32)
    xn = x_ref[0] + gm_ref[0] * acc
    if len(o_refs) == 1:
        o_refs[0][0] = xn
        return
    out = _rms(xn, fg_ref[...], 1e-6)

    @pl.when(pl.program_id(0) < bp)
    def _():
        o_refs[0][0] = out

    @pl.when(pl.program_id(0) >= bp)
    def _():
        o_refs[1][0] = out


def _out_projection_odd(x, y, gate_mod, w_out, final_g, split_at=None, tl=512):
    b, l, dm = x.shape
    w = y.shape[2]
    rows = lambda n: pl.BlockSpec((1, tl, n), lambda i, j: (i, j, 0))
    wb = w_out.astype(BF16)
    if split_at is None:
        out_specs = [rows(dm)]
        out_shape = [jax.ShapeDtypeStruct((b, l, dm), F32)]
    else:
        out_specs = [pl.BlockSpec((1, tl, dm), m) for m in _pair_row_maps(split_at, l // tl)]
        out_shape = [jax.ShapeDtypeStruct((n, l, dm), F32) for n in (split_at, b - split_at)]
    out = pl.pallas_call(
        functools.partial(_outproj_odd_kernel, bp=split_at),
        grid=(b, l // tl),
        in_specs=[rows(dm), rows(w),
                  pl.BlockSpec((1, 1, dm), lambda i, j: (i, 0, 0)),
                  _resident(wb.shape, lambda i, j: (0, 0)),
                  pl.BlockSpec((1, dm), lambda i, j: (0, 0))],
        out_specs=out_specs,
        out_shape=out_shape,
        compiler_params=_cparams("arbitrary", "arbitrary"),
        name="out_proj_odd",
    )(x, y, gate_mod, wb, final_g)
    return out[0] if split_at is None else tuple(out)


def kernel(x_prompt, x_sample, c_prompt, c_sample, mod_w, mod_b, norm_g, final_g, ev_w_in, ev_w_out, hy_conv_w, hy_conv_b, hy_fw1, hy_fb1, hy_fw2, hy_fb2, hy_fw3, hy_freq, hy_bias, ssd_conv_w, ssd_conv_b, ssd_dt_bias, ssd_A_log, ssd_D, ssd_norm_g, od_w_in, od_w_out, lru_conv_w, lru_conv_b, lru_w_a, lru_b_a, lru_w_x, lru_b_x, lru_lam):
    bp, l, dm = x_prompt.shape
    assert x_sample.shape[1:] == (l, dm), "the two request groups are stacked on the batch axis"
    x = (x_prompt, x_sample)
    c = jnp.concatenate([c_prompt, c_sample], axis=0)
    b = c.shape[0]
    depth = mod_w.shape[0]
    assert 2 * l == DFT_N1 * DFT_N2
    assert depth % 2 == 0, "the final norm is fused into the last (odd) layer's output projection"

    b_pad = -(-b // SUBLANES) * SUBLANES
    mods = _modulation(jnp.pad(c, ((0, b_pad - b), (0, 0))), mod_w, mod_b)[:, :b]
    shift, scale, gate_mod = (mods[:, :, k * dm:(k + 1) * dm].reshape(depth, b, 1, dm) for k in range(3))

    hyc = hy_bias.shape[1]
    ssw = ssd_norm_g.shape[1]
    xbw = ssd_conv_w.shape[2]
    o1, o2, o3, o4 = 3 * hyc, 4 * hyc, 4 * hyc + ssw, 4 * hyc + ssw + xbw

    for i in range(depth):
        j = i // 2
        g = norm_g[i][None, :]
        if i % 2 == 0:
            w_in = ev_w_in[j].astype(BF16)
            w_dt = jnp.pad(w_in[:, o4:], ((0, 0), (0, LANES - (w_in.shape[1] - o4))))
            hyu, hyg, z, xbc, dt_raw = _in_projection(
                x, shift[i], scale[i], g, [w_in[:, :o1], w_in[:, o1:o2], w_in[:, o2:o3], w_in[:, o3:o4], w_dt],
                [BF16, BF16, BF16, BF16, F32])
            hre, him = _hyena_spectrum(l, hyc, hy_fw1[j], hy_fb1[j], hy_fw2[j], hy_fb2[j], hy_fw3[j], hy_freq[j])
            ya = _hyena_branch(hyu, hyg, hy_conv_w[j], hy_conv_b[j], hre, him, hy_bias[j])
            ys = _ssd_branch(xbc, dt_raw, ssd_conv_w[j], ssd_conv_b[j], ssd_dt_bias[j], ssd_A_log[j], ssd_D[j])
            x = (_out_projection_even(x, ya, ys, z, gate_mod[i], ssd_norm_g[j][None, :], ev_w_out[j]),)
        else:
            w_in = od_w_in[j].astype(BF16)
            half = w_in.shape[1] // 2
            xb, gt = _in_projection(x, shift[i], scale[i], g, [w_in[:, :half], w_in[:, half:]], [BF16, BF16])
            y = _rglru_mixer(xb, gt, lru_conv_w[j], lru_conv_b[j], lru_w_a[j], lru_w_x[j],
                             lru_b_a[j], lru_b_x[j], lru_lam[j])
            if i == depth - 1:
                return _out_projection_odd(x[0], y, gate_mod[i], od_w_out[j], final_g[None, :], split_at=bp)
            x = (_out_projection_odd(x[0], y, gate_mod[i], od_w_out[j], final_g[None, :]),)
```

```python
import functools
import math

import numpy as np
import jax
import jax.numpy as jnp
from jax import lax
from jax.experimental import pallas as pl
from jax.experimental.pallas import tpu as pltpu

F32 = jnp.float32
BF16 = jnp.bfloat16
HI = lax.Precision.HIGHEST

V7X_VMEM_BYTES = 64 * 1024 * 1024
VMEM_LIMIT = V7X_VMEM_BYTES - 8 * 1024 * 1024
LANES = 128
SUBLANES = 8

HY_EMB = 33
HY_SHORT_DECAY_PCT = 0.3
HY_LONG_DECAY_PCT = 1.5
HY_DECAY_TARGET = 1e-2
SSD_HEADS = 16
SSD_HEAD_DIM = 64
SSD_GROUPS = 4
SSD_STATE = 128
SSD_CHUNK = 128
LRU_HEADS = 16
LRU_BLOCK = 128
LRU_C = 8.0

DFT_N1 = 64
DFT_N2 = 128
DFT_K1 = DFT_N1 // 2 + 1


def _cparams(*sem):
    return pltpu.CompilerParams(dimension_semantics=sem, vmem_limit_bytes=VMEM_LIMIT)


def _resident(shape, index_map):
    return pl.BlockSpec(shape, index_map, pipeline_mode=pl.Buffered(1))


def _sigmoid(x):
    return 0.5 + 0.5 * jnp.tanh(0.5 * x)


def _silu(x):
    return x * _sigmoid(x)


def _softplus(x):
    return jnp.maximum(x, 0.0) + jnp.log1p(jnp.exp(-jnp.abs(x)))


def _split3(x):
    hi = x.astype(BF16)
    r1 = x - hi.astype(F32)
    mid = r1.astype(BF16)
    lo = (r1 - mid.astype(F32)).astype(BF16)
    return hi, mid, lo


def _sel_dot(sel, x):
    hi, mid, lo = _split3(x)
    d = functools.partial(jnp.dot, preferred_element_type=F32)
    return d(sel, hi) + d(sel, mid) + d(sel, lo)


def _mod_kernel(c_ref, w_ref, b_ref, o_ref):
    cs = _silu(c_ref[...])
    o_ref[0] = jnp.dot(cs, w_ref[0], precision=HI, preferred_element_type=F32) + b_ref[0]


def _modulation(c_pad, mod_w, mod_b):
    depth, d, d3 = mod_w.shape
    bp = c_pad.shape[0]
    return pl.pallas_call(
        _mod_kernel,
        grid=(depth, d3 // d),
        in_specs=[
            pl.BlockSpec((bp, d), lambda i, j: (0, 0)),
            pl.BlockSpec((1, d, d), lambda i, j: (i, 0, j)),
            pl.BlockSpec((1, 1, d), lambda i, j: (i, 0, j)),
        ],
        out_specs=pl.BlockSpec((1, bp, d), lambda i, j: (i, 0, j)),
        out_shape=jax.ShapeDtypeStruct((depth, bp, d3), F32),
        compiler_params=_cparams("arbitrary", "arbitrary"),
        name="adaln_mod",
    )(c_pad, mod_w, mod_b.reshape(depth, 1, d3))


def _pair_row_maps(bp, nj):
    first = lambda i, j: (jnp.minimum(i, bp - 1), jnp.where(i < bp, j, nj - 1), 0)
    second = lambda i, j: (jnp.maximum(i - bp, 0), jnp.where(i < bp, 0, j), 0)
    return first, second


def _load_rows(x_refs, bp):
    if len(x_refs) == 1:
        return x_refs[0][0]
    return jnp.where(pl.program_id(0) < bp, x_refs[0][0], x_refs[1][0])


def _row_specs(xs, rows, nj, f=lambda j: j):
    d = xs[0].shape[2]
    if len(xs) == 1:
        return [pl.BlockSpec((1, rows, d), lambda i, j: (i, f(j), 0))]
    bp = xs[0].shape[0]
    first = lambda i, j: (jnp.minimum(i, bp - 1), jnp.where(i < bp, f(j), f(nj - 1)), 0)
    second = lambda i, j: (jnp.maximum(i - bp, 0), jnp.where(i < bp, f(0), f(j)), 0)
    return [pl.BlockSpec((1, rows, d), first), pl.BlockSpec((1, rows, d), second)]


_PROJ_GROUP = 2 * LANES


_CONV_POSTS = ("conv", "conv_silu", "hyena")


def _inproj_kernel(*refs, n_x, bp, plan, tl):
    refs = list(refs)
    take = lambda n: [refs.pop(0) for _ in range(n)]
    has_conv = any(post in _CONV_POSTS for post, _, _ in plan)
    x_main = take(n_x)
    x_prev, x_next = (take(n_x), take(n_x)) if has_conv else (None, None)
    sh_ref, sc_ref, g_ref = take(3)
    piece_refs = [take(3 if post in _CONV_POSTS else 1) for post, _, _ in plan]
    out_refs = [take(n_out) for _, _, n_out in plan]
    pad_ref = refs.pop(0) if has_conv else None

    j, last = pl.program_id(1), pl.num_programs(1) - 1
    norm = lambda x: _norm_modulate(x, g_ref[...], sh_ref[0], sc_ref[0])
    d = functools.partial(jnp.dot, preferred_element_type=F32)
    hn = norm(_load_rows(x_main, bp))
    hb = hn.astype(BF16)
    if has_conv:
        hb_ext = jnp.concatenate([norm(_load_rows(x_prev, bp)), hn, norm(_load_rows(x_next, bp))],
                                 axis=0).astype(BF16)
    slabs = _PROJ_GROUP // LANES

    def project_to_pad(w_ref, col, pad):
        u = d(hb_ext, w_ref[:, col:col + _PROJ_GROUP])
        for s in range(slabs):
            ls = slice(s * LANES, (s + 1) * LANES)
            pad[s, pl.ds(0, _HALO), :] = jnp.where(j > 0, u[:_HALO, ls], 0.0)
            pad[s, pl.ds(_HALO, tl), :] = u[_HALO:_HALO + tl, ls]
            pad[s, pl.ds(_HALO + tl, _HALO), :] = jnp.where(j < last, u[_HALO + tl:, ls], 0.0)

    def conv_from_pad(pad, cw_ref, cb_ref, col, taps, r):
        parts = [_conv_rows(pad.at[s], r, cw_ref[:, col + s * LANES:col + (s + 1) * LANES],
                            cb_ref[:, col + s * LANES:col + (s + 1) * LANES], taps, taps // 2)
                 for s in range(slabs)]
        return jnp.concatenate(parts, axis=1)

    group = 0
    for (post, taps, _), p_refs, o_refs in zip(plan, piece_refs, out_refs):
        w_ref = p_refs[0]
        if post not in _CONV_POSTS:
            u = d(hb, w_ref[...])
            o_refs[0][0] = (_silu(u) if post == "silu" else u).astype(o_refs[0].dtype)
            continue
        cw_ref, cb_ref = p_refs[1:]
        parts = 3 if post == "hyena" else 1
        width = w_ref.shape[1] // parts
        for c0 in range(0, width, _PROJ_GROUP):
            pads = pad_ref.at[group % 2]
            group += 1
            for o in range(parts):
                project_to_pad(w_ref, o * width + c0, pads.at[o])
            for r in range(0, tl, _CONV_ROWS):
                rows, cols = pl.ds(r, _CONV_ROWS), slice(c0, c0 + _PROJ_GROUP)
                cv = [conv_from_pad(pads.at[o], cw_ref, cb_ref, o * width + c0, taps, r) for o in range(parts)]
                if post == "hyena":
                    o_refs[0][0, rows, cols] = cv[0].astype(o_refs[0].dtype)
                    o_refs[1][0, rows, cols] = (cv[2] * cv[1]).astype(o_refs[1].dtype)
                else:
                    y = _silu(cv[0]) if post == "conv_silu" else cv[0]
                    o_refs[0][0, rows, cols] = y.astype(o_refs[0].dtype)


def _in_projection(x, shift, scale, g, pieces, tl=512):
    b = sum(a.shape[0] for a in x)
    l, d = x[0].shape[1:]
    nj = l // tl
    is_conv = lambda p: p.get("post") in _CONV_POSTS
    outs = []
    plan = []
    for p in pieces:
        n = p["w"].shape[1]
        if p.get("post") == "hyena":
            outs += [(n // 3, p["dtype"][0]), (n // 3, p["dtype"][1])]
        else:
            outs.append((n, p["dtype"]))
        plan.append((p.get("post"), p["conv_w"].shape[0] if is_conv(p) else 0, 2 if p.get("post") == "hyena" else 1))
    has_conv = any(is_conv(p) for p in pieces)
    const = lambda a: _resident(a.shape, lambda i, j: (0,) * a.ndim)
    in_specs, args = _row_specs(x, tl, nj), list(x)
    if has_conv:
        per = tl // _HALO
        in_specs += _row_specs(x, _HALO, nj, lambda j: jnp.maximum(j * per - 1, 0))
        in_specs += _row_specs(x, _HALO, nj, lambda j: jnp.minimum((j + 1) * per, l // _HALO - 1))
        args += list(x) * 2
    in_specs += [pl.BlockSpec((1, 1, d), lambda i, j: (i, 0, 0)), pl.BlockSpec((1, 1, d), lambda i, j: (i, 0, 0)),
                 pl.BlockSpec((1, d), lambda i, j: (0, 0))]
    args += [shift, scale, g]
    for p in pieces:
        for a in [p["w"]] + ([p["conv_w"], p["conv_b"]] if is_conv(p) else []):
            in_specs.append(const(a))
            args.append(a)
    scratch = [pltpu.VMEM((2, 3, _PROJ_GROUP // LANES, tl + 2 * _HALO, LANES), F32)] if has_conv else []
    return pl.pallas_call(
        functools.partial(_inproj_kernel, n_x=len(x), bp=x[0].shape[0], plan=tuple(plan), tl=tl),
        grid=(b, nj),
        in_specs=in_specs,
        out_specs=[pl.BlockSpec((1, tl, n), lambda i, j: (i, j, 0)) for n, _ in outs],
        out_shape=[jax.ShapeDtypeStruct((b, l, n), dt) for n, dt in outs],
        scratch_shapes=scratch,
        compiler_params=_cparams("arbitrary", "arbitrary"),
        name="in_proj",
    )(*args)


_CONV_ROWS = 256
_HALO = SUBLANES


def _conv_rows(pad_ref, r, w, bias, taps, left):
    acc = None
    for k in range(taps):
        term = w[k:k + 1, :] * pad_ref[pl.ds(r + _HALO + k - left, _CONV_ROWS), :]
        acc = term if acc is None else acc + term
    return acc + bias


_SEG_BLOCK = SUBLANES * SUBLANES


def _filter_kernel(z_ref, w1_ref, b1_ref, fr_ref, w2_ref, b2_ref, w3_ref, dl_ref,
                   hfg_ref, nrm_ref, *, tr, c):
    i = pl.program_id(0)
    z = z_ref[...]
    fr = fr_ref[...]
    d = functools.partial(jnp.dot, precision=HI, preferred_element_type=F32)
    h = jnp.sin(fr * (d(z, w1_ref[...]) + b1_ref[...]))
    h = jnp.sin(fr * (d(h, w2_ref[...]) + b2_ref[...]))
    h = jnp.dot(h.astype(BF16), w3_ref[...].astype(BF16), preferred_element_type=F32)
    window = jnp.exp(-z[:, 0:1] * dl_ref[...])
    hf = h[:, :c] * window
    row = lax.broadcasted_iota(jnp.int32, (tr, 1), 0) + i * tr
    g = jnp.where(row == 0, 0.0, h[:, c:] * window)
    hfg_ref[0] = hf
    hfg_ref[1] = g
    part = jnp.sum(jnp.abs(hf) + jnp.abs(g), axis=0, keepdims=True)

    @pl.when(i == 0)
    def _():
        nrm_ref[...] = part

    @pl.when(i != 0)
    def _():
        nrm_ref[...] = nrm_ref[...] + part


def _hyena_filter_taps(zfeat, w1, b1, freq, w2, b2, w3, deltas, tr=512):
    l, zp = zfeat.shape
    hp = w1.shape[1]
    c2 = w3.shape[1]
    c = c2 // 2
    full = lambda shape: pl.BlockSpec(shape, lambda i: (0,) * len(shape))
    return pl.pallas_call(
        functools.partial(_filter_kernel, tr=tr, c=c),
        grid=(l // tr,),
        in_specs=[pl.BlockSpec((tr, zp), lambda i: (i, 0)), full((zp, hp)), full((1, hp)),
                  full((1, hp)), full((hp, hp)), full((1, hp)), full((hp, c2)), full((1, c))],
        out_specs=[pl.BlockSpec((2, tr, c), lambda i: (0, i, 0)), full((1, c))],
        out_shape=[jax.ShapeDtypeStruct((2, l, c), F32), jax.ShapeDtypeStruct((1, c), F32)],
        compiler_params=_cparams("arbitrary"),
        name="hyena_filter_mlp",
    )(zfeat, w1, b1, freq, w2, b2, w3, deltas)


def _filter_spec_kernel(h_ref, f1_ref, g_ref, nrm_ref, hre_ref, him_ref, s_ref):
    _, nh, ng, sub, tc = h_ref.shape
    n2 = ng * sub
    nreal = s_ref.shape[1]
    k1 = nreal // 2 + 1
    d = functools.partial(jnp.dot, preferred_element_type=F32)
    f1 = f1_ref[...]
    for q in range(2):
        for g in range(ng):
            x = h_ref[q, :, g].reshape(nh * sub, tc).astype(BF16)
            s_ref[q, :, g] = d(f1, x).reshape(nreal, sub, tc)
    inv = 1.0 / nrm_ref[...]
    zero = jnp.zeros((n2, tc), F32)

    def spectrum(k, has_im):
        xs = []
        for q in range(2):
            re = s_ref[q, k].reshape(n2, tc)
            im = s_ref[q, k1 - 1 + k].reshape(n2, tc) if has_im else zero
            xs.append(d(g_ref[k], jnp.concatenate([re, im], axis=0).astype(BF16)))
        xf, xg = xs
        hre_ref[k] = (xf[:n2] + xg[:n2]) * inv
        him_ref[k] = (xf[n2:] - xg[n2:]) * inv

    spectrum(0, False)
    spectrum(k1 - 1, False)

    def body(k, carry):
        spectrum(k, True)
        return carry

    lax.fori_loop(1, k1 - 1, body, 0, unroll=_LEVEL2_UNROLL)


def _filter_spectrum(hfg, f1k, g, nrm, tc=128):
    _, l, c = hfg.shape
    nh, sub = DFT_N1 // 2, SUBLANES
    ng = DFT_N2 // sub
    k1, n2 = g.shape[0], DFT_N2
    return pl.pallas_call(
        _filter_spec_kernel,
        grid=(c // tc,),
        in_specs=[pl.BlockSpec((2, nh, ng, sub, tc), lambda j: (0, 0, 0, 0, j)),
                  _resident(f1k.shape, lambda j: (0, 0)), _resident(g.shape, lambda j: (0, 0, 0)),
                  pl.BlockSpec((1, tc), lambda j: (0, j))],
        out_specs=[pl.BlockSpec((k1, n2, tc), lambda j: (0, 0, j))] * 2,
        out_shape=[jax.ShapeDtypeStruct((k1, n2, c), F32)] * 2,
        scratch_shapes=[pltpu.VMEM((2, DFT_N1, ng, sub, tc), F32)],
        compiler_params=_cparams("arbitrary"),
        name="hyena_filter_spectrum",
    )(hfg.reshape(2, nh, ng, sub, c), f1k, g, nrm)


_LEVEL2_UNROLL = 16


def _fftconv_kernel(vf_ref, x0_ref, gt_ref, f1_ref, a1_ref, g_ref, gi_ref, hre_ref, him_ref, bias_ref,
                    o_ref, s_ref, y_ref):
    nh, ng, sub, tc = vf_ref.shape[1:]
    n2 = ng * sub
    nreal = s_ref.shape[0]
    k1 = nreal // 2 + 1
    d = functools.partial(jnp.dot, preferred_element_type=F32)

    f1 = f1_ref[...]
    for g in range(ng):
        x = vf_ref[0, :, g].reshape(nh * sub, tc).astype(BF16)
        s_ref[:, g] = d(f1, x).reshape(nreal, sub, tc)

    def level2(k, re, im):
        x = d(g_ref[k], jnp.concatenate([re, im], axis=0).astype(BF16))
        xr, xi = x[:n2], x[n2:]
        hr, hi = hre_ref[k], him_ref[k]
        y = jnp.concatenate([xr * hr - xi * hi, xr * hi + xi * hr], axis=0).astype(BF16)
        return d(gi_ref[k], y)

    zero = jnp.zeros((n2, tc), F32)
    for k in (0, k1 - 1):
        z = level2(k, s_ref[k].reshape(n2, tc), zero)
        s_ref[k] = z[:n2].reshape(ng, sub, tc)

    def body(k, carry):
        z = level2(k, s_ref[k].reshape(n2, tc), s_ref[k1 - 1 + k].reshape(n2, tc))
        s_ref[k] = z[:n2].reshape(ng, sub, tc)
        s_ref[k1 - 1 + k] = z[n2:].reshape(ng, sub, tc)
        return carry

    lax.fori_loop(1, k1 - 1, body, 0, unroll=_LEVEL2_UNROLL)

    a1 = a1_ref[...]
    for g in range(ng):
        z = s_ref[:, g].reshape(nreal * sub, tc).astype(BF16)
        y_ref[:, g] = d(a1, z).reshape(nh, sub, tc)

    bias = bias_ref[...]
    step = 2
    rows = step * n2

    def epilogue(i, carry):
        r = pl.multiple_of(i * rows, rows)
        y = y_ref[pl.ds(i * step, step)].reshape(rows, tc)
        vf = vf_ref[0, pl.ds(i * step, step)].reshape(rows, tc)
        x0 = x0_ref[0, pl.ds(r, rows), :].astype(F32)
        gt = gt_ref[0, pl.ds(r, rows), :].astype(F32)
        o_ref[0, pl.ds(r, rows), :] = ((x0 * (y + vf * bias)) * gt).astype(o_ref.dtype)
        return carry

    lax.fori_loop(0, nh // step, epilogue, 0)


def _hyena_fftconv(vf, x0c, gate, f1k, a1k, g, ginv, hre, him, bias, tc=256):
    b, l, c = vf.shape
    nh, sub = DFT_N1 // 2, SUBLANES
    ng = DFT_N2 // sub
    kb = g.shape[0]
    n2 = DFT_N2
    rows = pl.BlockSpec((1, l, tc), lambda j, i: (i, 0, j))
    return pl.pallas_call(
        _fftconv_kernel,
        grid=(c // tc, b),
        in_specs=[pl.BlockSpec((1, nh, ng, sub, tc), lambda j, i: (i, 0, 0, 0, j)), rows, rows,
                  _resident(f1k.shape, lambda j, i: (0, 0)), _resident(a1k.shape, lambda j, i: (0, 0)),
                  _resident(g.shape, lambda j, i: (0, 0, 0)), _resident(ginv.shape, lambda j, i: (0, 0, 0)),
                  _resident((kb, n2, tc), lambda j, i: (0, 0, j)), _resident((kb, n2, tc), lambda j, i: (0, 0, j)),
                  pl.BlockSpec((1, tc), lambda j, i: (0, j))],
        out_specs=rows,
        out_shape=jax.ShapeDtypeStruct((b, l, c), BF16),
        scratch_shapes=[pltpu.VMEM((DFT_N1, ng, sub, tc), F32), pltpu.VMEM((nh, ng, sub, tc), F32)],
        compiler_params=_cparams("arbitrary", "arbitrary"),
        name="hyena_fftconv",
    )(vf.reshape(b, nh, ng, sub, c), x0c, gate, f1k, a1k, g, ginv, hre, him, bias)


@functools.lru_cache(maxsize=None)
def _dft_tables():
    n1, n2, k1 = DFT_N1, DFT_N2, DFT_K1
    n = n1 * n2
    nh = n1 // 2
    ang = 2.0 * np.pi * np.arange(k1)[:, None] * np.arange(nh)[None, :] / n1
    fre, fim = np.cos(ang), -np.sin(ang)
    m = np.arange(n2)[None, None, :]
    k2 = np.arange(n2)[None, :, None]
    ka = np.arange(k1)[:, None, None]
    ph = -2.0 * np.pi * m * (ka / n + k2 / n2)
    gr, gi = np.cos(ph), np.sin(ph)
    g = np.concatenate([np.concatenate([gr, -gi], axis=2), np.concatenate([gi, gr], axis=2)], axis=1)
    grt, git = np.swapaxes(gr, 1, 2), np.swapaxes(gi, 1, 2)
    ginv = np.concatenate([np.concatenate([grt, git], axis=2), np.concatenate([-git, grt], axis=2)], axis=1)
    wk = np.where((np.arange(k1) == 0) | (np.arange(k1) == n1 // 2), 1.0, 2.0)[None, :]
    are, aim = wk * fre.T / n, wk * fim.T / n
    eye = np.eye(SUBLANES)
    f1k = np.kron(np.concatenate([fre, fim[1:k1 - 1]], axis=0), eye)
    a1k = np.kron(np.concatenate([are, aim[:, 1:k1 - 1]], axis=1), eye)
    return tuple(np.asarray(a, np.float32).astype(BF16) for a in (g, ginv, f1k, a1k))


def _position_features(l):
    f32 = F32
    pos = jnp.arange(l, dtype=f32)[:, None]
    t = pos / max(l - 1, 1)
    bands = (HY_EMB - 1) // 2
    f = jnp.linspace(1e-4, bands - 1, bands, dtype=f32)[None, :]
    ang = f * pos * (2.0 * math.pi / l)
    z = jnp.concatenate([t, jnp.cos(ang), -jnp.sin(ang)], axis=-1)
    return jnp.pad(z, ((0, 0), (0, LANES - HY_EMB)))


def _pad2(a, rows, cols):
    return jnp.pad(a, ((0, rows - a.shape[0]), (0, cols - a.shape[1])))


def _hyena_spectrum(l, c, fw1, fb1, fw2, fb2, fw3, freq):
    g, _, f1k, _ = _dft_tables()
    hp = LANES
    zfeat = _position_features(l)
    w1 = _pad2(fw1, LANES, hp)
    w2 = _pad2(fw2, hp, hp)
    w3 = _pad2(fw3, hp, fw3.shape[1])
    row = lambda v: _pad2(v[None, :], 1, hp)
    deltas = jnp.abs(jnp.linspace(math.log(HY_DECAY_TARGET) / HY_LONG_DECAY_PCT,
                                  math.log(HY_DECAY_TARGET) / HY_SHORT_DECAY_PCT, c, dtype=F32))[None, :]
    hfg, nrm = _hyena_filter_taps(zfeat, w1, row(fb1), row(freq), w2, row(fb2), w3, deltas)
    return _filter_spectrum(hfg, jnp.asarray(f1k), jnp.asarray(g), nrm)


def _hyena_branch(x0c, vf, hy_gate, hre, him, bias):
    g, ginv, f1k, a1k = (jnp.asarray(t) for t in _dft_tables())
    return _hyena_fftconv(vf, x0c, hy_gate, f1k, a1k, g, ginv, hre, him, bias[None, :])


def _ssd_kernel(xs_ref, bm_ref, cm_ref, dt_ref, dtb_ref, alog_ref, rep_ref, dsk_ref, mask_ref,
                *rest, reverse, lane0, chunks):
    yo_ref = rest[0] if len(rest) == 3 else None
    y_ref, prev_ref = rest[-2:]
    q, n, hd = SSD_CHUNK, SSD_STATE, SSD_HEAD_DIM
    gw = (SSD_HEADS // SSD_GROUPS) * hd

    @pl.when(pl.program_id(1) == 0)
    def _():
        prev_ref[...] = jnp.zeros_like(prev_ref)

    mask = mask_ref[...]
    maskb = mask > 0
    neg_a = -jnp.exp(alog_ref[...])
    lo_lanes = lax.broadcasted_iota(jnp.int32, (1, LANES), 1) < hd
    d = functools.partial(jnp.dot, preferred_element_type=F32)
    npair = gw // LANES
    edge = 0 if reverse else q - 1

    def per_pair(v, c0, shape):
        b0 = jnp.broadcast_to(v[:, c0:c0 + 1], shape)
        b1 = jnp.broadcast_to(v[:, c0 + 1:c0 + 2], shape)
        return jnp.where(lo_lanes, b0, b1)

    order = range(chunks - 1, -1, -1) if reverse else range(chunks)
    for k in order:
        rows = pl.ds(k * q, q)
        xs_b = xs_ref[0, rows, :]
        bm = bm_ref[0, rows, :]
        cm = cm_ref[0, rows, :]
        dt = _softplus(dt_ref[0, rows, :] + dtb_ref[...])
        cum = _sel_dot(mask, dt * neg_a)
        src_t = (cum - jnp.log(dt)).T
        cum_e = cum[edge:edge + 1, :]
        e_in = jnp.exp(cum)
        w_out = dt * jnp.exp(cum_e - cum)
        e_all = jnp.exp(cum_e)
        spread = d(jnp.concatenate([e_in, w_out], axis=0).astype(BF16), rep_ref[...])
        e_x = spread[:q]
        xsd_all = xs_b * spread[q:].astype(BF16)
        ys = []
        for g in range(SSD_GROUPS):
            cg = cm[:, g * n:(g + 1) * n].astype(BF16)
            bg = bm[:, g * n:(g + 1) * n].astype(BF16)
            cb = lax.dot_general(cg, bg, (((1,), (1,)), ((), ())), preferred_element_type=F32)
            sl = slice(g * gw, (g + 1) * gw)
            prev_g = prev_ref[:, sl]
            y_off = d(cg, prev_g.astype(BF16))
            cds = []
            for pair in range(npair):
                lo = g * gw + pair * LANES
                c0 = lane0 + lo // hd
                ms = []
                for c in (c0, c0 + 1):
                    seg_dt = jnp.exp(jnp.where(maskb, cum[:, c:c + 1] - src_t[c:c + 1, :], -jnp.inf))
                    ms.append((cb * seg_dt).astype(BF16))
                xp = xs_b[:, lo:lo + LANES]
                zero = jnp.zeros_like(xp)
                rhs = jnp.concatenate([jnp.where(lo_lanes, xp, zero), jnp.where(lo_lanes, zero, xp)], axis=0)
                yd = d(jnp.concatenate(ms, axis=1), rhs)
                ys.append(yd + y_off[:, pair * LANES:(pair + 1) * LANES] * e_x[:, lo:lo + LANES])
                cds.append(per_pair(e_all, c0, (n, LANES)))
            st = lax.dot_general(bg, xsd_all[:, sl], (((0,), (0,)), ((), ())), preferred_element_type=F32)
            prev_ref[:, sl] = prev_g * jnp.concatenate(cds, axis=1) + st
        y = jnp.concatenate(ys, axis=1)
        if not reverse:
            y = y + xs_b.astype(F32) * dsk_ref[...]
        if yo_ref is not None:
            y = y + yo_ref[0, rows, :].astype(F32)
        y_ref[0, rows, :] = y.astype(y_ref.dtype)


def _ssd_direction(xbc, dt_raw, dtb, alog, rep, dskip, mask, y_other=None, *, reverse, ts=1024):
    b, l, _ = xbc.shape
    hp = SSD_HEADS * SSD_HEAD_DIM
    gn = SSD_GROUPS * SSD_STATE
    nt = l // ts
    tmap = (lambda j: nt - 1 - j) if reverse else (lambda j: j)
    full = lambda shape: pl.BlockSpec(shape, lambda i, j: (0,) * len(shape))
    yspec = pl.BlockSpec((1, ts, hp), lambda i, j: (i, tmap(j), 0))
    extra = [] if y_other is None else [y_other]
    return pl.pallas_call(
        functools.partial(_ssd_kernel, reverse=reverse, lane0=SSD_HEADS if reverse else 0,
                          chunks=ts // SSD_CHUNK),
        grid=(b, nt),
        in_specs=[yspec,
                  pl.BlockSpec((1, ts, gn), lambda i, j: (i, tmap(j), hp // gn)),
                  pl.BlockSpec((1, ts, gn), lambda i, j: (i, tmap(j), hp // gn + 1)),
                  pl.BlockSpec((1, ts, LANES), lambda i, j: (i, tmap(j), 0)),
                  full((1, LANES)), full((1, LANES)), full((LANES, hp)), full((1, hp)),
                  full((SSD_CHUNK, SSD_CHUNK))] + [yspec] * len(extra),
        out_specs=yspec,
        out_shape=jax.ShapeDtypeStruct((b, l, hp), BF16),
        scratch_shapes=[pltpu.VMEM((SSD_STATE, hp), F32)],
        compiler_params=_cparams("parallel", "arbitrary"),
        name="ssd_bwd" if reverse else "ssd_fwd",
    )(xbc, xbc, xbc, dt_raw, dtb, alog, rep, dskip, mask, *extra)


def _ssd_branch(xc, dt_raw, dt_bias, a_log, d_skip):
    pad_row = lambda v: _pad2(v.reshape(1, -1), 1, LANES)
    dtb, alog = pad_row(dt_bias), pad_row(a_log)
    dskip = jnp.repeat(d_skip, SSD_HEAD_DIM)[None, :]
    lane_head = np.arange(SSD_HEADS * SSD_HEAD_DIM)[None, :] // SSD_HEAD_DIM
    src = np.arange(LANES)[:, None]
    idx = np.arange(SSD_CHUNK)
    y = None
    for reverse in (False, True):
        rep = jnp.asarray(src == lane_head + (SSD_HEADS if reverse else 0), dtype=BF16)
        allowed = (idx[None, :] >= idx[:, None]) if reverse else (idx[None, :] <= idx[:, None])
        mask = jnp.asarray(allowed, dtype=BF16)
        y = _ssd_direction(xc, dt_raw, dtb, alog, rep, dskip, mask, y, reverse=reverse)
    return y


_LRU_ROWS = 256


def _lru_kernel(*refs, l, heads):
    x_refs, (gt_ref, wa_ref, wx_ref, ba_ref, bx_ref, lam_ref, o_ref), seq_refs = (
        refs[:heads], refs[heads:heads + 7], refs[heads + 7:])
    blk = _SEG_BLOCK
    d = functools.partial(jnp.dot, preferred_element_type=F32)

    def rows_segment_order(hd, r):
        x = x_refs[hd].at[0]
        return jnp.concatenate([x[pl.ds(r + base + j, SUBLANES, stride=SUBLANES), :]
                                for base in range(0, _CONV_ROWS, blk) for j in range(SUBLANES)], axis=0)
    e0 = (-0.5 * LRU_C / math.log(2.0)) * _softplus(-lam_ref[...])
    ba, bx = ba_ref[...], bx_ref[...]
    a_refs, u_refs, h_refs = ([[seq_refs[(q * 2 + dr) * heads + hd] for hd in range(heads)] for dr in range(2)]
                              for q in range(3))

    def gates(i, carry):
        r = pl.multiple_of(i * _CONV_ROWS, _CONV_ROWS)
        for hd in range(heads):
            ls = slice(hd * LRU_BLOCK, (hd + 1) * LRU_BLOCK)
            xh = rows_segment_order(hd, r)
            xb = xh.astype(BF16)
            xh2 = 0.5 * xh
            for dr in range(2):
                tr = jnp.tanh(d(xb, wa_ref[dr, hd]) + ba[dr:dr + 1, ls])
                ti = jnp.tanh(d(xb, wx_ref[dr, hd]) + bx[dr:dr + 1, ls])
                e = e0[dr:dr + 1, ls]
                a = jnp.exp2(e + e * tr)
                s = 1.0 - a * a
                root = jnp.where(s > 0.0, s * lax.rsqrt(s), 0.0)
                u = (xh2 + xh2 * ti) * root
                a_refs[dr][hd][pl.ds(r, _CONV_ROWS), :] = a
                u_refs[dr][hd][pl.ds(r, _CONV_ROWS), :] = u
        return carry

    lax.fori_loop(0, l // _CONV_ROWS, gates, 0)

    row = lax.broadcasted_iota(jnp.int32, (SUBLANES, LRU_BLOCK), 0)
    nblk = l // blk

    def sublane_scan(a, u, reverse):
        for s in (1, 2, 4):
            sh = (SUBLANES - s) if reverse else s
            valid = (row < SUBLANES - s) if reverse else (row >= s)
            a_sh = jnp.where(valid, pltpu.roll(a, sh, axis=0), 1.0)
            u_sh = jnp.where(valid, pltpu.roll(u, sh, axis=0), 0.0)
            u = u + a * u_sh
            a = a * a_sh
        return a, u

    def block_scan(a_ref, u_ref, h_ref, r, h_in, reverse):
        order = range(SUBLANES - 1, -1, -1) if reverse else range(SUBLANES)
        hs, ps = {}, {}
        h = p = None
        for k in order:
            rows = pl.ds(r + SUBLANES * k, SUBLANES)
            a, u = a_ref[rows, :], u_ref[rows, :]
            h = u if h is None else a * h + u
            p = a if p is None else a * p
            hs[k], ps[k] = h, p
        sp, se = sublane_scan(p, h, reverse)
        full = se + sp * h_in
        edge = 0 if reverse else SUBLANES - 1
        h_out = jnp.broadcast_to(full[edge:edge + 1, :], (SUBLANES, LRU_BLOCK))
        if reverse:
            carry = jnp.where(row == SUBLANES - 1, h_in, pltpu.roll(full, SUBLANES - 1, axis=0))
        else:
            carry = jnp.where(row == 0, h_in, pltpu.roll(full, 1, axis=0))
        for k in order:
            h_ref[pl.ds(r + k, SUBLANES, stride=SUBLANES), :] = hs[k] + ps[k] * carry
        return h_out

    def scan(i, carry):
        rf = pl.multiple_of(i * blk, blk)
        rb = pl.multiple_of((nblk - 1 - i) * blk, blk)
        out = []
        for hd in range(heads):
            hf, hb = carry[2 * hd], carry[2 * hd + 1]
            out.append(block_scan(a_refs[0][hd], u_refs[0][hd], h_refs[0][hd], rf, hf, False))
            out.append(block_scan(a_refs[1][hd], u_refs[1][hd], h_refs[1][hd], rb, hb, True))
        return tuple(out)

    zero = jnp.zeros((SUBLANES, LRU_BLOCK), F32)
    lax.fori_loop(0, nblk, scan, (zero,) * (2 * heads), unroll=2)

    def combine(i, carry):
        r = pl.multiple_of(i * _LRU_ROWS, _LRU_ROWS)
        rows = pl.ds(r, _LRU_ROWS)
        h = jnp.concatenate([h_refs[0][hd][rows, :] + h_refs[1][hd][rows, :] for hd in range(heads)], axis=1)
        o_ref[0, rows, :] = (h * gt_ref[0, rows, :].astype(F32)).astype(o_ref.dtype)
        return carry

    lax.fori_loop(0, l // _LRU_ROWS, combine, 0)


def _rglru_mixer(xc, gate, w_a, w_x, b_a, b_x, lam, heads=2):
    b, l, w = xc.shape
    tw = heads * LRU_BLOCK
    gspec = pl.BlockSpec((1, l, tw), lambda i, j: (i, 0, j))
    xspecs = [pl.BlockSpec((1, l, LRU_BLOCK), lambda i, j, hd=hd: (i, 0, j * heads + hd)) for hd in range(heads)]
    wspec = pl.BlockSpec((2, heads, LRU_BLOCK, LRU_BLOCK), lambda i, j: (0, j, 0, 0))
    vspec = pl.BlockSpec((2, tw), lambda i, j: (0, j))
    seqs = [pltpu.VMEM((l, LRU_BLOCK), F32)] * (3 * 2 * heads)
    return pl.pallas_call(
        functools.partial(_lru_kernel, l=l, heads=heads),
        grid=(b, w // tw),
        in_specs=xspecs + [gspec, wspec, wspec, vspec, vspec, vspec],
        out_specs=gspec,
        out_shape=jax.ShapeDtypeStruct((b, l, w), BF16),
        scratch_shapes=seqs,
        compiler_params=_cparams("parallel", "parallel"),
        name="rglru",
    )(*([xc] * heads), gate, (0.5 * w_a).astype(BF16), (0.5 * w_x).astype(BF16), 0.5 * b_a, 0.5 * b_x, lam)


def _rms(x, g, eps):
    return (x * lax.rsqrt(jnp.mean(x * x, axis=-1, keepdims=True) + eps)) * g


def _norm_modulate(x, g, shift, scale):
    ms = jnp.mean(x * x, axis=-1, keepdims=True)
    return ((x * lax.rsqrt(ms + 1e-6)) * g) * (1.0 + scale) + shift


def _outproj_even_kernel(*refs, n_x, bp):
    x_refs, (ya_ref, ys_ref, z_ref, gm_ref, ng_ref, wa_ref, wb_ref, o_ref) = refs[:n_x], refs[n_x:]
    yb = ys_ref[0].astype(F32) * z_ref[0].astype(F32)
    ng = ng_ref[...]
    gw = yb.shape[1] // SSD_GROUPS
    parts = [_rms(yb[:, g * gw:(g + 1) * gw], ng[:, g * gw:(g + 1) * gw], 1e-5) for g in range(SSD_GROUPS)]
    ybn = jnp.concatenate(parts, axis=1)
    d = functools.partial(jnp.dot, preferred_element_type=F32)
    acc = d(ya_ref[0].astype(BF16), wa_ref[...]) + d(ybn.astype(BF16), wb_ref[...])
    o_ref[0] = _load_rows(x_refs, bp) + gm_ref[0] * acc


def _out_projection_even(x, ya, ys, z, gate_mod, norm_g, w_out, tl=512):
    b, l, c = ya.shape
    dm = x[0].shape[2]
    rows = lambda n: pl.BlockSpec((1, tl, n), lambda i, j: (i, j, 0))
    wa, wb = w_out[:c].astype(BF16), w_out[c:].astype(BF16)
    return pl.pallas_call(
        functools.partial(_outproj_even_kernel, n_x=len(x), bp=x[0].shape[0]),
        grid=(b, l // tl),
        in_specs=_row_specs(x, tl, l // tl) + [rows(c), rows(c), rows(c),
                                      pl.BlockSpec((1, 1, dm), lambda i, j: (i, 0, 0)),
                                      pl.BlockSpec((1, c), lambda i, j: (0, 0)),
                                      _resident(wa.shape, lambda i, j: (0, 0)),
                                      _resident(wb.shape, lambda i, j: (0, 0))],
        out_specs=rows(dm),
        out_shape=jax.ShapeDtypeStruct((b, l, dm), F32),
        compiler_params=_cparams("arbitrary", "arbitrary"),
        name="out_proj_even",
    )(*x, ya, ys, z, gate_mod, norm_g, wa, wb)


def _outproj_odd_kernel(x_ref, y_ref, gm_ref, w_ref, fg_ref, *o_refs, bp):
    acc = jnp.dot(y_ref[0].astype(BF16), w_ref[...], preferred_element_type=F32)
    xn = x_ref[0] + gm_ref[0] * acc
    if len(o_refs) == 1:
        o_refs[0][0] = xn
        return
    out = _rms(xn, fg_ref[...], 1e-6)

    @pl.when(pl.program_id(0) < bp)
    def _():
        o_refs[0][0] = out

    @pl.when(pl.program_id(0) >= bp)
    def _():
        o_refs[1][0] = out


def _out_projection_odd(x, y, gate_mod, w_out, final_g, split_at=None, tl=512):
    b, l, dm = x.shape
    w = y.shape[2]
    rows = lambda n: pl.BlockSpec((1, tl, n), lambda i, j: (i, j, 0))
    wb = w_out.astype(BF16)
    if split_at is None:
        out_specs = [rows(dm)]
        out_shape = [jax.ShapeDtypeStruct((b, l, dm), F32)]
    else:
        out_specs = [pl.BlockSpec((1, tl, dm), m) for m in _pair_row_maps(split_at, l // tl)]
        out_shape = [jax.ShapeDtypeStruct((n, l, dm), F32) for n in (split_at, b - split_at)]
    out = pl.pallas_call(
        functools.partial(_outproj_odd_kernel, bp=split_at),
        grid=(b, l // tl),
        in_specs=[rows(dm), rows(w),
                  pl.BlockSpec((1, 1, dm), lambda i, j: (i, 0, 0)),
                  _resident(wb.shape, lambda i, j: (0, 0)),
                  pl.BlockSpec((1, dm), lambda i, j: (0, 0))],
        out_specs=out_specs,
        out_shape=out_shape,
        compiler_params=_cparams("arbitrary", "arbitrary"),
        name="out_proj_odd",
    )(x, y, gate_mod, wb, final_g)
    return out[0] if split_at is None else tuple(out)


def kernel(x_prompt, x_sample, c_prompt, c_sample, mod_w, mod_b, norm_g, final_g, ev_w_in, ev_w_out, hy_conv_w, hy_conv_b, hy_fw1, hy_fb1, hy_fw2, hy_fb2, hy_fw3, hy_freq, hy_bias, ssd_conv_w, ssd_conv_b, ssd_dt_bias, ssd_A_log, ssd_D, ssd_norm_g, od_w_in, od_w_out, lru_conv_w, lru_conv_b, lru_w_a, lru_b_a, lru_w_x, lru_b_x, lru_lam):
    bp, l, dm = x_prompt.shape
    assert x_sample.shape[1:] == (l, dm), "the two request groups are stacked on the batch axis"
    x = (x_prompt, x_sample)
    c = jnp.concatenate([c_prompt, c_sample], axis=0)
    b = c.shape[0]
    depth = mod_w.shape[0]
    assert 2 * l == DFT_N1 * DFT_N2
    assert depth % 2 == 0, "the final norm is fused into the last (odd) layer's output projection"

    b_pad = -(-b // SUBLANES) * SUBLANES
    mods = _modulation(jnp.pad(c, ((0, b_pad - b), (0, 0))), mod_w, mod_b)[:, :b]
    shift, scale, gate_mod = (mods[:, :, k * dm:(k + 1) * dm].reshape(depth, b, 1, dm) for k in range(3))

    hyc = hy_bias.shape[1]
    ssw = ssd_norm_g.shape[1]
    xbw = ssd_conv_w.shape[2]
    o1, o2, o3, o4 = 3 * hyc, 4 * hyc, 4 * hyc + ssw, 4 * hyc + ssw + xbw

    for i in range(depth):
        j = i // 2
        g = norm_g[i][None, :]
        if i % 2 == 0:
            w_in = ev_w_in[j].astype(BF16)
            w_dt = jnp.pad(w_in[:, o4:], ((0, 0), (0, LANES - (w_in.shape[1] - o4))))
            x0c, vf, hyg, z, xc, dt_raw = _in_projection(x, shift[i], scale[i], g, [
                dict(w=w_in[:, :o1], dtype=(BF16, F32), post="hyena", conv_w=hy_conv_w[j],
                     conv_b=hy_conv_b[j][None, :]),
                dict(w=w_in[:, o1:o2], dtype=BF16, post="silu"),
                dict(w=w_in[:, o2:o3], dtype=BF16, post="silu"),
                dict(w=w_in[:, o3:o4], dtype=BF16, post="conv_silu", conv_w=ssd_conv_w[j],
                     conv_b=ssd_conv_b[j][None, :]),
                dict(w=w_dt, dtype=F32)])
            hre, him = _hyena_spectrum(l, hyc, hy_fw1[j], hy_fb1[j], hy_fw2[j], hy_fb2[j], hy_fw3[j], hy_freq[j])
            ya = _hyena_branch(x0c, vf, hyg, hre, him, hy_bias[j])
            ys = _ssd_branch(xc, dt_raw, ssd_dt_bias[j], ssd_A_log[j], ssd_D[j])
            x = (_out_projection_even(x, ya, ys, z, gate_mod[i], ssd_norm_g[j][None, :], ev_w_out[j]),)
        else:
            w_in = od_w_in[j].astype(BF16)
            half = w_in.shape[1] // 2
            xc, gt = _in_projection(x, shift[i], scale[i], g, [
                dict(w=w_in[:, :half], dtype=F32, post="conv", conv_w=lru_conv_w[j], conv_b=lru_conv_b[j][None, :]),
                dict(w=w_in[:, half:], dtype=BF16, post="silu")])
            y = _rglru_mixer(xc, gt, lru_w_a[j], lru_w_x[j], lru_b_a[j], lru_b_x[j], lru_lam[j])
            if i == depth - 1:
                return _out_projection_odd(x[0], y, gate_mod[i], od_w_out[j], final_g[None, :], split_at=bp)
            x = (_out_projection_odd(x[0], y, gate_mod[i], od_w_out[j], final_g[None, :]),)
```

```python
import functools
import math

import numpy as np
import jax
import jax.numpy as jnp
from jax import lax
from jax.experimental import pallas as pl
from jax.experimental.pallas import tpu as pltpu

F32 = jnp.float32
BF16 = jnp.bfloat16
HI = lax.Precision.HIGHEST

V7X_VMEM_BYTES = 64 * 1024 * 1024
VMEM_LIMIT = V7X_VMEM_BYTES - 8 * 1024 * 1024
LANES = 128
SUBLANES = 8

HY_EMB = 33
HY_SHORT_DECAY_PCT = 0.3
HY_LONG_DECAY_PCT = 1.5
HY_DECAY_TARGET = 1e-2
SSD_HEADS = 16
SSD_HEAD_DIM = 64
SSD_GROUPS = 4
SSD_STATE = 128
SSD_CHUNK = 128
LRU_HEADS = 16
LRU_BLOCK = 128
LRU_C = 8.0

DFT_N1 = 64
DFT_N2 = 128
DFT_K1 = DFT_N1 // 2 + 1


def _cparams(*sem):
    return pltpu.CompilerParams(dimension_semantics=sem, vmem_limit_bytes=VMEM_LIMIT)


def _resident(shape, index_map):
    return pl.BlockSpec(shape, index_map, pipeline_mode=pl.Buffered(1))


def _sigmoid(x):
    return 0.5 + 0.5 * jnp.tanh(0.5 * x)


def _silu(x):
    return x * _sigmoid(x)


def _softplus(x):
    return jnp.maximum(x, 0.0) + jnp.log1p(jnp.exp(-jnp.abs(x)))


def _split3(x):
    hi = x.astype(BF16)
    r1 = x - hi.astype(F32)
    mid = r1.astype(BF16)
    lo = (r1 - mid.astype(F32)).astype(BF16)
    return hi, mid, lo


def _sel_dot(sel, x):
    hi, mid, lo = _split3(x)
    d = functools.partial(jnp.dot, preferred_element_type=F32)
    return d(sel, hi) + d(sel, mid) + d(sel, lo)


def _mod_kernel(c_ref, w_ref, b_ref, o_ref):
    cs = _silu(c_ref[...])
    o_ref[0] = jnp.dot(cs, w_ref[0], precision=HI, preferred_element_type=F32) + b_ref[0]


def _modulation(c_pad, mod_w, mod_b):
    depth, d, d3 = mod_w.shape
    bp = c_pad.shape[0]
    return pl.pallas_call(
        _mod_kernel,
        grid=(depth, d3 // d),
        in_specs=[
            pl.BlockSpec((bp, d), lambda i, j: (0, 0)),
            pl.BlockSpec((1, d, d), lambda i, j: (i, 0, j)),
            pl.BlockSpec((1, 1, d), lambda i, j: (i, 0, j)),
        ],
        out_specs=pl.BlockSpec((1, bp, d), lambda i, j: (i, 0, j)),
        out_shape=jax.ShapeDtypeStruct((depth, bp, d3), F32),
        compiler_params=_cparams("arbitrary", "arbitrary"),
        name="adaln_mod",
    )(c_pad, mod_w, mod_b.reshape(depth, 1, d3))


def _pair_row_maps(bp, nj):
    first = lambda i, j: (jnp.minimum(i, bp - 1), jnp.where(i < bp, j, nj - 1), 0)
    second = lambda i, j: (jnp.maximum(i - bp, 0), jnp.where(i < bp, 0, j), 0)
    return first, second


def _load_rows(x_refs, bp):
    if len(x_refs) == 1:
        return x_refs[0][0]
    return jnp.where(pl.program_id(0) < bp, x_refs[0][0], x_refs[1][0])


def _row_specs(xs, rows, nj, f=lambda j: j):
    d = xs[0].shape[2]
    if len(xs) == 1:
        return [pl.BlockSpec((1, rows, d), lambda i, j: (i, f(j), 0))]
    bp = xs[0].shape[0]
    first = lambda i, j: (jnp.minimum(i, bp - 1), jnp.where(i < bp, f(j), f(nj - 1)), 0)
    second = lambda i, j: (jnp.maximum(i - bp, 0), jnp.where(i < bp, f(0), f(j)), 0)
    return [pl.BlockSpec((1, rows, d), first), pl.BlockSpec((1, rows, d), second)]


_PROJ_GROUP = 2 * LANES


_CONV_POSTS = ("conv", "conv_silu", "hyena")


def _inproj_kernel(*refs, n_x, bp, plan, tl):
    refs = list(refs)
    take = lambda n: [refs.pop(0) for _ in range(n)]
    has_conv = any(post in _CONV_POSTS for post, _, _ in plan)
    x_main = take(n_x)
    x_prev, x_next = (take(n_x), take(n_x)) if has_conv else (None, None)
    sh_ref, sc_ref, g_ref = take(3)
    piece_refs = [take(3 if post in _CONV_POSTS else 1) for post, _, _ in plan]
    out_refs = [take(n_out) for _, _, n_out in plan]
    pad_ref = refs.pop(0) if has_conv else None

    j, last = pl.program_id(1), pl.num_programs(1) - 1
    norm = lambda x: _norm_modulate(x, g_ref[...], sh_ref[0], sc_ref[0])
    d = functools.partial(jnp.dot, preferred_element_type=F32)
    hn = norm(_load_rows(x_main, bp))
    hb = hn.astype(BF16)
    if has_conv:
        hb_ext = jnp.concatenate([norm(_load_rows(x_prev, bp)), hn, norm(_load_rows(x_next, bp))],
                                 axis=0).astype(BF16)
    slabs = _PROJ_GROUP // LANES

    def project_to_pad(w_ref, col, pad):
        u = d(hb_ext, w_ref[:, col:col + _PROJ_GROUP])
        for s in range(slabs):
            ls = slice(s * LANES, (s + 1) * LANES)
            pad[s, pl.ds(0, _HALO), :] = jnp.where(j > 0, u[:_HALO, ls], 0.0)
            pad[s, pl.ds(_HALO, tl), :] = u[_HALO:_HALO + tl, ls]
            pad[s, pl.ds(_HALO + tl, _HALO), :] = jnp.where(j < last, u[_HALO + tl:, ls], 0.0)

    def conv_from_pad(pad, cw_ref, cb_ref, col, taps, r):
        parts = [_conv_rows(pad.at[s], r, cw_ref[:, col + s * LANES:col + (s + 1) * LANES],
                            cb_ref[:, col + s * LANES:col + (s + 1) * LANES], taps, taps // 2)
                 for s in range(slabs)]
        return jnp.concatenate(parts, axis=1)

    group = 0
    for (post, taps, _), p_refs, o_refs in zip(plan, piece_refs, out_refs):
        w_ref = p_refs[0]
        if post not in _CONV_POSTS:
            u = d(hb, w_ref[...])
            o_refs[0][0] = (_silu(u) if post == "silu" else u).astype(o_refs[0].dtype)
            continue
        cw_ref, cb_ref = p_refs[1:]
        parts = 3 if post == "hyena" else 1
        width = w_ref.shape[1] // parts
        for c0 in range(0, width, _PROJ_GROUP):
            pads = pad_ref.at[group % 2]
            group += 1
            for o in range(parts):
                project_to_pad(w_ref, o * width + c0, pads.at[o])
            for r in range(0, tl, _CONV_ROWS):
                rows, cols = pl.ds(r, _CONV_ROWS), slice(c0, c0 + _PROJ_GROUP)
                cv = [conv_from_pad(pads.at[o], cw_ref, cb_ref, o * width + c0, taps, r) for o in range(parts)]
                if post == "hyena":
                    o_refs[0][0, rows, cols] = cv[0].astype(o_refs[0].dtype)
                    o_refs[1][0, rows, cols] = (cv[2] * cv[1]).astype(o_refs[1].dtype)
                else:
                    y = _silu(cv[0]) if post == "conv_silu" else cv[0]
                    o_refs[0][0, rows, cols] = y.astype(o_refs[0].dtype)


def _in_projection(x, shift, scale, g, pieces, tl=512):
    b = sum(a.shape[0] for a in x)
    l, d = x[0].shape[1:]
    nj = l // tl
    is_conv = lambda p: p.get("post") in _CONV_POSTS
    outs = []
    plan = []
    for p in pieces:
        n = p["w"].shape[1]
        if p.get("post") == "hyena":
            outs += [(n // 3, p["dtype"][0]), (n // 3, p["dtype"][1])]
        else:
            outs.append((n, p["dtype"]))
        plan.append((p.get("post"), p["conv_w"].shape[0] if is_conv(p) else 0, 2 if p.get("post") == "hyena" else 1))
    has_conv = any(is_conv(p) for p in pieces)
    const = lambda a: _resident(a.shape, lambda i, j: (0,) * a.ndim)
    in_specs, args = _row_specs(x, tl, nj), list(x)
    if has_conv:
        per = tl // _HALO
        in_specs += _row_specs(x, _HALO, nj, lambda j: jnp.maximum(j * per - 1, 0))
        in_specs += _row_specs(x, _HALO, nj, lambda j: jnp.minimum((j + 1) * per, l // _HALO - 1))
        args += list(x) * 2
    in_specs += [pl.BlockSpec((1, 1, d), lambda i, j: (i, 0, 0)), pl.BlockSpec((1, 1, d), lambda i, j: (i, 0, 0)),
                 pl.BlockSpec((1, d), lambda i, j: (0, 0))]
    args += [shift, scale, g]
    for p in pieces:
        for a in [p["w"]] + ([p["conv_w"], p["conv_b"]] if is_conv(p) else []):
            in_specs.append(const(a))
            args.append(a)
    scratch = [pltpu.VMEM((2, 3, _PROJ_GROUP // LANES, tl + 2 * _HALO, LANES), F32)] if has_conv else []
    return pl.pallas_call(
        functools.partial(_inproj_kernel, n_x=len(x), bp=x[0].shape[0], plan=tuple(plan), tl=tl),
        grid=(b, nj),
        in_specs=in_specs,
        out_specs=[pl.BlockSpec((1, tl, n), lambda i, j: (i, j, 0)) for n, _ in outs],
        out_shape=[jax.ShapeDtypeStruct((b, l, n), dt) for n, dt in outs],
        scratch_shapes=scratch,
        compiler_params=_cparams("arbitrary", "arbitrary"),
        name="in_proj",
    )(*args)


_CONV_ROWS = 256
_HALO = SUBLANES


def _conv_rows(pad_ref, r, w, bias, taps, left):
    acc = None
    for k in range(taps):
        term = w[k:k + 1, :] * pad_ref[pl.ds(r + _HALO + k - left, _CONV_ROWS), :]
        acc = term if acc is None else acc + term
    return acc + bias


_SEG_BLOCK = SUBLANES * SUBLANES


def _filter_kernel(z_ref, w1_ref, b1_ref, fr_ref, w2_ref, b2_ref, w3_ref, dl_ref,
                   hfg_ref, nrm_ref, *, tr, c):
    i = pl.program_id(0)
    z = z_ref[...]
    fr = fr_ref[...]
    d = functools.partial(jnp.dot, precision=HI, preferred_element_type=F32)
    h = jnp.sin(fr * (d(z, w1_ref[...]) + b1_ref[...]))
    h = jnp.sin(fr * (d(h, w2_ref[...]) + b2_ref[...]))
    h = jnp.dot(h.astype(BF16), w3_ref[...].astype(BF16), preferred_element_type=F32)
    window = jnp.exp(-z[:, 0:1] * dl_ref[...])
    hf = h[:, :c] * window
    row = lax.broadcasted_iota(jnp.int32, (tr, 1), 0) + i * tr
    g = jnp.where(row == 0, 0.0, h[:, c:] * window)
    hfg_ref[0] = hf
    hfg_ref[1] = g
    part = jnp.sum(jnp.abs(hf) + jnp.abs(g), axis=0, keepdims=True)

    @pl.when(i == 0)
    def _():
        nrm_ref[...] = part

    @pl.when(i != 0)
    def _():
        nrm_ref[...] = nrm_ref[...] + part


def _hyena_filter_taps(zfeat, w1, b1, freq, w2, b2, w3, deltas, tr=512):
    l, zp = zfeat.shape
    hp = w1.shape[1]
    c2 = w3.shape[1]
    c = c2 // 2
    full = lambda shape: pl.BlockSpec(shape, lambda i: (0,) * len(shape))
    return pl.pallas_call(
        functools.partial(_filter_kernel, tr=tr, c=c),
        grid=(l // tr,),
        in_specs=[pl.BlockSpec((tr, zp), lambda i: (i, 0)), full((zp, hp)), full((1, hp)),
                  full((1, hp)), full((hp, hp)), full((1, hp)), full((hp, c2)), full((1, c))],
        out_specs=[pl.BlockSpec((2, tr, c), lambda i: (0, i, 0)), full((1, c))],
        out_shape=[jax.ShapeDtypeStruct((2, l, c), F32), jax.ShapeDtypeStruct((1, c), F32)],
        compiler_params=_cparams("arbitrary"),
        name="hyena_filter_mlp",
    )(zfeat, w1, b1, freq, w2, b2, w3, deltas)


def _filter_spec_kernel(h_ref, f1_ref, g_ref, nrm_ref, hre_ref, him_ref, s_ref):
    _, nh, ng, sub, tc = h_ref.shape
    n2 = ng * sub
    nreal = s_ref.shape[1]
    k1 = nreal // 2 + 1
    d = functools.partial(jnp.dot, preferred_element_type=F32)
    f1 = f1_ref[...]
    for q in range(2):
        for g in range(ng):
            x = h_ref[q, :, g].reshape(nh * sub, tc).astype(BF16)
            s_ref[q, :, g] = d(f1, x).reshape(nreal, sub, tc)
    inv = 1.0 / nrm_ref[...]
    zero = jnp.zeros((n2, tc), F32)

    def spectrum(k, has_im):
        xs = []
        for q in range(2):
            re = s_ref[q, k].reshape(n2, tc)
            im = s_ref[q, k1 - 1 + k].reshape(n2, tc) if has_im else zero
            xs.append(d(g_ref[k], jnp.concatenate([re, im], axis=0).astype(BF16)))
        xf, xg = xs
        hre_ref[k] = (xf[:n2] + xg[:n2]) * inv
        him_ref[k] = (xf[n2:] - xg[n2:]) * inv

    spectrum(0, False)
    spectrum(k1 - 1, False)

    def body(k, carry):
        spectrum(k, True)
        return carry

    lax.fori_loop(1, k1 - 1, body, 0, unroll=_LEVEL2_UNROLL)


def _filter_spectrum(hfg, f1k, g, nrm, tc=128):
    _, l, c = hfg.shape
    nh, sub = DFT_N1 // 2, SUBLANES
    ng = DFT_N2 // sub
    k1, n2 = g.shape[0], DFT_N2
    return pl.pallas_call(
        _filter_spec_kernel,
        grid=(c // tc,),
        in_specs=[pl.BlockSpec((2, nh, ng, sub, tc), lambda j: (0, 0, 0, 0, j)),
                  _resident(f1k.shape, lambda j: (0, 0)), _resident(g.shape, lambda j: (0, 0, 0)),
                  pl.BlockSpec((1, tc), lambda j: (0, j))],
        out_specs=[pl.BlockSpec((k1, n2, tc), lambda j: (0, 0, j))] * 2,
        out_shape=[jax.ShapeDtypeStruct((k1, n2, c), F32)] * 2,
        scratch_shapes=[pltpu.VMEM((2, DFT_N1, ng, sub, tc), F32)],
        compiler_params=_cparams("arbitrary"),
        name="hyena_filter_spectrum",
    )(hfg.reshape(2, nh, ng, sub, c), f1k, g, nrm)


_LEVEL2_UNROLL = 16


def _fftconv_kernel(vf_ref, x0_ref, gt_ref, f1_ref, a1_ref, g_ref, gi_ref, hre_ref, him_ref, bias_ref,
                    o_ref, s_ref, y_ref):
    nh, ng, sub, tc = vf_ref.shape[1:]
    n2 = ng * sub
    nreal = s_ref.shape[0]
    k1 = nreal // 2 + 1
    d = functools.partial(jnp.dot, preferred_element_type=F32)

    f1 = f1_ref[...]
    for g in range(ng):
        x = vf_ref[0, :, g].reshape(nh * sub, tc).astype(BF16)
        s_ref[:, g] = d(f1, x).reshape(nreal, sub, tc)

    def level2(k, re, im):
        x = d(g_ref[k], jnp.concatenate([re, im], axis=0).astype(BF16))
        xr, xi = x[:n2], x[n2:]
        hr, hi = hre_ref[k], him_ref[k]
        y = jnp.concatenate([xr * hr - xi * hi, xr * hi + xi * hr], axis=0).astype(BF16)
        return d(gi_ref[k], y)

    zero = jnp.zeros((n2, tc), F32)
    for k in (0, k1 - 1):
        z = level2(k, s_ref[k].reshape(n2, tc), zero)
        s_ref[k] = z[:n2].reshape(ng, sub, tc)

    def body(k, carry):
        z = level2(k, s_ref[k].reshape(n2, tc), s_ref[k1 - 1 + k].reshape(n2, tc))
        s_ref[k] = z[:n2].reshape(ng, sub, tc)
        s_ref[k1 - 1 + k] = z[n2:].reshape(ng, sub, tc)
        return carry

    lax.fori_loop(1, k1 - 1, body, 0, unroll=_LEVEL2_UNROLL)

    a1 = a1_ref[...]
    for g in range(ng):
        z = s_ref[:, g].reshape(nreal * sub, tc).astype(BF16)
        y_ref[:, g] = d(a1, z).reshape(nh, sub, tc)

    bias = bias_ref[...]
    step = 2
    rows = step * n2

    def epilogue(i, carry):
        r = pl.multiple_of(i * rows, rows)
        y = y_ref[pl.ds(i * step, step)].reshape(rows, tc)
        vf = vf_ref[0, pl.ds(i * step, step)].reshape(rows, tc)
        x0 = x0_ref[0, pl.ds(r, rows), :].astype(F32)
        gt = gt_ref[0, pl.ds(r, rows), :].astype(F32)
        o_ref[0, pl.ds(r, rows), :] = ((x0 * (y + vf * bias)) * gt).astype(o_ref.dtype)
        return carry

    lax.fori_loop(0, nh // step, epilogue, 0)


def _hyena_fftconv(vf, x0c, gate, f1k, a1k, g, ginv, hre, him, bias, tc=256):
    b, l, c = vf.shape
    nh, sub = DFT_N1 // 2, SUBLANES
    ng = DFT_N2 // sub
    kb = g.shape[0]
    n2 = DFT_N2
    rows = pl.BlockSpec((1, l, tc), lambda j, i: (i, 0, j))
    return pl.pallas_call(
        _fftconv_kernel,
        grid=(c // tc, b),
        in_specs=[pl.BlockSpec((1, nh, ng, sub, tc), lambda j, i: (i, 0, 0, 0, j)), rows, rows,
                  _resident(f1k.shape, lambda j, i: (0, 0)), _resident(a1k.shape, lambda j, i: (0, 0)),
                  _resident(g.shape, lambda j, i: (0, 0, 0)), _resident(ginv.shape, lambda j, i: (0, 0, 0)),
                  _resident((kb, n2, tc), lambda j, i: (0, 0, j)), _resident((kb, n2, tc), lambda j, i: (0, 0, j)),
                  pl.BlockSpec((1, tc), lambda j, i: (0, j))],
        out_specs=rows,
        out_shape=jax.ShapeDtypeStruct((b, l, c), BF16),
        scratch_shapes=[pltpu.VMEM((DFT_N1, ng, sub, tc), F32), pltpu.VMEM((nh, ng, sub, tc), F32)],
        compiler_params=_cparams("arbitrary", "arbitrary"),
        name="hyena_fftconv",
    )(vf.reshape(b, nh, ng, sub, c), x0c, gate, f1k, a1k, g, ginv, hre, him, bias)


@functools.lru_cache(maxsize=None)
def _dft_tables():
    n1, n2, k1 = DFT_N1, DFT_N2, DFT_K1
    n = n1 * n2
    nh = n1 // 2
    ang = 2.0 * np.pi * np.arange(k1)[:, None] * np.arange(nh)[None, :] / n1
    fre, fim = np.cos(ang), -np.sin(ang)
    m = np.arange(n2)[None, None, :]
    k2 = np.arange(n2)[None, :, None]
    ka = np.arange(k1)[:, None, None]
    ph = -2.0 * np.pi * m * (ka / n + k2 / n2)
    gr, gi = np.cos(ph), np.sin(ph)
    g = np.concatenate([np.concatenate([gr, -gi], axis=2), np.concatenate([gi, gr], axis=2)], axis=1)
    grt, git = np.swapaxes(gr, 1, 2), np.swapaxes(gi, 1, 2)
    ginv = np.concatenate([np.concatenate([grt, git], axis=2), np.concatenate([-git, grt], axis=2)], axis=1)
    wk = np.where((np.arange(k1) == 0) | (np.arange(k1) == n1 // 2), 1.0, 2.0)[None, :]
    are, aim = wk * fre.T / n, wk * fim.T / n
    eye = np.eye(SUBLANES)
    f1k = np.kron(np.concatenate([fre, fim[1:k1 - 1]], axis=0), eye)
    a1k = np.kron(np.concatenate([are, aim[:, 1:k1 - 1]], axis=1), eye)
    return tuple(np.asarray(a, np.float32).astype(BF16) for a in (g, ginv, f1k, a1k))


def _position_features(l):
    f32 = F32
    pos = jnp.arange(l, dtype=f32)[:, None]
    t = pos / max(l - 1, 1)
    bands = (HY_EMB - 1) // 2
    f = jnp.linspace(1e-4, bands - 1, bands, dtype=f32)[None, :]
    ang = f * pos * (2.0 * math.pi / l)
    z = jnp.concatenate([t, jnp.cos(ang), -jnp.sin(ang)], axis=-1)
    return jnp.pad(z, ((0, 0), (0, LANES - HY_EMB)))


def _pad2(a, rows, cols):
    return jnp.pad(a, ((0, rows - a.shape[0]), (0, cols - a.shape[1])))


def _hyena_spectrum(l, c, fw1, fb1, fw2, fb2, fw3, freq):
    g, _, f1k, _ = _dft_tables()
    hp = LANES
    zfeat = _position_features(l)
    w1 = _pad2(fw1, LANES, hp)
    w2 = _pad2(fw2, hp, hp)
    w3 = _pad2(fw3, hp, fw3.shape[1])
    row = lambda v: _pad2(v[None, :], 1, hp)
    deltas = jnp.abs(jnp.linspace(math.log(HY_DECAY_TARGET) / HY_LONG_DECAY_PCT,
                                  math.log(HY_DECAY_TARGET) / HY_SHORT_DECAY_PCT, c, dtype=F32))[None, :]
    hfg, nrm = _hyena_filter_taps(zfeat, w1, row(fb1), row(freq), w2, row(fb2), w3, deltas)
    return _filter_spectrum(hfg, jnp.asarray(f1k), jnp.asarray(g), nrm)


def _hyena_branch(x0c, vf, hy_gate, hre, him, bias):
    g, ginv, f1k, a1k = (jnp.asarray(t) for t in _dft_tables())
    return _hyena_fftconv(vf, x0c, hy_gate, f1k, a1k, g, ginv, hre, him, bias[None, :])


def _ssd_kernel(xs_ref, bm_ref, cm_ref, dt_ref, dtb_ref, alog_ref, rep_ref, dsk_ref, mask_ref,
                *rest, reverse, lane0, chunks):
    yo_ref = rest[0] if len(rest) == 3 else None
    y_ref, prev_ref = rest[-2:]
    q, n, hd = SSD_CHUNK, SSD_STATE, SSD_HEAD_DIM
    gw = (SSD_HEADS // SSD_GROUPS) * hd

    @pl.when(pl.program_id(1) == 0)
    def _():
        prev_ref[...] = jnp.zeros_like(prev_ref)

    mask = mask_ref[...]
    maskb = mask > 0
    neg_a = -jnp.exp(alog_ref[...])
    lo_lanes = lax.broadcasted_iota(jnp.int32, (1, LANES), 1) < hd
    d = functools.partial(jnp.dot, preferred_element_type=F32)
    npair = gw // LANES
    edge = 0 if reverse else q - 1

    def per_pair(v, c0, shape):
        b0 = jnp.broadcast_to(v[:, c0:c0 + 1], shape)
        b1 = jnp.broadcast_to(v[:, c0 + 1:c0 + 2], shape)
        return jnp.where(lo_lanes, b0, b1)

    order = range(chunks - 1, -1, -1) if reverse else range(chunks)
    for k in order:
        rows = pl.ds(k * q, q)
        xs_b = xs_ref[0, rows, :]
        bm = bm_ref[0, rows, :]
        cm = cm_ref[0, rows, :]
        dt = _softplus(dt_ref[0, rows, :] + dtb_ref[...])
        cum = _sel_dot(mask, dt * neg_a)
        src_t = (cum - jnp.log(dt)).T
        cum_e = cum[edge:edge + 1, :]
        e_in = jnp.exp(cum)
        w_out = dt * jnp.exp(cum_e - cum)
        e_all = jnp.exp(cum_e)
        spread = d(jnp.concatenate([e_in, w_out], axis=0).astype(BF16), rep_ref[...])
        e_x = spread[:q]
        xsd_all = xs_b * spread[q:].astype(BF16)
        ys = []
        for g in range(SSD_GROUPS):
            cg = cm[:, g * n:(g + 1) * n].astype(BF16)
            bg = bm[:, g * n:(g + 1) * n].astype(BF16)
            cb = lax.dot_general(cg, bg, (((1,), (1,)), ((), ())), preferred_element_type=F32)
            sl = slice(g * gw, (g + 1) * gw)
            prev_g = prev_ref[:, sl]
            y_off = d(cg, prev_g.astype(BF16))
            cds = []
            for pair in range(npair):
                lo = g * gw + pair * LANES
                c0 = lane0 + lo // hd
                ms = []
                for c in (c0, c0 + 1):
                    seg_dt = jnp.exp(jnp.where(maskb, cum[:, c:c + 1] - src_t[c:c + 1, :], -jnp.inf))
                    ms.append((cb * seg_dt).astype(BF16))
                xp = xs_b[:, lo:lo + LANES]
                zero = jnp.zeros_like(xp)
                rhs = jnp.concatenate([jnp.where(lo_lanes, xp, zero), jnp.where(lo_lanes, zero, xp)], axis=0)
                yd = d(jnp.concatenate(ms, axis=1), rhs)
                ys.append(yd + y_off[:, pair * LANES:(pair + 1) * LANES] * e_x[:, lo:lo + LANES])
                cds.append(per_pair(e_all, c0, (n, LANES)))
            st = lax.dot_general(bg, xsd_all[:, sl], (((0,), (0,)), ((), ())), preferred_element_type=F32)
            prev_ref[:, sl] = prev_g * jnp.concatenate(cds, axis=1) + st
        y = jnp.concatenate(ys, axis=1)
        if not reverse:
            y = y + xs_b.astype(F32) * dsk_ref[...]
        if yo_ref is not None:
            y = y + yo_ref[0, rows, :].astype(F32)
        y_ref[0, rows, :] = y.astype(y_ref.dtype)


def _ssd_direction(xbc, dt_raw, dtb, alog, rep, dskip, mask, y_other=None, *, reverse, ts=1024):
    b, l, _ = xbc.shape
    hp = SSD_HEADS * SSD_HEAD_DIM
    gn = SSD_GROUPS * SSD_STATE
    nt = l // ts
    tmap = (lambda j: nt - 1 - j) if reverse else (lambda j: j)
    full = lambda shape: pl.BlockSpec(shape, lambda i, j: (0,) * len(shape))
    yspec = pl.BlockSpec((1, ts, hp), lambda i, j: (i, tmap(j), 0))
    extra = [] if y_other is None else [y_other]
    return pl.pallas_call(
        functools.partial(_ssd_kernel, reverse=reverse, lane0=SSD_HEADS if reverse else 0,
                          chunks=ts // SSD_CHUNK),
        grid=(b, nt),
        in_specs=[yspec,
                  pl.BlockSpec((1, ts, gn), lambda i, j: (i, tmap(j), hp // gn)),
                  pl.BlockSpec((1, ts, gn), lambda i, j: (i, tmap(j), hp // gn + 1)),
                  pl.BlockSpec((1, ts, LANES), lambda i, j: (i, tmap(j), 0)),
                  full((1, LANES)), full((1, LANES)), full((LANES, hp)), full((1, hp)),
                  full((SSD_CHUNK, SSD_CHUNK))] + [yspec] * len(extra),
        out_specs=yspec,
        out_shape=jax.ShapeDtypeStruct((b, l, hp), BF16),
        scratch_shapes=[pltpu.VMEM((SSD_STATE, hp), F32)],
        compiler_params=_cparams("parallel", "arbitrary"),
        name="ssd_bwd" if reverse else "ssd_fwd",
    )(xbc, xbc, xbc, dt_raw, dtb, alog, rep, dskip, mask, *extra)


def _ssd_branch(xc, dt_raw, dt_bias, a_log, d_skip):
    pad_row = lambda v: _pad2(v.reshape(1, -1), 1, LANES)
    dtb, alog = pad_row(dt_bias), pad_row(a_log)
    dskip = jnp.repeat(d_skip, SSD_HEAD_DIM)[None, :]
    lane_head = np.arange(SSD_HEADS * SSD_HEAD_DIM)[None, :] // SSD_HEAD_DIM
    src = np.arange(LANES)[:, None]
    idx = np.arange(SSD_CHUNK)
    y = None
    for reverse in (False, True):
        rep = jnp.asarray(src == lane_head + (SSD_HEADS if reverse else 0), dtype=BF16)
        allowed = (idx[None, :] >= idx[:, None]) if reverse else (idx[None, :] <= idx[:, None])
        mask = jnp.asarray(allowed, dtype=BF16)
        y = _ssd_direction(xc, dt_raw, dtb, alog, rep, dskip, mask, y, reverse=reverse)
    return y


_LRU_ROWS = 256


def _lru_kernel(*refs, l, heads):
    x_refs, (gt_ref, wa_ref, wx_ref, ba_ref, bx_ref, lam_ref, o_ref), seq_refs = (
        refs[:heads], refs[heads:heads + 7], refs[heads + 7:])
    blk = _SEG_BLOCK
    d = functools.partial(jnp.dot, preferred_element_type=F32)

    def rows_segment_order(hd, r):
        x = x_refs[hd].at[0]
        return jnp.concatenate([x[pl.ds(r + base + j, SUBLANES, stride=SUBLANES), :]
                                for base in range(0, _CONV_ROWS, blk) for j in range(SUBLANES)], axis=0)
    e0 = (-0.5 * LRU_C / math.log(2.0)) * _softplus(-lam_ref[...])
    ba, bx = ba_ref[...], bx_ref[...]
    a_refs, u_refs, h_refs = ([[seq_refs[(q * 2 + dr) * heads + hd] for hd in range(heads)] for dr in range(2)]
                              for q in range(3))

    def gates(i, carry):
        r = pl.multiple_of(i * _CONV_ROWS, _CONV_ROWS)
        for hd in range(heads):
            ls = slice(hd * LRU_BLOCK, (hd + 1) * LRU_BLOCK)
            xh = rows_segment_order(hd, r)
            xb = xh.astype(BF16)
            xh2 = 0.5 * xh
            for dr in range(2):
                tr = jnp.tanh(d(xb, wa_ref[dr, hd]) + ba[dr:dr + 1, ls])
                ti = jnp.tanh(d(xb, wx_ref[dr, hd]) + bx[dr:dr + 1, ls])
                e = e0[dr:dr + 1, ls]
                a = jnp.exp2(e + e * tr)
                s = 1.0 - a * a
                root = jnp.where(s > 0.0, s * lax.rsqrt(s), 0.0)
                u = (xh2 + xh2 * ti) * root
                a_refs[dr][hd][pl.ds(r, _CONV_ROWS), :] = a
                u_refs[dr][hd][pl.ds(r, _CONV_ROWS), :] = u
        return carry

    lax.fori_loop(0, l // _CONV_ROWS, gates, 0, unroll=4)

    row = lax.broadcasted_iota(jnp.int32, (SUBLANES, LRU_BLOCK), 0)
    nblk = l // blk

    def sublane_scan(a, u, reverse):
        for s in (1, 2, 4):
            sh = (SUBLANES - s) if reverse else s
            valid = (row < SUBLANES - s) if reverse else (row >= s)
            a_sh = jnp.where(valid, pltpu.roll(a, sh, axis=0), 1.0)
            u_sh = jnp.where(valid, pltpu.roll(u, sh, axis=0), 0.0)
            u = u + a * u_sh
            a = a * a_sh
        return a, u

    def block_scan(a_ref, u_ref, h_ref, r, h_in, reverse):
        order = range(SUBLANES - 1, -1, -1) if reverse else range(SUBLANES)
        hs, ps = {}, {}
        h = p = None
        for k in order:
            rows = pl.ds(r + SUBLANES * k, SUBLANES)
            a, u = a_ref[rows, :], u_ref[rows, :]
            h = u if h is None else a * h + u
            p = a if p is None else a * p
            hs[k], ps[k] = h, p
        sp, se = sublane_scan(p, h, reverse)
        full = se + sp * h_in
        edge = 0 if reverse else SUBLANES - 1
        h_out = jnp.broadcast_to(full[edge:edge + 1, :], (SUBLANES, LRU_BLOCK))
        if reverse:
            carry = jnp.where(row == SUBLANES - 1, h_in, pltpu.roll(full, SUBLANES - 1, axis=0))
        else:
            carry = jnp.where(row == 0, h_in, pltpu.roll(full, 1, axis=0))
        for k in order:
            h_ref[pl.ds(r + k, SUBLANES, stride=SUBLANES), :] = hs[k] + ps[k] * carry
        return h_out

    def scan(i, carry):
        rf = pl.multiple_of(i * blk, blk)
        rb = pl.multiple_of((nblk - 1 - i) * blk, blk)
        out = []
        for hd in range(heads):
            hf, hb = carry[2 * hd], carry[2 * hd + 1]
            out.append(block_scan(a_refs[0][hd], u_refs[0][hd], h_refs[0][hd], rf, hf, False))
            out.append(block_scan(a_refs[1][hd], u_refs[1][hd], h_refs[1][hd], rb, hb, True))
        return tuple(out)

    zero = jnp.zeros((SUBLANES, LRU_BLOCK), F32)
    lax.fori_loop(0, nblk, scan, (zero,) * (2 * heads), unroll=4)

    def combine(i, carry):
        r = pl.multiple_of(i * _LRU_ROWS, _LRU_ROWS)
        rows = pl.ds(r, _LRU_ROWS)
        h = jnp.concatenate([h_refs[0][hd][rows, :] + h_refs[1][hd][rows, :] for hd in range(heads)], axis=1)
        o_ref[0, rows, :] = (h * gt_ref[0, rows, :].astype(F32)).astype(o_ref.dtype)
        return carry

    lax.fori_loop(0, l // _LRU_ROWS, combine, 0)


def _rglru_mixer(xc, gate, w_a, w_x, b_a, b_x, lam, heads=2):
    b, l, w = xc.shape
    tw = heads * LRU_BLOCK
    gspec = pl.BlockSpec((1, l, tw), lambda i, j: (i, 0, j))
    xspecs = [pl.BlockSpec((1, l, LRU_BLOCK), lambda i, j, hd=hd: (i, 0, j * heads + hd)) for hd in range(heads)]
    wspec = pl.BlockSpec((2, heads, LRU_BLOCK, LRU_BLOCK), lambda i, j: (0, j, 0, 0))
    vspec = pl.BlockSpec((2, tw), lambda i, j: (0, j))
    seqs = [pltpu.VMEM((l, LRU_BLOCK), F32)] * (3 * 2 * heads)
    return pl.pallas_call(
        functools.partial(_lru_kernel, l=l, heads=heads),
        grid=(b, w // tw),
        in_specs=xspecs + [gspec, wspec, wspec, vspec, vspec, vspec],
        out_specs=gspec,
        out_shape=jax.ShapeDtypeStruct((b, l, w), BF16),
        scratch_shapes=seqs,
        compiler_params=_cparams("parallel", "parallel"),
        name="rglru",
    )(*([xc] * heads), gate, (0.5 * w_a).astype(BF16), (0.5 * w_x).astype(BF16), 0.5 * b_a, 0.5 * b_x, lam)


def _rms(x, g, eps):
    return (x * lax.rsqrt(jnp.mean(x * x, axis=-1, keepdims=True) + eps)) * g


def _norm_modulate(x, g, shift, scale):
    ms = jnp.mean(x * x, axis=-1, keepdims=True)
    return ((x * lax.rsqrt(ms + 1e-6)) * g) * (1.0 + scale) + shift


def _outproj_even_kernel(*refs, n_x, bp):
    x_refs, (ya_ref, ys_ref, z_ref, gm_ref, ng_ref, wa_ref, wb_ref, o_ref) = refs[:n_x], refs[n_x:]
    yb = ys_ref[0].astype(F32) * z_ref[0].astype(F32)
    ng = ng_ref[...]
    gw = yb.shape[1] // SSD_GROUPS
    parts = [_rms(yb[:, g * gw:(g + 1) * gw], ng[:, g * gw:(g + 1) * gw], 1e-5) for g in range(SSD_GROUPS)]
    ybn = jnp.concatenate(parts, axis=1)
    d = functools.partial(jnp.dot, preferred_element_type=F32)
    acc = d(ya_ref[0].astype(BF16), wa_ref[...]) + d(ybn.astype(BF16), wb_ref[...])
    o_ref[0] = _load_rows(x_refs, bp) + gm_ref[0] * acc


def _out_projection_even(x, ya, ys, z, gate_mod, norm_g, w_out, tl=512):
    b, l, c = ya.shape
    dm = x[0].shape[2]
    rows = lambda n: pl.BlockSpec((1, tl, n), lambda i, j: (i, j, 0))
    wa, wb = w_out[:c].astype(BF16), w_out[c:].astype(BF16)
    return pl.pallas_call(
        functools.partial(_outproj_even_kernel, n_x=len(x), bp=x[0].shape[0]),
        grid=(b, l // tl),
        in_specs=_row_specs(x, tl, l // tl) + [rows(c), rows(c), rows(c),
                                      pl.BlockSpec((1, 1, dm), lambda i, j: (i, 0, 0)),
                                      pl.BlockSpec((1, c), lambda i, j: (0, 0)),
                                      _resident(wa.shape, lambda i, j: (0, 0)),
                                      _resident(wb.shape, lambda i, j: (0, 0))],
        out_specs=rows(dm),
        out_shape=jax.ShapeDtypeStruct((b, l, dm), F32),
        compiler_params=_cparams("arbitrary", "arbitrary"),
        name="out_proj_even",
    )(*x, ya, ys, z, gate_mod, norm_g, wa, wb)


def _outproj_odd_kernel(x_ref, y_ref, gm_ref, w_ref, fg_ref, *o_refs, bp):
    acc = jnp.dot(y_ref[0].astype(BF16), w_ref[...], preferred_element_type=F32)
    xn = x_ref[0] + gm_ref[0] * acc
    if len(o_refs) == 1:
        o_refs[0][0] = xn
        return
    out = _rms(xn, fg_ref[...], 1e-6)

    @pl.when(pl.program_id(0) < bp)
    def _():
        o_refs[0][0] = out

    @pl.when(pl.program_id(0) >= bp)
    def _():
        o_refs[1][0] = out


def _out_projection_odd(x, y, gate_mod, w_out, final_g, split_at=None, tl=512):
    b, l, dm = x.shape
    w = y.shape[2]
    rows = lambda n: pl.BlockSpec((1, tl, n), lambda i, j: (i, j, 0))
    wb = w_out.astype(BF16)
    if split_at is None:
        out_specs = [rows(dm)]
        out_shape = [jax.ShapeDtypeStruct((b, l, dm), F32)]
    else:
        out_specs = [pl.BlockSpec((1, tl, dm), m) for m in _pair_row_maps(split_at, l // tl)]
        out_shape = [jax.ShapeDtypeStruct((n, l, dm), F32) for n in (split_at, b - split_at)]
    out = pl.pallas_call(
        functools.partial(_outproj_odd_kernel, bp=split_at),
        grid=(b, l // tl),
        in_specs=[rows(dm), rows(w),
                  pl.BlockSpec((1, 1, dm), lambda i, j: (i, 0, 0)),
                  _resident(wb.shape, lambda i, j: (0, 0)),
                  pl.BlockSpec((1, dm), lambda i, j: (0, 0))],
        out_specs=out_specs,
        out_shape=out_shape,
        compiler_params=_cparams("arbitrary", "arbitrary"),
        name="out_proj_odd",
    )(x, y, gate_mod, wb, final_g)
    return out[0] if split_at is None else tuple(out)


def kernel(x_prompt, x_sample, c_prompt, c_sample, mod_w, mod_b, norm_g, final_g, ev_w_in, ev_w_out, hy_conv_w, hy_conv_b, hy_fw1, hy_fb1, hy_fw2, hy_fb2, hy_fw3, hy_freq, hy_bias, ssd_conv_w, ssd_conv_b, ssd_dt_bias, ssd_A_log, ssd_D, ssd_norm_g, od_w_in, od_w_out, lru_conv_w, lru_conv_b, lru_w_a, lru_b_a, lru_w_x, lru_b_x, lru_lam):
    bp, l, dm = x_prompt.shape
    assert x_sample.shape[1:] == (l, dm), "the two request groups are stacked on the batch axis"
    x = (x_prompt, x_sample)
    c = jnp.concatenate([c_prompt, c_sample], axis=0)
    b = c.shape[0]
    depth = mod_w.shape[0]
    assert 2 * l == DFT_N1 * DFT_N2
    assert depth % 2 == 0, "the final norm is fused into the last (odd) layer's output projection"

    b_pad = -(-b // SUBLANES) * SUBLANES
    mods = _modulation(jnp.pad(c, ((0, b_pad - b), (0, 0))), mod_w, mod_b)[:, :b]
    shift, scale, gate_mod = (mods[:, :, k * dm:(k + 1) * dm].reshape(depth, b, 1, dm) for k in range(3))

    hyc = hy_bias.shape[1]
    ssw = ssd_norm_g.shape[1]
    xbw = ssd_conv_w.shape[2]
    o1, o2, o3, o4 = 3 * hyc, 4 * hyc, 4 * hyc + ssw, 4 * hyc + ssw + xbw

    for i in range(depth):
        j = i // 2
        g = norm_g[i][None, :]
        if i % 2 == 0:
            w_in = ev_w_in[j].astype(BF16)
            w_dt = jnp.pad(w_in[:, o4:], ((0, 0), (0, LANES - (w_in.shape[1] - o4))))
            x0c, vf, hyg, z, xc, dt_raw = _in_projection(x, shift[i], scale[i], g, [
                dict(w=w_in[:, :o1], dtype=(BF16, F32), post="hyena", conv_w=hy_conv_w[j],
                     conv_b=hy_conv_b[j][None, :]),
                dict(w=w_in[:, o1:o2], dtype=BF16, post="silu"),
                dict(w=w_in[:, o2:o3], dtype=BF16, post="silu"),
                dict(w=w_in[:, o3:o4], dtype=BF16, post="conv_silu", conv_w=ssd_conv_w[j],
                     conv_b=ssd_conv_b[j][None, :]),
                dict(w=w_dt, dtype=F32)])
            hre, him = _hyena_spectrum(l, hyc, hy_fw1[j], hy_fb1[j], hy_fw2[j], hy_fb2[j], hy_fw3[j], hy_freq[j])
            ya = _hyena_branch(x0c, vf, hyg, hre, him, hy_bias[j])
            ys = _ssd_branch(xc, dt_raw, ssd_dt_bias[j], ssd_A_log[j], ssd_D[j])
            x = (_out_projection_even(x, ya, ys, z, gate_mod[i], ssd_norm_g[j][None, :], ev_w_out[j]),)
        else:
            w_in = od_w_in[j].astype(BF16)
            half = w_in.shape[1] // 2
            xc, gt = _in_projection(x, shift[i], scale[i], g, [
                dict(w=w_in[:, :half], dtype=F32, post="conv", conv_w=lru_conv_w[j], conv_b=lru_conv_b[j][None, :]),
                dict(w=w_in[:, half:], dtype=BF16, post="silu")])
            y = _rglru_mixer(xc, gt, lru_w_a[j], lru_w_x[j], lru_b_a[j], lru_b_x[j], lru_lam[j])
            if i == depth - 1:
                return _out_projection_odd(x[0], y, gate_mod[i], od_w_out[j], final_g[None, :], split_at=bp)
            x = (_out_projection_odd(x[0], y, gate_mod[i], od_w_out[j], final_g[None, :]),)
```

```python
import functools
import math

import numpy as np
import jax
import jax.numpy as jnp
from jax import lax
from jax.experimental import pallas as pl
from jax.experimental.pallas import tpu as pltpu

F32 = jnp.float32
BF16 = jnp.bfloat16
HI = lax.Precision.HIGHEST

V7X_VMEM_BYTES = 64 * 1024 * 1024
VMEM_LIMIT = V7X_VMEM_BYTES - 8 * 1024 * 1024
LANES = 128
SUBLANES = 8

HY_EMB = 33
HY_SHORT_DECAY_PCT = 0.3
HY_LONG_DECAY_PCT = 1.5
HY_DECAY_TARGET = 1e-2
SSD_HEADS = 16
SSD_HEAD_DIM = 64
SSD_GROUPS = 4
SSD_STATE = 128
SSD_CHUNK = 128
LRU_HEADS = 16
LRU_BLOCK = 128
LRU_C = 8.0

DFT_N1 = 64
DFT_N2 = 128
DFT_K1 = DFT_N1 // 2 + 1


def _cparams(*sem):
    return pltpu.CompilerParams(dimension_semantics=sem, vmem_limit_bytes=VMEM_LIMIT)


def _resident(shape, index_map):
    return pl.BlockSpec(shape, index_map, pipeline_mode=pl.Buffered(1))


def _sigmoid(x):
    return 0.5 + 0.5 * jnp.tanh(0.5 * x)


def _silu(x):
    return x * _sigmoid(x)


def _softplus(x):
    return jnp.maximum(x, 0.0) + jnp.log1p(jnp.exp(-jnp.abs(x)))


def _split3(x):
    hi = x.astype(BF16)
    r1 = x - hi.astype(F32)
    mid = r1.astype(BF16)
    lo = (r1 - mid.astype(F32)).astype(BF16)
    return hi, mid, lo


def _sel_dot(sel, x):
    hi, mid, lo = _split3(x)
    d = functools.partial(jnp.dot, preferred_element_type=F32)
    return d(sel, hi) + d(sel, mid) + d(sel, lo)


def _mod_kernel(c_ref, w_ref, b_ref, o_ref):
    cs = _silu(c_ref[...])
    o_ref[0] = jnp.dot(cs, w_ref[0], precision=HI, preferred_element_type=F32) + b_ref[0]


def _modulation(c_pad, mod_w, mod_b):
    depth, d, d3 = mod_w.shape
    bp = c_pad.shape[0]
    return pl.pallas_call(
        _mod_kernel,
        grid=(depth, d3 // d),
        in_specs=[
            pl.BlockSpec((bp, d), lambda i, j: (0, 0)),
            pl.BlockSpec((1, d, d), lambda i, j: (i, 0, j)),
            pl.BlockSpec((1, 1, d), lambda i, j: (i, 0, j)),
        ],
        out_specs=pl.BlockSpec((1, bp, d), lambda i, j: (i, 0, j)),
        out_shape=jax.ShapeDtypeStruct((depth, bp, d3), F32),
        compiler_params=_cparams("arbitrary", "arbitrary"),
        name="adaln_mod",
    )(c_pad, mod_w, mod_b.reshape(depth, 1, d3))


def _pair_row_maps(bp, nj):
    first = lambda i, j: (jnp.minimum(i, bp - 1), jnp.where(i < bp, j, nj - 1), 0)
    second = lambda i, j: (jnp.maximum(i - bp, 0), jnp.where(i < bp, 0, j), 0)
    return first, second


def _load_rows(x_refs, bp):
    if len(x_refs) == 1:
        return x_refs[0][0]
    return jnp.where(pl.program_id(0) < bp, x_refs[0][0], x_refs[1][0])


def _row_specs(xs, rows, nj, f=lambda j: j):
    d = xs[0].shape[2]
    if len(xs) == 1:
        return [pl.BlockSpec((1, rows, d), lambda i, j: (i, f(j), 0))]
    bp = xs[0].shape[0]
    first = lambda i, j: (jnp.minimum(i, bp - 1), jnp.where(i < bp, f(j), f(nj - 1)), 0)
    second = lambda i, j: (jnp.maximum(i - bp, 0), jnp.where(i < bp, f(0), f(j)), 0)
    return [pl.BlockSpec((1, rows, d), first), pl.BlockSpec((1, rows, d), second)]


_PROJ_GROUP = 2 * LANES


_CONV_POSTS = ("conv", "conv_silu", "hyena")


def _inproj_kernel(*refs, n_x, bp, plan, tl):
    refs = list(refs)
    take = lambda n: [refs.pop(0) for _ in range(n)]
    has_conv = any(post in _CONV_POSTS for post, _, _ in plan)
    x_main = take(n_x)
    x_prev, x_next = (take(n_x), take(n_x)) if has_conv else (None, None)
    sh_ref, sc_ref, g_ref = take(3)
    piece_refs = [take(3 if post in _CONV_POSTS else 1) for post, _, _ in plan]
    out_refs = [take(n_out) for _, _, n_out in plan]
    pad_ref = refs.pop(0) if has_conv else None

    j, last = pl.program_id(1), pl.num_programs(1) - 1
    norm = lambda x: _norm_modulate(x, g_ref[...], sh_ref[0], sc_ref[0])
    d = functools.partial(jnp.dot, preferred_element_type=F32)
    hn = norm(_load_rows(x_main, bp))
    hb = hn.astype(BF16)
    if has_conv:
        hb_ext = jnp.concatenate([norm(_load_rows(x_prev, bp)), hn, norm(_load_rows(x_next, bp))],
                                 axis=0).astype(BF16)
    slabs = _PROJ_GROUP // LANES

    def project_to_pad(w_ref, col, pad):
        u = d(hb_ext, w_ref[:, col:col + _PROJ_GROUP])
        for s in range(slabs):
            ls = slice(s * LANES, (s + 1) * LANES)
            pad[s, pl.ds(0, _HALO), :] = jnp.where(j > 0, u[:_HALO, ls], 0.0)
            pad[s, pl.ds(_HALO, tl), :] = u[_HALO:_HALO + tl, ls]
            pad[s, pl.ds(_HALO + tl, _HALO), :] = jnp.where(j < last, u[_HALO + tl:, ls], 0.0)

    def conv_from_pad(pad, cw_ref, cb_ref, col, taps, r):
        parts = [_conv_rows(pad.at[s], r, cw_ref[:, col + s * LANES:col + (s + 1) * LANES],
                            cb_ref[:, col + s * LANES:col + (s + 1) * LANES], taps, taps // 2)
                 for s in range(slabs)]
        return jnp.concatenate(parts, axis=1)

    group = 0
    for (post, taps, _), p_refs, o_refs in zip(plan, piece_refs, out_refs):
        w_ref = p_refs[0]
        if post not in _CONV_POSTS:
            u = d(hb, w_ref[...])
            o_refs[0][0] = (_silu(u) if post == "silu" else u).astype(o_refs[0].dtype)
            continue
        cw_ref, cb_ref = p_refs[1:]
        parts = 3 if post == "hyena" else 1
        width = w_ref.shape[1] // parts
        for c0 in range(0, width, _PROJ_GROUP):
            pads = pad_ref.at[group % 2]
            group += 1
            for o in range(parts):
                project_to_pad(w_ref, o * width + c0, pads.at[o])
            for r in range(0, tl, _CONV_ROWS):
                rows, cols = pl.ds(r, _CONV_ROWS), slice(c0, c0 + _PROJ_GROUP)
                cv = [conv_from_pad(pads.at[o], cw_ref, cb_ref, o * width + c0, taps, r) for o in range(parts)]
                if post == "hyena":
                    o_refs[0][0, rows, cols] = cv[0].astype(o_refs[0].dtype)
                    o_refs[1][0, rows, cols] = (cv[2] * cv[1]).astype(o_refs[1].dtype)
                else:
                    y = _silu(cv[0]) if post == "conv_silu" else cv[0]
                    o_refs[0][0, rows, cols] = y.astype(o_refs[0].dtype)


def _in_projection(x, shift, scale, g, pieces, tl=512):
    b = sum(a.shape[0] for a in x)
    l, d = x[0].shape[1:]
    nj = l // tl
    is_conv = lambda p: p.get("post") in _CONV_POSTS
    outs = []
    plan = []
    for p in pieces:
        n = p["w"].shape[1]
        if p.get("post") == "hyena":
            outs += [(n // 3, p["dtype"][0]), (n // 3, p["dtype"][1])]
        else:
            outs.append((n, p["dtype"]))
        plan.append((p.get("post"), p["conv_w"].shape[0] if is_conv(p) else 0, 2 if p.get("post") == "hyena" else 1))
    has_conv = any(is_conv(p) for p in pieces)
    const = lambda a: _resident(a.shape, lambda i, j: (0,) * a.ndim)
    in_specs, args = _row_specs(x, tl, nj), list(x)
    if has_conv:
        per = tl // _HALO
        in_specs += _row_specs(x, _HALO, nj, lambda j: jnp.maximum(j * per - 1, 0))
        in_specs += _row_specs(x, _HALO, nj, lambda j: jnp.minimum((j + 1) * per, l // _HALO - 1))
        args += list(x) * 2
    in_specs += [pl.BlockSpec((1, 1, d), lambda i, j: (i, 0, 0)), pl.BlockSpec((1, 1, d), lambda i, j: (i, 0, 0)),
                 pl.BlockSpec((1, d), lambda i, j: (0, 0))]
    args += [shift, scale, g]
    for p in pieces:
        for a in [p["w"]] + ([p["conv_w"], p["conv_b"]] if is_conv(p) else []):
            in_specs.append(const(a))
            args.append(a)
    scratch = [pltpu.VMEM((2, 3, _PROJ_GROUP // LANES, tl + 2 * _HALO, LANES), F32)] if has_conv else []
    return pl.pallas_call(
        functools.partial(_inproj_kernel, n_x=len(x), bp=x[0].shape[0], plan=tuple(plan), tl=tl),
        grid=(b, nj),
        in_specs=in_specs,
        out_specs=[pl.BlockSpec((1, tl, n), lambda i, j: (i, j, 0)) for n, _ in outs],
        out_shape=[jax.ShapeDtypeStruct((b, l, n), dt) for n, dt in outs],
        scratch_shapes=scratch,
        compiler_params=_cparams("arbitrary", "arbitrary"),
        name="in_proj",
    )(*args)


_CONV_ROWS = 256
_HALO = SUBLANES


def _conv_rows(pad_ref, r, w, bias, taps, left):
    acc = None
    for k in range(taps):
        term = w[k:k + 1, :] * pad_ref[pl.ds(r + _HALO + k - left, _CONV_ROWS), :]
        acc = term if acc is None else acc + term
    return acc + bias


_SEG_BLOCK = SUBLANES * SUBLANES


def _filter_kernel(z_ref, w1_ref, b1_ref, fr_ref, w2_ref, b2_ref, w3_ref, dl_ref,
                   hfg_ref, nrm_ref, *, tr, c):
    i = pl.program_id(0)
    z = z_ref[...]
    fr = fr_ref[...]
    d = functools.partial(jnp.dot, precision=HI, preferred_element_type=F32)
    h = jnp.sin(fr * (d(z, w1_ref[...]) + b1_ref[...]))
    h = jnp.sin(fr * (d(h, w2_ref[...]) + b2_ref[...]))
    h = jnp.dot(h.astype(BF16), w3_ref[...].astype(BF16), preferred_element_type=F32)
    window = jnp.exp(-z[:, 0:1] * dl_ref[...])
    hf = h[:, :c] * window
    row = lax.broadcasted_iota(jnp.int32, (tr, 1), 0) + i * tr
    g = jnp.where(row == 0, 0.0, h[:, c:] * window)
    hfg_ref[0] = hf
    hfg_ref[1] = g
    part = jnp.sum(jnp.abs(hf) + jnp.abs(g), axis=0, keepdims=True)

    @pl.when(i == 0)
    def _():
        nrm_ref[...] = part

    @pl.when(i != 0)
    def _():
        nrm_ref[...] = nrm_ref[...] + part


def _hyena_filter_taps(zfeat, w1, b1, freq, w2, b2, w3, deltas, tr=512):
    l, zp = zfeat.shape
    hp = w1.shape[1]
    c2 = w3.shape[1]
    c = c2 // 2
    full = lambda shape: pl.BlockSpec(shape, lambda i: (0,) * len(shape))
    return pl.pallas_call(
        functools.partial(_filter_kernel, tr=tr, c=c),
        grid=(l // tr,),
        in_specs=[pl.BlockSpec((tr, zp), lambda i: (i, 0)), full((zp, hp)), full((1, hp)),
                  full((1, hp)), full((hp, hp)), full((1, hp)), full((hp, c2)), full((1, c))],
        out_specs=[pl.BlockSpec((2, tr, c), lambda i: (0, i, 0)), full((1, c))],
        out_shape=[jax.ShapeDtypeStruct((2, l, c), F32), jax.ShapeDtypeStruct((1, c), F32)],
        compiler_params=_cparams("arbitrary"),
        name="hyena_filter_mlp",
    )(zfeat, w1, b1, freq, w2, b2, w3, deltas)


def _filter_spec_kernel(h_ref, f1_ref, g_ref, nrm_ref, hre_ref, him_ref, s_ref):
    _, nh, ng, sub, tc = h_ref.shape
    n2 = ng * sub
    nreal = s_ref.shape[1]
    k1 = nreal // 2 + 1
    d = functools.partial(jnp.dot, preferred_element_type=F32)
    f1 = f1_ref[...]
    for q in range(2):
        for g in range(ng):
            x = h_ref[q, :, g].reshape(nh * sub, tc).astype(BF16)
            s_ref[q, :, g] = d(f1, x).reshape(nreal, sub, tc)
    inv = 1.0 / nrm_ref[...]
    zero = jnp.zeros((n2, tc), F32)

    def spectrum(k, has_im):
        xs = []
        for q in range(2):
            re = s_ref[q, k].reshape(n2, tc)
            im = s_ref[q, k1 - 1 + k].reshape(n2, tc) if has_im else zero
            xs.append(d(g_ref[k], jnp.concatenate([re, im], axis=0).astype(BF16)))
        xf, xg = xs
        hre_ref[k] = (xf[:n2] + xg[:n2]) * inv
        him_ref[k] = (xf[n2:] - xg[n2:]) * inv

    spectrum(0, False)
    spectrum(k1 - 1, False)

    def body(k, carry):
        spectrum(k, True)
        return carry

    lax.fori_loop(1, k1 - 1, body, 0, unroll=_LEVEL2_UNROLL)


def _filter_spectrum(hfg, f1k, g, nrm, tc=128):
    _, l, c = hfg.shape
    nh, sub = DFT_N1 // 2, SUBLANES
    ng = DFT_N2 // sub
    k1, n2 = g.shape[0], DFT_N2
    return pl.pallas_call(
        _filter_spec_kernel,
        grid=(c // tc,),
        in_specs=[pl.BlockSpec((2, nh, ng, sub, tc), lambda j: (0, 0, 0, 0, j)),
                  _resident(f1k.shape, lambda j: (0, 0)), _resident(g.shape, lambda j: (0, 0, 0)),
                  pl.BlockSpec((1, tc), lambda j: (0, j))],
        out_specs=[pl.BlockSpec((k1, n2, tc), lambda j: (0, 0, j))] * 2,
        out_shape=[jax.ShapeDtypeStruct((k1, n2, c), F32)] * 2,
        scratch_shapes=[pltpu.VMEM((2, DFT_N1, ng, sub, tc), F32)],
        compiler_params=_cparams("arbitrary"),
        name="hyena_filter_spectrum",
    )(hfg.reshape(2, nh, ng, sub, c), f1k, g, nrm)


_LEVEL2_UNROLL = 16


def _fftconv_kernel(vf_ref, x0_ref, gt_ref, f1_ref, a1_ref, g_ref, gi_ref, hre_ref, him_ref, bias_ref,
                    o_ref, s_ref, y_ref):
    nh, ng, sub, tc = vf_ref.shape[1:]
    n2 = ng * sub
    nreal = s_ref.shape[0]
    k1 = nreal // 2 + 1
    d = functools.partial(jnp.dot, preferred_element_type=F32)

    f1 = f1_ref[...]
    for g in range(ng):
        x = vf_ref[0, :, g].reshape(nh * sub, tc).astype(BF16)
        s_ref[:, g] = d(f1, x).reshape(nreal, sub, tc)

    def level2(k, re, im):
        x = d(g_ref[k], jnp.concatenate([re, im], axis=0).astype(BF16))
        xr, xi = x[:n2], x[n2:]
        hr, hi = hre_ref[k], him_ref[k]
        y = jnp.concatenate([xr * hr - xi * hi, xr * hi + xi * hr], axis=0).astype(BF16)
        return d(gi_ref[k], y)

    zero = jnp.zeros((n2, tc), F32)
    for k in (0, k1 - 1):
        z = level2(k, s_ref[k].reshape(n2, tc), zero)
        s_ref[k] = z[:n2].reshape(ng, sub, tc)

    def body(k, carry):
        z = level2(k, s_ref[k].reshape(n2, tc), s_ref[k1 - 1 + k].reshape(n2, tc))
        s_ref[k] = z[:n2].reshape(ng, sub, tc)
        s_ref[k1 - 1 + k] = z[n2:].reshape(ng, sub, tc)
        return carry

    lax.fori_loop(1, k1 - 1, body, 0, unroll=_LEVEL2_UNROLL)

    a1 = a1_ref[...]
    for g in range(ng):
        z = s_ref[:, g].reshape(nreal * sub, tc).astype(BF16)
        y_ref[:, g] = d(a1, z).reshape(nh, sub, tc)

    bias = bias_ref[...]
    step = 2
    rows = step * n2

    def epilogue(i, carry):
        r = pl.multiple_of(i * rows, rows)
        y = y_ref[pl.ds(i * step, step)].reshape(rows, tc)
        vf = vf_ref[0, pl.ds(i * step, step)].reshape(rows, tc)
        x0 = x0_ref[0, pl.ds(r, rows), :].astype(F32)
        gt = gt_ref[0, pl.ds(r, rows), :].astype(F32)
        o_ref[0, pl.ds(r, rows), :] = ((x0 * (y + vf * bias)) * gt).astype(o_ref.dtype)
        return carry

    lax.fori_loop(0, nh // step, epilogue, 0)


def _hyena_fftconv(vf, x0c, gate, f1k, a1k, g, ginv, hre, him, bias, tc=256):
    b, l, c = vf.shape
    nh, sub = DFT_N1 // 2, SUBLANES
    ng = DFT_N2 // sub
    kb = g.shape[0]
    n2 = DFT_N2
    rows = pl.BlockSpec((1, l, tc), lambda j, i: (i, 0, j))
    return pl.pallas_call(
        _fftconv_kernel,
        grid=(c // tc, b),
        in_specs=[pl.BlockSpec((1, nh, ng, sub, tc), lambda j, i: (i, 0, 0, 0, j)), rows, rows,
                  _resident(f1k.shape, lambda j, i: (0, 0)), _resident(a1k.shape, lambda j, i: (0, 0)),
                  _resident(g.shape, lambda j, i: (0, 0, 0)), _resident(ginv.shape, lambda j, i: (0, 0, 0)),
                  _resident((kb, n2, tc), lambda j, i: (0, 0, j)), _resident((kb, n2, tc), lambda j, i: (0, 0, j)),
                  pl.BlockSpec((1, tc), lambda j, i: (0, j))],
        out_specs=rows,
        out_shape=jax.ShapeDtypeStruct((b, l, c), BF16),
        scratch_shapes=[pltpu.VMEM((DFT_N1, ng, sub, tc), F32), pltpu.VMEM((nh, ng, sub, tc), F32)],
        compiler_params=_cparams("arbitrary", "arbitrary"),
        name="hyena_fftconv",
    )(vf.reshape(b, nh, ng, sub, c), x0c, gate, f1k, a1k, g, ginv, hre, him, bias)


@functools.lru_cache(maxsize=None)
def _dft_tables():
    n1, n2, k1 = DFT_N1, DFT_N2, DFT_K1
    n = n1 * n2
    nh = n1 // 2
    ang = 2.0 * np.pi * np.arange(k1)[:, None] * np.arange(nh)[None, :] / n1
    fre, fim = np.cos(ang), -np.sin(ang)
    m = np.arange(n2)[None, None, :]
    k2 = np.arange(n2)[None, :, None]
    ka = np.arange(k1)[:, None, None]
    ph = -2.0 * np.pi * m * (ka / n + k2 / n2)
    gr, gi = np.cos(ph), np.sin(ph)
    g = np.concatenate([np.concatenate([gr, -gi], axis=2), np.concatenate([gi, gr], axis=2)], axis=1)
    grt, git = np.swapaxes(gr, 1, 2), np.swapaxes(gi, 1, 2)
    ginv = np.concatenate([np.concatenate([grt, git], axis=2), np.concatenate([-git, grt], axis=2)], axis=1)
    wk = np.where((np.arange(k1) == 0) | (np.arange(k1) == n1 // 2), 1.0, 2.0)[None, :]
    are, aim = wk * fre.T / n, wk * fim.T / n
    eye = np.eye(SUBLANES)
    f1k = np.kron(np.concatenate([fre, fim[1:k1 - 1]], axis=0), eye)
    a1k = np.kron(np.concatenate([are, aim[:, 1:k1 - 1]], axis=1), eye)
    return tuple(np.asarray(a, np.float32).astype(BF16) for a in (g, ginv, f1k, a1k))


def _position_features(l):
    f32 = F32
    pos = jnp.arange(l, dtype=f32)[:, None]
    t = pos / max(l - 1, 1)
    bands = (HY_EMB - 1) // 2
    f = jnp.linspace(1e-4, bands - 1, bands, dtype=f32)[None, :]
    ang = f * pos * (2.0 * math.pi / l)
    z = jnp.concatenate([t, jnp.cos(ang), -jnp.sin(ang)], axis=-1)
    return jnp.pad(z, ((0, 0), (0, LANES - HY_EMB)))


def _pad2(a, rows, cols):
    return jnp.pad(a, ((0, rows - a.shape[0]), (0, cols - a.shape[1])))


def _hyena_spectrum(l, c, fw1, fb1, fw2, fb2, fw3, freq):
    g, _, f1k, _ = _dft_tables()
    hp = LANES
    zfeat = _position_features(l)
    w1 = _pad2(fw1, LANES, hp)
    w2 = _pad2(fw2, hp, hp)
    w3 = _pad2(fw3, hp, fw3.shape[1])
    row = lambda v: _pad2(v[None, :], 1, hp)
    deltas = jnp.abs(jnp.linspace(math.log(HY_DECAY_TARGET) / HY_LONG_DECAY_PCT,
                                  math.log(HY_DECAY_TARGET) / HY_SHORT_DECAY_PCT, c, dtype=F32))[None, :]
    hfg, nrm = _hyena_filter_taps(zfeat, w1, row(fb1), row(freq), w2, row(fb2), w3, deltas)
    return _filter_spectrum(hfg, jnp.asarray(f1k), jnp.asarray(g), nrm)


def _hyena_branch(x0c, vf, hy_gate, hre, him, bias):
    g, ginv, f1k, a1k = (jnp.asarray(t) for t in _dft_tables())
    return _hyena_fftconv(vf, x0c, hy_gate, f1k, a1k, g, ginv, hre, him, bias[None, :])


def _ssd_kernel(xs_ref, bm_ref, cm_ref, dt_ref, dtb_ref, alog_ref, rep_ref, dsk_ref, mask_ref,
                *rest, reverse, lane0, chunks):
    yo_ref = rest[0] if len(rest) == 3 else None
    y_ref, prev_ref = rest[-2:]
    q, n, hd = SSD_CHUNK, SSD_STATE, SSD_HEAD_DIM
    gw = (SSD_HEADS // SSD_GROUPS) * hd

    @pl.when(pl.program_id(1) == 0)
    def _():
        prev_ref[...] = jnp.zeros_like(prev_ref)

    mask = mask_ref[...]
    maskb = mask > 0
    neg_a = -jnp.exp(alog_ref[...])
    lo_lanes = lax.broadcasted_iota(jnp.int32, (1, LANES), 1) < hd
    d = functools.partial(jnp.dot, preferred_element_type=F32)
    npair = gw // LANES
    edge = 0 if reverse else q - 1

    def per_pair(v, c0, shape):
        b0 = jnp.broadcast_to(v[:, c0:c0 + 1], shape)
        b1 = jnp.broadcast_to(v[:, c0 + 1:c0 + 2], shape)
        return jnp.where(lo_lanes, b0, b1)

    order = range(chunks - 1, -1, -1) if reverse else range(chunks)
    for k in order:
        rows = pl.ds(k * q, q)
        xs_b = xs_ref[0, rows, :]
        bm = bm_ref[0, rows, :]
        cm = cm_ref[0, rows, :]
        dt = _softplus(dt_ref[0, rows, :] + dtb_ref[...])
        cum = _sel_dot(mask, dt * neg_a)
        src_t = (cum - jnp.log(dt)).T
        cum_e = cum[edge:edge + 1, :]
        e_in = jnp.exp(cum)
        w_out = dt * jnp.exp(cum_e - cum)
        e_all = jnp.exp(cum_e)
        spread = d(jnp.concatenate([e_in, w_out], axis=0).astype(BF16), rep_ref[...])
        e_x = spread[:q]
        xsd_all = xs_b * spread[q:].astype(BF16)
        ys = []
        for g in range(SSD_GROUPS):
            cg = cm[:, g * n:(g + 1) * n].astype(BF16)
            bg = bm[:, g * n:(g + 1) * n].astype(BF16)
            cb = lax.dot_general(cg, bg, (((1,), (1,)), ((), ())), preferred_element_type=F32)
            sl = slice(g * gw, (g + 1) * gw)
            prev_g = prev_ref[:, sl]
            y_off = d(cg, prev_g.astype(BF16))
            cds = []
            for pair in range(npair):
                lo = g * gw + pair * LANES
                c0 = lane0 + lo // hd
                ms = []
                for c in (c0, c0 + 1):
                    seg_dt = jnp.exp(jnp.where(maskb, cum[:, c:c + 1] - src_t[c:c + 1, :], -jnp.inf))
                    ms.append((cb * seg_dt).astype(BF16))
                xp = xs_b[:, lo:lo + LANES]
                zero = jnp.zeros_like(xp)
                rhs = jnp.concatenate([jnp.where(lo_lanes, xp, zero), jnp.where(lo_lanes, zero, xp)], axis=0)
                yd = d(jnp.concatenate(ms, axis=1), rhs)
                ys.append(yd + y_off[:, pair * LANES:(pair + 1) * LANES] * e_x[:, lo:lo + LANES])
                cds.append(per_pair(e_all, c0, (n, LANES)))
            st = lax.dot_general(bg, xsd_all[:, sl], (((0,), (0,)), ((), ())), preferred_element_type=F32)
            prev_ref[:, sl] = prev_g * jnp.concatenate(cds, axis=1) + st
        y = jnp.concatenate(ys, axis=1)
        if not reverse:
            y = y + xs_b.astype(F32) * dsk_ref[...]
        if yo_ref is not None:
            y = y + yo_ref[0, rows, :].astype(F32)
        y_ref[0, rows, :] = y.astype(y_ref.dtype)


def _ssd_direction(xbc, dt_raw, dtb, alog, rep, dskip, mask, y_other=None, *, reverse, ts=1024):
    b, l, _ = xbc.shape
    hp = SSD_HEADS * SSD_HEAD_DIM
    gn = SSD_GROUPS * SSD_STATE
    nt = l // ts
    tmap = (lambda j: nt - 1 - j) if reverse else (lambda j: j)
    full = lambda shape: pl.BlockSpec(shape, lambda i, j: (0,) * len(shape))
    yspec = pl.BlockSpec((1, ts, hp), lambda i, j: (i, tmap(j), 0))
    extra = [] if y_other is None else [y_other]
    return pl.pallas_call(
        functools.partial(_ssd_kernel, reverse=reverse, lane0=SSD_HEADS if reverse else 0,
                          chunks=ts // SSD_CHUNK),
        grid=(b, nt),
        in_specs=[yspec,
                  pl.BlockSpec((1, ts, gn), lambda i, j: (i, tmap(j), hp // gn)),
                  pl.BlockSpec((1, ts, gn), lambda i, j: (i, tmap(j), hp // gn + 1)),
                  pl.BlockSpec((1, ts, LANES), lambda i, j: (i, tmap(j), 0)),
                  full((1, LANES)), full((1, LANES)), full((LANES, hp)), full((1, hp)),
                  full((SSD_CHUNK, SSD_CHUNK))] + [yspec] * len(extra),
        out_specs=yspec,
        out_shape=jax.ShapeDtypeStruct((b, l, hp), BF16),
        scratch_shapes=[pltpu.VMEM((SSD_STATE, hp), F32)],
        compiler_params=_cparams("parallel", "arbitrary"),
        name="ssd_bwd" if reverse else "ssd_fwd",
    )(xbc, xbc, xbc, dt_raw, dtb, alog, rep, dskip, mask, *extra)


def _ssd_branch(xc, dt_raw, dt_bias, a_log, d_skip):
    pad_row = lambda v: _pad2(v.reshape(1, -1), 1, LANES)
    dtb, alog = pad_row(dt_bias), pad_row(a_log)
    dskip = jnp.repeat(d_skip, SSD_HEAD_DIM)[None, :]
    lane_head = np.arange(SSD_HEADS * SSD_HEAD_DIM)[None, :] // SSD_HEAD_DIM
    src = np.arange(LANES)[:, None]
    idx = np.arange(SSD_CHUNK)
    y = None
    for reverse in (False, True):
        rep = jnp.asarray(src == lane_head + (SSD_HEADS if reverse else 0), dtype=BF16)
        allowed = (idx[None, :] >= idx[:, None]) if reverse else (idx[None, :] <= idx[:, None])
        mask = jnp.asarray(allowed, dtype=BF16)
        y = _ssd_direction(xc, dt_raw, dtb, alog, rep, dskip, mask, y, reverse=reverse)
    return y


_LRU_ROWS = 256


def _lru_kernel(*refs, l, heads):
    x_refs, (gt_ref, wa_ref, wx_ref, ba_ref, bx_ref, lam_ref, o_ref), seq_refs = (
        refs[:heads], refs[heads:heads + 7], refs[heads + 7:])
    blk = _SEG_BLOCK
    d = functools.partial(jnp.dot, preferred_element_type=F32)

    def rows_segment_order(hd, r):
        x = x_refs[hd].at[0]
        return jnp.concatenate([x[pl.ds(r + base + j, SUBLANES, stride=SUBLANES), :]
                                for base in range(0, _CONV_ROWS, blk) for j in range(SUBLANES)], axis=0)
    e0 = (-0.5 * LRU_C / math.log(2.0)) * _softplus(-lam_ref[...])
    ba, bx = ba_ref[...], bx_ref[...]
    a_refs, u_refs, h_refs = ([[seq_refs[(q * 2 + dr) * heads + hd] for hd in range(heads)] for dr in range(2)]
                              for q in range(3))

    def gates(i, carry):
        r = pl.multiple_of(i * _CONV_ROWS, _CONV_ROWS)
        for hd in range(heads):
            ls = slice(hd * LRU_BLOCK, (hd + 1) * LRU_BLOCK)
            xh = rows_segment_order(hd, r)
            xb = xh.astype(BF16)
            xh2 = 0.5 * xh
            for dr in range(2):
                tr = jnp.tanh(d(xb, wa_ref[dr, hd]) + ba[dr:dr + 1, ls])
                ti = jnp.tanh(d(xb, wx_ref[dr, hd]) + bx[dr:dr + 1, ls])
                e = e0[dr:dr + 1, ls]
                a = jnp.exp2(e + e * tr)
                s = 1.0 - a * a
                root = jnp.where(s > 0.0, s * lax.rsqrt(s), 0.0)
                u = (xh2 + xh2 * ti) * root
                a_refs[dr][hd][pl.ds(r, _CONV_ROWS), :] = a
                u_refs[dr][hd][pl.ds(r, _CONV_ROWS), :] = u
        return carry

    lax.fori_loop(0, l // _CONV_ROWS, gates, 0, unroll=4)

    row = lax.broadcasted_iota(jnp.int32, (SUBLANES, LRU_BLOCK), 0)
    nblk = l // blk

    def sublane_scan(a, u, reverse):
        for s in (1, 2, 4):
            sh = (SUBLANES - s) if reverse else s
            valid = (row < SUBLANES - s) if reverse else (row >= s)
            a_sh = jnp.where(valid, pltpu.roll(a, sh, axis=0), 1.0)
            u_sh = jnp.where(valid, pltpu.roll(u, sh, axis=0), 0.0)
            u = u + a * u_sh
            a = a * a_sh
        return a, u

    def block_scan(a_ref, u_ref, h_ref, r, h_in, reverse):
        order = range(SUBLANES - 1, -1, -1) if reverse else range(SUBLANES)
        hs, ps = {}, {}
        h = p = None
        for k in order:
            rows = pl.ds(r + SUBLANES * k, SUBLANES)
            a, u = a_ref[rows, :], u_ref[rows, :]
            h = u if h is None else a * h + u
            p = a if p is None else a * p
            hs[k], ps[k] = h, p
        sp, se = sublane_scan(p, h, reverse)
        full = se + sp * h_in
        edge = 0 if reverse else SUBLANES - 1
        h_out = jnp.broadcast_to(full[edge:edge + 1, :], (SUBLANES, LRU_BLOCK))
        if reverse:
            carry = jnp.where(row == SUBLANES - 1, h_in, pltpu.roll(full, SUBLANES - 1, axis=0))
        else:
            carry = jnp.where(row == 0, h_in, pltpu.roll(full, 1, axis=0))
        for k in order:
            h_ref[pl.ds(r + k, SUBLANES, stride=SUBLANES), :] = hs[k] + ps[k] * carry
        return h_out

    def scan(i, carry):
        rf = pl.multiple_of(i * blk, blk)
        rb = pl.multiple_of((nblk - 1 - i) * blk, blk)
        out = []
        for hd in range(heads):
            hf, hb = carry[2 * hd], carry[2 * hd + 1]
            out.append(block_scan(a_refs[0][hd], u_refs[0][hd], h_refs[0][hd], rf, hf, False))
            out.append(block_scan(a_refs[1][hd], u_refs[1][hd], h_refs[1][hd], rb, hb, True))
        return tuple(out)

    zero = jnp.zeros((SUBLANES, LRU_BLOCK), F32)
    lax.fori_loop(0, nblk, scan, (zero,) * (2 * heads), unroll=4)

    def combine(i, carry):
        r = pl.multiple_of(i * _LRU_ROWS, _LRU_ROWS)
        rows = pl.ds(r, _LRU_ROWS)
        h = jnp.concatenate([h_refs[0][hd][rows, :] + h_refs[1][hd][rows, :] for hd in range(heads)], axis=1)
        o_ref[0, rows, :] = (h * gt_ref[0, rows, :].astype(F32)).astype(o_ref.dtype)
        return carry

    lax.fori_loop(0, l // _LRU_ROWS, combine, 0)


def _rglru_mixer(xc, gate, w_a, w_x, b_a, b_x, lam, heads=2):
    b, l, w = xc.shape
    tw = heads * LRU_BLOCK
    gspec = pl.BlockSpec((1, l, tw), lambda i, j: (i, 0, j))
    xspecs = [pl.BlockSpec((1, l, LRU_BLOCK), lambda i, j, hd=hd: (i, 0, j * heads + hd)) for hd in range(heads)]
    wspec = pl.BlockSpec((2, heads, LRU_BLOCK, LRU_BLOCK), lambda i, j: (0, j, 0, 0))
    vspec = pl.BlockSpec((2, tw), lambda i, j: (0, j))
    seqs = [pltpu.VMEM((l, LRU_BLOCK), F32)] * (3 * 2 * heads)
    return pl.pallas_call(
        functools.partial(_lru_kernel, l=l, heads=heads),
        grid=(b, w // tw),
        in_specs=xspecs + [gspec, wspec, wspec, vspec, vspec, vspec],
        out_specs=gspec,
        out_shape=jax.ShapeDtypeStruct((b, l, w), BF16),
        scratch_shapes=seqs,
        compiler_params=_cparams("parallel", "parallel"),
        name="rglru",
    )(*([xc] * heads), gate, (0.5 * w_a).astype(BF16), (0.5 * w_x).astype(BF16), 0.5 * b_a, 0.5 * b_x, lam)


def _rms(x, g, eps):
    return (x * lax.rsqrt(jnp.mean(x * x, axis=-1, keepdims=True) + eps)) * g


def _norm_modulate(x, g, shift, scale):
    ms = jnp.mean(x * x, axis=-1, keepdims=True)
    return ((x * lax.rsqrt(ms + 1e-6)) * g) * (1.0 + scale) + shift


def _outproj_even_kernel(*refs, n_x, bp):
    x_refs, (ya_ref, ys_ref, z_ref, gm_ref, ng_ref, wa_ref, wb_ref, o_ref) = refs[:n_x], refs[n_x:]
    yb = ys_ref[0].astype(F32) * z_ref[0].astype(F32)
    ng = ng_ref[...]
    gw = yb.shape[1] // SSD_GROUPS
    parts = [_rms(yb[:, g * gw:(g + 1) * gw], ng[:, g * gw:(g + 1) * gw], 1e-5) for g in range(SSD_GROUPS)]
    ybn = jnp.concatenate(parts, axis=1)
    d = functools.partial(jnp.dot, preferred_element_type=F32)
    acc = d(ya_ref[0].astype(BF16), wa_ref[...]) + d(ybn.astype(BF16), wb_ref[...])
    o_ref[0] = _load_rows(x_refs, bp) + gm_ref[0] * acc


def _out_projection_even(x, ya, ys, z, gate_mod, norm_g, w_out, tl=1024):
    b, l, c = ya.shape
    dm = x[0].shape[2]
    rows = lambda n: pl.BlockSpec((1, tl, n), lambda i, j: (i, j, 0))
    wa, wb = w_out[:c].astype(BF16), w_out[c:].astype(BF16)
    return pl.pallas_call(
        functools.partial(_outproj_even_kernel, n_x=len(x), bp=x[0].shape[0]),
        grid=(b, l // tl),
        in_specs=_row_specs(x, tl, l // tl) + [rows(c), rows(c), rows(c),
                                      pl.BlockSpec((1, 1, dm), lambda i, j: (i, 0, 0)),
                                      pl.BlockSpec((1, c), lambda i, j: (0, 0)),
                                      _resident(wa.shape, lambda i, j: (0, 0)),
                                      _resident(wb.shape, lambda i, j: (0, 0))],
        out_specs=rows(dm),
        out_shape=jax.ShapeDtypeStruct((b, l, dm), F32),
        compiler_params=_cparams("arbitrary", "arbitrary"),
        name="out_proj_even",
    )(*x, ya, ys, z, gate_mod, norm_g, wa, wb)


def _outproj_odd_kernel(x_ref, y_ref, gm_ref, w_ref, fg_ref, *o_refs, bp):
    acc = jnp.dot(y_ref[0].astype(BF16), w_ref[...], preferred_element_type=F32)
    xn = x_ref[0] + gm_ref[0] * acc
    if len(o_refs) == 1:
        o_refs[0][0] = xn
        return
    out = _rms(xn, fg_ref[...], 1e-6)

    @pl.when(pl.program_id(0) < bp)
    def _():
        o_refs[0][0] = out

    @pl.when(pl.program_id(0) >= bp)
    def _():
        o_refs[1][0] = out


def _out_projection_odd(x, y, gate_mod, w_out, final_g, split_at=None, tl=1024):
    b, l, dm = x.shape
    w = y.shape[2]
    rows = lambda n: pl.BlockSpec((1, tl, n), lambda i, j: (i, j, 0))
    wb = w_out.astype(BF16)
    if split_at is None:
        out_specs = [rows(dm)]
        out_shape = [jax.ShapeDtypeStruct((b, l, dm), F32)]
    else:
        out_specs = [pl.BlockSpec((1, tl, dm), m) for m in _pair_row_maps(split_at, l // tl)]
        out_shape = [jax.ShapeDtypeStruct((n, l, dm), F32) for n in (split_at, b - split_at)]
    out = pl.pallas_call(
        functools.partial(_outproj_odd_kernel, bp=split_at),
        grid=(b, l // tl),
        in_specs=[rows(dm), rows(w),
                  pl.BlockSpec((1, 1, dm), lambda i, j: (i, 0, 0)),
                  _resident(wb.shape, lambda i, j: (0, 0)),
                  pl.BlockSpec((1, dm), lambda i, j: (0, 0))],
        out_specs=out_specs,
        out_shape=out_shape,
        compiler_params=_cparams("arbitrary", "arbitrary"),
        name="out_proj_odd",
    )(x, y, gate_mod, wb, final_g)
    return out[0] if split_at is None else tuple(out)


def kernel(x_prompt, x_sample, c_prompt, c_sample, mod_w, mod_b, norm_g, final_g, ev_w_in, ev_w_out, hy_conv_w, hy_conv_b, hy_fw1, hy_fb1, hy_fw2, hy_fb2, hy_fw3, hy_freq, hy_bias, ssd_conv_w, ssd_conv_b, ssd_dt_bias, ssd_A_log, ssd_D, ssd_norm_g, od_w_in, od_w_out, lru_conv_w, lru_conv_b, lru_w_a, lru_b_a, lru_w_x, lru_b_x, lru_lam):
    bp, l, dm = x_prompt.shape
    assert x_sample.shape[1:] == (l, dm), "the two request groups are stacked on the batch axis"
    x = (x_prompt, x_sample)
    c = jnp.concatenate([c_prompt, c_sample], axis=0)
    b = c.shape[0]
    depth = mod_w.shape[0]
    assert 2 * l == DFT_N1 * DFT_N2
    assert depth % 2 == 0, "the final norm is fused into the last (odd) layer's output projection"

    b_pad = -(-b // SUBLANES) * SUBLANES
    mods = _modulation(jnp.pad(c, ((0, b_pad - b), (0, 0))), mod_w, mod_b)[:, :b]
    shift, scale, gate_mod = (mods[:, :, k * dm:(k + 1) * dm].reshape(depth, b, 1, dm) for k in range(3))

    hyc = hy_bias.shape[1]
    ssw = ssd_norm_g.shape[1]
    xbw = ssd_conv_w.shape[2]
    o1, o2, o3, o4 = 3 * hyc, 4 * hyc, 4 * hyc + ssw, 4 * hyc + ssw + xbw

    for i in range(depth):
        j = i // 2
        g = norm_g[i][None, :]
        if i % 2 == 0:
            w_in = ev_w_in[j].astype(BF16)
            w_dt = jnp.pad(w_in[:, o4:], ((0, 0), (0, LANES - (w_in.shape[1] - o4))))
            x0c, vf, hyg, z, xc, dt_raw = _in_projection(x, shift[i], scale[i], g, [
                dict(w=w_in[:, :o1], dtype=(BF16, F32), post="hyena", conv_w=hy_conv_w[j],
                     conv_b=hy_conv_b[j][None, :]),
                dict(w=w_in[:, o1:o2], dtype=BF16, post="silu"),
                dict(w=w_in[:, o2:o3], dtype=BF16, post="silu"),
                dict(w=w_in[:, o3:o4], dtype=BF16, post="conv_silu", conv_w=ssd_conv_w[j],
                     conv_b=ssd_conv_b[j][None, :]),
                dict(w=w_dt, dtype=F32)])
            hre, him = _hyena_spectrum(l, hyc, hy_fw1[j], hy_fb1[j], hy_fw2[j], hy_fb2[j], hy_fw3[j], hy_freq[j])
            ya = _hyena_branch(x0c, vf, hyg, hre, him, hy_bias[j])
            ys = _ssd_branch(xc, dt_raw, ssd_dt_bias[j], ssd_A_log[j], ssd_D[j])
            x = (_out_projection_even(x, ya, ys, z, gate_mod[i], ssd_norm_g[j][None, :], ev_w_out[j]),)
        else:
            w_in = od_w_in[j].astype(BF16)
            half = w_in.shape[1] // 2
            xc, gt = _in_projection(x, shift[i], scale[i], g, [
                dict(w=w_in[:, :half], dtype=F32, post="conv", conv_w=lru_conv_w[j], conv_b=lru_conv_b[j][None, :]),
                dict(w=w_in[:, half:], dtype=BF16, post="silu")])
            y = _rglru_mixer(xc, gt, lru_w_a[j], lru_w_x[j], lru_b_a[j], lru_b_x[j], lru_lam[j])
            if i == depth - 1:
                return _out_projection_odd(x[0], y, gate_mod[i], od_w_out[j], final_g[None, :], split_at=bp)
            x = (_out_projection_odd(x[0], y, gate_mod[i], od_w_out[j], final_g[None, :]),)
```

```python
import functools
import math

import numpy as np
import jax
import jax.numpy as jnp
from jax import lax
from jax.experimental import pallas as pl
from jax.experimental.pallas import tpu as pltpu

F32 = jnp.float32
BF16 = jnp.bfloat16
HI = lax.Precision.HIGHEST

V7X_VMEM_BYTES = 64 * 1024 * 1024
VMEM_LIMIT = V7X_VMEM_BYTES - 8 * 1024 * 1024
LANES = 128
SUBLANES = 8

IN_PROJ_ROWS = 512
OUT_PROJ_ROWS = 1024
SSD_ROWS = 1024
FFTCONV_CHANNELS = 256
FILTER_CHANNELS = 128
FILTER_ROWS = 512
LRU_HEADS_PER_STEP = 2

HY_EMB = 33
HY_SHORT_DECAY_PCT = 0.3
HY_LONG_DECAY_PCT = 1.5
HY_DECAY_TARGET = 1e-2
SSD_HEADS = 16
SSD_HEAD_DIM = 64
SSD_GROUPS = 4
SSD_STATE = 128
SSD_CHUNK = 128
LRU_BLOCK = 128
LRU_C = 8.0

DFT_N1 = 64
DFT_N2 = 128
DFT_K1 = DFT_N1 // 2 + 1


def _cparams(*sem):
    return pltpu.CompilerParams(dimension_semantics=sem, vmem_limit_bytes=VMEM_LIMIT)


def _resident(shape, index_map):
    return pl.BlockSpec(shape, index_map, pipeline_mode=pl.Buffered(1))


def _sigmoid(x):
    return 0.5 + 0.5 * jnp.tanh(0.5 * x)


def _silu(x):
    return x * _sigmoid(x)


def _softplus(x):
    return jnp.maximum(x, 0.0) + jnp.log1p(jnp.exp(-jnp.abs(x)))


def _norm_modulate(x, g, shift, scale):
    ms = jnp.mean(x * x, axis=-1, keepdims=True)
    return ((x * lax.rsqrt(ms + 1e-6)) * g) * (1.0 + scale) + shift


def _split3(x):
    hi = x.astype(BF16)
    r1 = x - hi.astype(F32)
    mid = r1.astype(BF16)
    lo = (r1 - mid.astype(F32)).astype(BF16)
    return hi, mid, lo


def _sel_dot(sel, x):
    hi, mid, lo = _split3(x)
    d = functools.partial(jnp.dot, preferred_element_type=F32)
    return d(sel, hi) + d(sel, mid) + d(sel, lo)


def _mod_kernel(c_ref, w_ref, b_ref, o_ref):
    cs = _silu(c_ref[...])
    o_ref[0] = jnp.dot(cs, w_ref[0], precision=HI, preferred_element_type=F32) + b_ref[0]


def _modulation(c_pad, mod_w, mod_b):
    depth, d, d3 = mod_w.shape
    bp = c_pad.shape[0]
    return pl.pallas_call(
        _mod_kernel,
        grid=(depth, d3 // d),
        in_specs=[
            pl.BlockSpec((bp, d), lambda i, j: (0, 0)),
            pl.BlockSpec((1, d, d), lambda i, j: (i, 0, j)),
            pl.BlockSpec((1, 1, d), lambda i, j: (i, 0, j)),
        ],
        out_specs=pl.BlockSpec((1, bp, d), lambda i, j: (i, 0, j)),
        out_shape=jax.ShapeDtypeStruct((depth, bp, d3), F32),
        compiler_params=_cparams("arbitrary", "arbitrary"),
        name="adaln_mod",
    )(c_pad, mod_w, mod_b.reshape(depth, 1, d3))


def _pair_row_maps(bp, nj):
    first = lambda i, j: (jnp.minimum(i, bp - 1), jnp.where(i < bp, j, nj - 1), 0)
    second = lambda i, j: (jnp.maximum(i - bp, 0), jnp.where(i < bp, 0, j), 0)
    return first, second


def _load_rows(x_refs, bp):
    if len(x_refs) == 1:
        return x_refs[0][0]
    return jnp.where(pl.program_id(0) < bp, x_refs[0][0], x_refs[1][0])


def _row_specs(xs, rows, nj, f=lambda j: j):
    d = xs[0].shape[2]
    if len(xs) == 1:
        return [pl.BlockSpec((1, rows, d), lambda i, j: (i, f(j), 0))]
    bp = xs[0].shape[0]
    first = lambda i, j: (jnp.minimum(i, bp - 1), jnp.where(i < bp, f(j), f(nj - 1)), 0)
    second = lambda i, j: (jnp.maximum(i - bp, 0), jnp.where(i < bp, f(0), f(j)), 0)
    return [pl.BlockSpec((1, rows, d), first), pl.BlockSpec((1, rows, d), second)]


_PROJ_GROUP = 2 * LANES


_CONV_POSTS = ("conv", "conv_silu", "hyena")


def _inproj_kernel(*refs, n_x, bp, plan, tl):
    refs = list(refs)
    take = lambda n: [refs.pop(0) for _ in range(n)]
    has_conv = any(post in _CONV_POSTS for post, _, _ in plan)
    x_main = take(n_x)
    x_prev, x_next = (take(n_x), take(n_x)) if has_conv else (None, None)
    sh_ref, sc_ref, g_ref = take(3)
    piece_refs = [take(3 if post in _CONV_POSTS else 1) for post, _, _ in plan]
    out_refs = [take(n_out) for _, _, n_out in plan]
    pad_ref = refs.pop(0) if has_conv else None

    j, last = pl.program_id(1), pl.num_programs(1) - 1
    norm = lambda x: _norm_modulate(x, g_ref[...], sh_ref[0], sc_ref[0])
    d = functools.partial(jnp.dot, preferred_element_type=F32)
    hn = norm(_load_rows(x_main, bp))
    hb = hn.astype(BF16)
    if has_conv:
        hb_ext = jnp.concatenate([norm(_load_rows(x_prev, bp)), hn, norm(_load_rows(x_next, bp))],
                                 axis=0).astype(BF16)
    slabs = _PROJ_GROUP // LANES

    def project_to_pad(w_ref, col, pad):
        u = d(hb_ext, w_ref[:, col:col + _PROJ_GROUP])
        for s in range(slabs):
            ls = slice(s * LANES, (s + 1) * LANES)
            pad[s, pl.ds(0, _HALO), :] = jnp.where(j > 0, u[:_HALO, ls], 0.0)
            pad[s, pl.ds(_HALO, tl), :] = u[_HALO:_HALO + tl, ls]
            pad[s, pl.ds(_HALO + tl, _HALO), :] = jnp.where(j < last, u[_HALO + tl:, ls], 0.0)

    def conv_from_pad(pad, cw_ref, cb_ref, col, taps, r):
        parts = [_conv_rows(pad.at[s], r, cw_ref[:, col + s * LANES:col + (s + 1) * LANES],
                            cb_ref[:, col + s * LANES:col + (s + 1) * LANES], taps, taps // 2)
                 for s in range(slabs)]
        return jnp.concatenate(parts, axis=1)

    group = 0
    for (post, taps, _), p_refs, o_refs in zip(plan, piece_refs, out_refs):
        w_ref = p_refs[0]
        if post not in _CONV_POSTS:
            u = d(hb, w_ref[...])
            o_refs[0][0] = (_silu(u) if post == "silu" else u).astype(o_refs[0].dtype)
            continue
        cw_ref, cb_ref = p_refs[1:]
        parts = 3 if post == "hyena" else 1
        width = w_ref.shape[1] // parts
        for c0 in range(0, width, _PROJ_GROUP):
            pads = pad_ref.at[group % 2]
            group += 1
            for o in range(parts):
                project_to_pad(w_ref, o * width + c0, pads.at[o])
            for r in range(0, tl, _CONV_ROWS):
                rows, cols = pl.ds(r, _CONV_ROWS), slice(c0, c0 + _PROJ_GROUP)
                cv = [conv_from_pad(pads.at[o], cw_ref, cb_ref, o * width + c0, taps, r) for o in range(parts)]
                if post == "hyena":
                    o_refs[0][0, rows, cols] = cv[0].astype(o_refs[0].dtype)
                    o_refs[1][0, rows, cols] = (cv[2] * cv[1]).astype(o_refs[1].dtype)
                else:
                    y = _silu(cv[0]) if post == "conv_silu" else cv[0]
                    o_refs[0][0, rows, cols] = y.astype(o_refs[0].dtype)


def _in_projection(x, shift, scale, g, pieces, tl=IN_PROJ_ROWS):
    b = sum(a.shape[0] for a in x)
    l, d = x[0].shape[1:]
    nj = l // tl
    is_conv = lambda p: p.get("post") in _CONV_POSTS
    outs = []
    plan = []
    for p in pieces:
        n = p["w"].shape[1]
        if p.get("post") == "hyena":
            outs += [(n // 3, p["dtype"][0]), (n // 3, p["dtype"][1])]
        else:
            outs.append((n, p["dtype"]))
        plan.append((p.get("post"), p["conv_w"].shape[0] if is_conv(p) else 0, 2 if p.get("post") == "hyena" else 1))
    has_conv = any(is_conv(p) for p in pieces)
    const = lambda a: _resident(a.shape, lambda i, j: (0,) * a.ndim)
    in_specs, args = _row_specs(x, tl, nj), list(x)
    if has_conv:
        per = tl // _HALO
        in_specs += _row_specs(x, _HALO, nj, lambda j: jnp.maximum(j * per - 1, 0))
        in_specs += _row_specs(x, _HALO, nj, lambda j: jnp.minimum((j + 1) * per, l // _HALO - 1))
        args += list(x) * 2
    in_specs += [pl.BlockSpec((1, 1, d), lambda i, j: (i, 0, 0)), pl.BlockSpec((1, 1, d), lambda i, j: (i, 0, 0)),
                 pl.BlockSpec((1, d), lambda i, j: (0, 0))]
    args += [shift, scale, g]
    for p in pieces:
        for a in [p["w"]] + ([p["conv_w"], p["conv_b"]] if is_conv(p) else []):
            in_specs.append(const(a))
            args.append(a)
    scratch = [pltpu.VMEM((2, 3, _PROJ_GROUP // LANES, tl + 2 * _HALO, LANES), F32)] if has_conv else []
    return pl.pallas_call(
        functools.partial(_inproj_kernel, n_x=len(x), bp=x[0].shape[0], plan=tuple(plan), tl=tl),
        grid=(b, nj),
        in_specs=in_specs,
        out_specs=[pl.BlockSpec((1, tl, n), lambda i, j: (i, j, 0)) for n, _ in outs],
        out_shape=[jax.ShapeDtypeStruct((b, l, n), dt) for n, dt in outs],
        scratch_shapes=scratch,
        compiler_params=_cparams("arbitrary", "arbitrary"),
        name="in_proj",
    )(*args)


_CONV_ROWS = 256
_HALO = SUBLANES


def _conv_rows(pad_ref, r, w, bias, taps, left):
    acc = None
    for k in range(taps):
        term = w[k:k + 1, :] * pad_ref[pl.ds(r + _HALO + k - left, _CONV_ROWS), :]
        acc = term if acc is None else acc + term
    return acc + bias


_SEG_BLOCK = SUBLANES * SUBLANES


def _filter_kernel(z_ref, w1_ref, b1_ref, fr_ref, w2_ref, b2_ref, w3_ref, dl_ref,
                   hfg_ref, nrm_ref, *, tr, c):
    i = pl.program_id(0)
    z = z_ref[...]
    fr = fr_ref[...]
    d = functools.partial(jnp.dot, precision=HI, preferred_element_type=F32)
    h = jnp.sin(fr * (d(z, w1_ref[...]) + b1_ref[...]))
    h = jnp.sin(fr * (d(h, w2_ref[...]) + b2_ref[...]))
    h = jnp.dot(h.astype(BF16), w3_ref[...].astype(BF16), preferred_element_type=F32)
    window = jnp.exp(-z[:, 0:1] * dl_ref[...])
    hf = h[:, :c] * window
    row = lax.broadcasted_iota(jnp.int32, (tr, 1), 0) + i * tr
    g = jnp.where(row == 0, 0.0, h[:, c:] * window)
    hfg_ref[0] = hf
    hfg_ref[1] = g
    part = jnp.sum(jnp.abs(hf) + jnp.abs(g), axis=0, keepdims=True)

    @pl.when(i == 0)
    def _():
        nrm_ref[...] = part

    @pl.when(i != 0)
    def _():
        nrm_ref[...] = nrm_ref[...] + part


def _hyena_filter_taps(zfeat, w1, b1, freq, w2, b2, w3, deltas, tr=FILTER_ROWS):
    l, zp = zfeat.shape
    hp = w1.shape[1]
    c2 = w3.shape[1]
    c = c2 // 2
    full = lambda shape: pl.BlockSpec(shape, lambda i: (0,) * len(shape))
    return pl.pallas_call(
        functools.partial(_filter_kernel, tr=tr, c=c),
        grid=(l // tr,),
        in_specs=[pl.BlockSpec((tr, zp), lambda i: (i, 0)), full((zp, hp)), full((1, hp)),
                  full((1, hp)), full((hp, hp)), full((1, hp)), full((hp, c2)), full((1, c))],
        out_specs=[pl.BlockSpec((2, tr, c), lambda i: (0, i, 0)), full((1, c))],
        out_shape=[jax.ShapeDtypeStruct((2, l, c), F32), jax.ShapeDtypeStruct((1, c), F32)],
        compiler_params=_cparams("arbitrary"),
        name="hyena_filter_mlp",
    )(zfeat, w1, b1, freq, w2, b2, w3, deltas)


def _filter_spec_kernel(h_ref, f1_ref, g_ref, nrm_ref, hre_ref, him_ref, s_ref):
    _, nh, ng, sub, tc = h_ref.shape
    n2 = ng * sub
    nreal = s_ref.shape[1]
    k1 = nreal // 2 + 1
    d = functools.partial(jnp.dot, preferred_element_type=F32)
    f1 = f1_ref[...]
    for q in range(2):
        for g in range(ng):
            x = h_ref[q, :, g].reshape(nh * sub, tc).astype(BF16)
            s_ref[q, :, g] = d(f1, x).reshape(nreal, sub, tc)
    inv = 1.0 / nrm_ref[...]
    zero = jnp.zeros((n2, tc), F32)

    def spectrum(k, has_im):
        xs = []
        for q in range(2):
            re = s_ref[q, k].reshape(n2, tc)
            im = s_ref[q, k1 - 1 + k].reshape(n2, tc) if has_im else zero
            xs.append(d(g_ref[k], jnp.concatenate([re, im], axis=0).astype(BF16)))
        xf, xg = xs
        hre_ref[k] = (xf[:n2] + xg[:n2]) * inv
        him_ref[k] = (xf[n2:] - xg[n2:]) * inv

    spectrum(0, False)
    spectrum(k1 - 1, False)

    def body(k, carry):
        spectrum(k, True)
        return carry

    lax.fori_loop(1, k1 - 1, body, 0, unroll=_LEVEL2_UNROLL)


def _filter_spectrum(hfg, f1k, g, nrm, tc=FILTER_CHANNELS):
    _, l, c = hfg.shape
    nh, sub = DFT_N1 // 2, SUBLANES
    ng = DFT_N2 // sub
    k1, n2 = g.shape[0], DFT_N2
    return pl.pallas_call(
        _filter_spec_kernel,
        grid=(c // tc,),
        in_specs=[pl.BlockSpec((2, nh, ng, sub, tc), lambda j: (0, 0, 0, 0, j)),
                  _resident(f1k.shape, lambda j: (0, 0)), _resident(g.shape, lambda j: (0, 0, 0)),
                  pl.BlockSpec((1, tc), lambda j: (0, j))],
        out_specs=[pl.BlockSpec((k1, n2, tc), lambda j: (0, 0, j))] * 2,
        out_shape=[jax.ShapeDtypeStruct((k1, n2, c), F32)] * 2,
        scratch_shapes=[pltpu.VMEM((2, DFT_N1, ng, sub, tc), F32)],
        compiler_params=_cparams("arbitrary"),
        name="hyena_filter_spectrum",
    )(hfg.reshape(2, nh, ng, sub, c), f1k, g, nrm)


_LEVEL2_UNROLL = 16


def _fftconv_kernel(vf_ref, x0_ref, gt_ref, f1_ref, a1_ref, g_ref, gi_ref, hre_ref, him_ref, bias_ref,
                    o_ref, s_ref, y_ref):
    nh, ng, sub, tc = vf_ref.shape[1:]
    n2 = ng * sub
    nreal = s_ref.shape[0]
    k1 = nreal // 2 + 1
    d = functools.partial(jnp.dot, preferred_element_type=F32)

    f1 = f1_ref[...]
    for g in range(ng):
        x = vf_ref[0, :, g].reshape(nh * sub, tc).astype(BF16)
        s_ref[:, g] = d(f1, x).reshape(nreal, sub, tc)

    def level2(k, re, im):
        x = d(g_ref[k], jnp.concatenate([re, im], axis=0).astype(BF16))
        xr, xi = x[:n2], x[n2:]
        hr, hi = hre_ref[k], him_ref[k]
        y = jnp.concatenate([xr * hr - xi * hi, xr * hi + xi * hr], axis=0).astype(BF16)
        return d(gi_ref[k], y)

    zero = jnp.zeros((n2, tc), F32)
    for k in (0, k1 - 1):
        z = level2(k, s_ref[k].reshape(n2, tc), zero)
        s_ref[k] = z[:n2].reshape(ng, sub, tc)

    def body(k, carry):
        z = level2(k, s_ref[k].reshape(n2, tc), s_ref[k1 - 1 + k].reshape(n2, tc))
        s_ref[k] = z[:n2].reshape(ng, sub, tc)
        s_ref[k1 - 1 + k] = z[n2:].reshape(ng, sub, tc)
        return carry

    lax.fori_loop(1, k1 - 1, body, 0, unroll=_LEVEL2_UNROLL)

    a1 = a1_ref[...]
    for g in range(ng):
        z = s_ref[:, g].reshape(nreal * sub, tc).astype(BF16)
        y_ref[:, g] = d(a1, z).reshape(nh, sub, tc)

    bias = bias_ref[...]
    step = 2
    rows = step * n2

    def epilogue(i, carry):
        r = pl.multiple_of(i * rows, rows)
        y = y_ref[pl.ds(i * step, step)].reshape(rows, tc)
        vf = vf_ref[0, pl.ds(i * step, step)].reshape(rows, tc)
        x0 = x0_ref[0, pl.ds(r, rows), :].astype(F32)
        gt = gt_ref[0, pl.ds(r, rows), :].astype(F32)
        o_ref[0, pl.ds(r, rows), :] = ((x0 * (y + vf * bias)) * gt).astype(o_ref.dtype)
        return carry

    lax.fori_loop(0, nh // step, epilogue, 0)


def _hyena_fftconv(vf, x0c, gate, f1k, a1k, g, ginv, hre, him, bias, tc=FFTCONV_CHANNELS):
    b, l, c = vf.shape
    nh, sub = DFT_N1 // 2, SUBLANES
    ng = DFT_N2 // sub
    kb = g.shape[0]
    n2 = DFT_N2
    rows = pl.BlockSpec((1, l, tc), lambda j, i: (i, 0, j))
    return pl.pallas_call(
        _fftconv_kernel,
        grid=(c // tc, b),
        in_specs=[pl.BlockSpec((1, nh, ng, sub, tc), lambda j, i: (i, 0, 0, 0, j)), rows, rows,
                  _resident(f1k.shape, lambda j, i: (0, 0)), _resident(a1k.shape, lambda j, i: (0, 0)),
                  _resident(g.shape, lambda j, i: (0, 0, 0)), _resident(ginv.shape, lambda j, i: (0, 0, 0)),
                  _resident((kb, n2, tc), lambda j, i: (0, 0, j)), _resident((kb, n2, tc), lambda j, i: (0, 0, j)),
                  pl.BlockSpec((1, tc), lambda j, i: (0, j))],
        out_specs=rows,
        out_shape=jax.ShapeDtypeStruct((b, l, c), BF16),
        scratch_shapes=[pltpu.VMEM((DFT_N1, ng, sub, tc), F32), pltpu.VMEM((nh, ng, sub, tc), F32)],
        compiler_params=_cparams("arbitrary", "arbitrary"),
        name="hyena_fftconv",
    )(vf.reshape(b, nh, ng, sub, c), x0c, gate, f1k, a1k, g, ginv, hre, him, bias)


@functools.lru_cache(maxsize=None)
def _dft_tables():
    n1, n2, k1 = DFT_N1, DFT_N2, DFT_K1
    n = n1 * n2
    nh = n1 // 2
    ang = 2.0 * np.pi * np.arange(k1)[:, None] * np.arange(nh)[None, :] / n1
    fre, fim = np.cos(ang), -np.sin(ang)
    m = np.arange(n2)[None, None, :]
    k2 = np.arange(n2)[None, :, None]
    ka = np.arange(k1)[:, None, None]
    ph = -2.0 * np.pi * m * (ka / n + k2 / n2)
    gr, gi = np.cos(ph), np.sin(ph)
    g = np.concatenate([np.concatenate([gr, -gi], axis=2), np.concatenate([gi, gr], axis=2)], axis=1)
    grt, git = np.swapaxes(gr, 1, 2), np.swapaxes(gi, 1, 2)
    ginv = np.concatenate([np.concatenate([grt, git], axis=2), np.concatenate([-git, grt], axis=2)], axis=1)
    wk = np.where((np.arange(k1) == 0) | (np.arange(k1) == n1 // 2), 1.0, 2.0)[None, :]
    are, aim = wk * fre.T / n, wk * fim.T / n
    eye = np.eye(SUBLANES)
    f1k = np.kron(np.concatenate([fre, fim[1:k1 - 1]], axis=0), eye)
    a1k = np.kron(np.concatenate([are, aim[:, 1:k1 - 1]], axis=1), eye)
    return tuple(np.asarray(a, np.float32).astype(BF16) for a in (g, ginv, f1k, a1k))


def _position_features(l):
    f32 = F32
    pos = jnp.arange(l, dtype=f32)[:, None]
    t = pos / max(l - 1, 1)
    bands = (HY_EMB - 1) // 2
    f = jnp.linspace(1e-4, bands - 1, bands, dtype=f32)[None, :]
    ang = f * pos * (2.0 * math.pi / l)
    z = jnp.concatenate([t, jnp.cos(ang), -jnp.sin(ang)], axis=-1)
    return jnp.pad(z, ((0, 0), (0, LANES - HY_EMB)))


def _pad2(a, rows, cols):
    return jnp.pad(a, ((0, rows - a.shape[0]), (0, cols - a.shape[1])))


def _hyena_spectrum(l, c, fw1, fb1, fw2, fb2, fw3, freq):
    g, _, f1k, _ = _dft_tables()
    hp = LANES
    zfeat = _position_features(l)
    w1 = _pad2(fw1, LANES, hp)
    w2 = _pad2(fw2, hp, hp)
    w3 = _pad2(fw3, hp, fw3.shape[1])
    row = lambda v: _pad2(v[None, :], 1, hp)
    deltas = jnp.abs(jnp.linspace(math.log(HY_DECAY_TARGET) / HY_LONG_DECAY_PCT,
                                  math.log(HY_DECAY_TARGET) / HY_SHORT_DECAY_PCT, c, dtype=F32))[None, :]
    hfg, nrm = _hyena_filter_taps(zfeat, w1, row(fb1), row(freq), w2, row(fb2), w3, deltas)
    return _filter_spectrum(hfg, jnp.asarray(f1k), jnp.asarray(g), nrm)


def _hyena_branch(x0c, vf, hy_gate, hre, him, bias):
    g, ginv, f1k, a1k = (jnp.asarray(t) for t in _dft_tables())
    return _hyena_fftconv(vf, x0c, hy_gate, f1k, a1k, g, ginv, hre, him, bias[None, :])


def _ssd_kernel(xs_ref, bm_ref, cm_ref, dt_ref, dtb_ref, alog_ref, rep_ref, dsk_ref, mask_ref,
                *rest, reverse, lane0, chunks):
    yo_ref = rest[0] if len(rest) == 3 else None
    y_ref, prev_ref = rest[-2:]
    q, n, hd = SSD_CHUNK, SSD_STATE, SSD_HEAD_DIM
    gw = (SSD_HEADS // SSD_GROUPS) * hd

    @pl.when(pl.program_id(1) == 0)
    def _():
        prev_ref[...] = jnp.zeros_like(prev_ref)

    mask = mask_ref[...]
    maskb = mask > 0
    neg_a = -jnp.exp(alog_ref[...])
    lo_lanes = lax.broadcasted_iota(jnp.int32, (1, LANES), 1) < hd
    d = functools.partial(jnp.dot, preferred_element_type=F32)
    npair = gw // LANES
    edge = 0 if reverse else q - 1

    def per_pair(v, c0, shape):
        b0 = jnp.broadcast_to(v[:, c0:c0 + 1], shape)
        b1 = jnp.broadcast_to(v[:, c0 + 1:c0 + 2], shape)
        return jnp.where(lo_lanes, b0, b1)

    order = range(chunks - 1, -1, -1) if reverse else range(chunks)
    for k in order:
        rows = pl.ds(k * q, q)
        xs_b = xs_ref[0, rows, :]
        bm = bm_ref[0, rows, :]
        cm = cm_ref[0, rows, :]
        dt = _softplus(dt_ref[0, rows, :] + dtb_ref[...])
        cum = _sel_dot(mask, dt * neg_a)
        src_t = (cum - jnp.log(dt)).T
        cum_e = cum[edge:edge + 1, :]
        e_in = jnp.exp(cum)
        w_out = dt * jnp.exp(cum_e - cum)
        e_all = jnp.exp(cum_e)
        spread = d(jnp.concatenate([e_in, w_out], axis=0).astype(BF16), rep_ref[...])
        e_x = spread[:q]
        xsd_all = xs_b * spread[q:].astype(BF16)
        ys = []
        for g in range(SSD_GROUPS):
            cg = cm[:, g * n:(g + 1) * n].astype(BF16)
            bg = bm[:, g * n:(g + 1) * n].astype(BF16)
            cb = lax.dot_general(cg, bg, (((1,), (1,)), ((), ())), preferred_element_type=F32)
            sl = slice(g * gw, (g + 1) * gw)
            prev_g = prev_ref[:, sl]
            y_off = d(cg, prev_g.astype(BF16))
            cds = []
            for pair in range(npair):
                lo = g * gw + pair * LANES
                c0 = lane0 + lo // hd
                ms = []
                for c in (c0, c0 + 1):
                    seg_dt = jnp.exp(jnp.where(maskb, cum[:, c:c + 1] - src_t[c:c + 1, :], -jnp.inf))
                    ms.append((cb * seg_dt).astype(BF16))
                xp = xs_b[:, lo:lo + LANES]
                zero = jnp.zeros_like(xp)
                rhs = jnp.concatenate([jnp.where(lo_lanes, xp, zero), jnp.where(lo_lanes, zero, xp)], axis=0)
                yd = d(jnp.concatenate(ms, axis=1), rhs)
                ys.append(yd + y_off[:, pair * LANES:(pair + 1) * LANES] * e_x[:, lo:lo + LANES])
                cds.append(per_pair(e_all, c0, (n, LANES)))
            st = lax.dot_general(bg, xsd_all[:, sl], (((0,), (0,)), ((), ())), preferred_element_type=F32)
            prev_ref[:, sl] = prev_g * jnp.concatenate(cds, axis=1) + st
        y = jnp.concatenate(ys, axis=1)
        if not reverse:
            y = y + xs_b.astype(F32) * dsk_ref[...]
        if yo_ref is not None:
            y = y + yo_ref[0, rows, :].astype(F32)
        y_ref[0, rows, :] = y.astype(y_ref.dtype)


def _ssd_direction(xbc, dt_raw, dtb, alog, rep, dskip, mask, y_other=None, *, reverse, ts=SSD_ROWS):
    b, l, _ = xbc.shape
    hp = SSD_HEADS * SSD_HEAD_DIM
    gn = SSD_GROUPS * SSD_STATE
    nt = l // ts
    tmap = (lambda j: nt - 1 - j) if reverse else (lambda j: j)
    full = lambda shape: pl.BlockSpec(shape, lambda i, j: (0,) * len(shape))
    yspec = pl.BlockSpec((1, ts, hp), lambda i, j: (i, tmap(j), 0))
    extra = [] if y_other is None else [y_other]
    return pl.pallas_call(
        functools.partial(_ssd_kernel, reverse=reverse, lane0=SSD_HEADS if reverse else 0,
                          chunks=ts // SSD_CHUNK),
        grid=(b, nt),
        in_specs=[yspec,
                  pl.BlockSpec((1, ts, gn), lambda i, j: (i, tmap(j), hp // gn)),
                  pl.BlockSpec((1, ts, gn), lambda i, j: (i, tmap(j), hp // gn + 1)),
                  pl.BlockSpec((1, ts, LANES), lambda i, j: (i, tmap(j), 0)),
                  full((1, LANES)), full((1, LANES)), full((LANES, hp)), full((1, hp)),
                  full((SSD_CHUNK, SSD_CHUNK))] + [yspec] * len(extra),
        out_specs=yspec,
        out_shape=jax.ShapeDtypeStruct((b, l, hp), BF16),
        scratch_shapes=[pltpu.VMEM((SSD_STATE, hp), F32)],
        compiler_params=_cparams("parallel", "arbitrary"),
        name="ssd_bwd" if reverse else "ssd_fwd",
    )(xbc, xbc, xbc, dt_raw, dtb, alog, rep, dskip, mask, *extra)


def _ssd_branch(xc, dt_raw, dt_bias, a_log, d_skip):
    pad_row = lambda v: _pad2(v.reshape(1, -1), 1, LANES)
    dtb, alog = pad_row(dt_bias), pad_row(a_log)
    dskip = jnp.repeat(d_skip, SSD_HEAD_DIM)[None, :]
    lane_head = np.arange(SSD_HEADS * SSD_HEAD_DIM)[None, :] // SSD_HEAD_DIM
    src = np.arange(LANES)[:, None]
    idx = np.arange(SSD_CHUNK)
    y = None
    for reverse in (False, True):
        rep = jnp.asarray(src == lane_head + (SSD_HEADS if reverse else 0), dtype=BF16)
        allowed = (idx[None, :] >= idx[:, None]) if reverse else (idx[None, :] <= idx[:, None])
        mask = jnp.asarray(allowed, dtype=BF16)
        y = _ssd_direction(xc, dt_raw, dtb, alog, rep, dskip, mask, y, reverse=reverse)
    return y


_LRU_ROWS = 256


def _lru_kernel(*refs, l, heads):
    x_refs, (gt_ref, wa_ref, wx_ref, ba_ref, bx_ref, lam_ref, o_ref), seq_refs = (
        refs[:heads], refs[heads:heads + 7], refs[heads + 7:])
    blk = _SEG_BLOCK
    d = functools.partial(jnp.dot, preferred_element_type=F32)

    def rows_segment_order(hd, r):
        x = x_refs[hd].at[0]
        return jnp.concatenate([x[pl.ds(r + base + j, SUBLANES, stride=SUBLANES), :]
                                for base in range(0, _CONV_ROWS, blk) for j in range(SUBLANES)], axis=0)

    e0 = (-0.5 * LRU_C / math.log(2.0)) * _softplus(-lam_ref[...])
    ba, bx = ba_ref[...], bx_ref[...]
    a_refs, u_refs, h_refs = ([[seq_refs[(q * 2 + dr) * heads + hd] for hd in range(heads)] for dr in range(2)]
                              for q in range(3))

    def gates(i, carry):
        r = pl.multiple_of(i * _CONV_ROWS, _CONV_ROWS)
        for hd in range(heads):
            ls = slice(hd * LRU_BLOCK, (hd + 1) * LRU_BLOCK)
            xh = rows_segment_order(hd, r)
            xb = xh.astype(BF16)
            xh2 = 0.5 * xh
            for dr in range(2):
                tr = jnp.tanh(d(xb, wa_ref[dr, hd]) + ba[dr:dr + 1, ls])
                ti = jnp.tanh(d(xb, wx_ref[dr, hd]) + bx[dr:dr + 1, ls])
                e = e0[dr:dr + 1, ls]
                a = jnp.exp2(e + e * tr)
                s = 1.0 - a * a
                root = jnp.where(s > 0.0, s * lax.rsqrt(s), 0.0)
                u = (xh2 + xh2 * ti) * root
                a_refs[dr][hd][pl.ds(r, _CONV_ROWS), :] = a
                u_refs[dr][hd][pl.ds(r, _CONV_ROWS), :] = u
        return carry

    lax.fori_loop(0, l // _CONV_ROWS, gates, 0, unroll=4)

    row = lax.broadcasted_iota(jnp.int32, (SUBLANES, LRU_BLOCK), 0)
    nblk = l // blk

    def sublane_scan(a, u, reverse):
        for s in (1, 2, 4):
            sh = (SUBLANES - s) if reverse else s
            valid = (row < SUBLANES - s) if reverse else (row >= s)
            a_sh = jnp.where(valid, pltpu.roll(a, sh, axis=0), 1.0)
            u_sh = jnp.where(valid, pltpu.roll(u, sh, axis=0), 0.0)
            u = u + a * u_sh
            a = a * a_sh
        return a, u

    def block_scan(a_ref, u_ref, h_ref, r, h_in, reverse):
        order = range(SUBLANES - 1, -1, -1) if reverse else range(SUBLANES)
        hs, ps = {}, {}
        h = p = None
        for k in order:
            rows = pl.ds(r + SUBLANES * k, SUBLANES)
            a, u = a_ref[rows, :], u_ref[rows, :]
            h = u if h is None else a * h + u
            p = a if p is None else a * p
            hs[k], ps[k] = h, p
        sp, se = sublane_scan(p, h, reverse)
        full = se + sp * h_in
        edge = 0 if reverse else SUBLANES - 1
        h_out = jnp.broadcast_to(full[edge:edge + 1, :], (SUBLANES, LRU_BLOCK))
        if reverse:
            carry = jnp.where(row == SUBLANES - 1, h_in, pltpu.roll(full, SUBLANES - 1, axis=0))
        else:
            carry = jnp.where(row == 0, h_in, pltpu.roll(full, 1, axis=0))
        for k in order:
            h_ref[pl.ds(r + k, SUBLANES, stride=SUBLANES), :] = hs[k] + ps[k] * carry
        return h_out

    def scan(i, carry):
        rf = pl.multiple_of(i * blk, blk)
        rb = pl.multiple_of((nblk - 1 - i) * blk, blk)
        out = []
        for hd in range(heads):
            hf, hb = carry[2 * hd], carry[2 * hd + 1]
            out.append(block_scan(a_refs[0][hd], u_refs[0][hd], h_refs[0][hd], rf, hf, False))
            out.append(block_scan(a_refs[1][hd], u_refs[1][hd], h_refs[1][hd], rb, hb, True))
        return tuple(out)

    zero = jnp.zeros((SUBLANES, LRU_BLOCK), F32)
    lax.fori_loop(0, nblk, scan, (zero,) * (2 * heads), unroll=4)

    def combine(i, carry):
        r = pl.multiple_of(i * _LRU_ROWS, _LRU_ROWS)
        rows = pl.ds(r, _LRU_ROWS)
        h = jnp.concatenate([h_refs[0][hd][rows, :] + h_refs[1][hd][rows, :] for hd in range(heads)], axis=1)
        o_ref[0, rows, :] = (h * gt_ref[0, rows, :].astype(F32)).astype(o_ref.dtype)
        return carry

    lax.fori_loop(0, l // _LRU_ROWS, combine, 0)


def _rglru_mixer(xc, gate, w_a, w_x, b_a, b_x, lam, heads=LRU_HEADS_PER_STEP):
    b, l, w = xc.shape
    tw = heads * LRU_BLOCK
    gspec = pl.BlockSpec((1, l, tw), lambda i, j: (i, 0, j))
    xspecs = [pl.BlockSpec((1, l, LRU_BLOCK), lambda i, j, hd=hd: (i, 0, j * heads + hd)) for hd in range(heads)]
    wspec = pl.BlockSpec((2, heads, LRU_BLOCK, LRU_BLOCK), lambda i, j: (0, j, 0, 0))
    vspec = pl.BlockSpec((2, tw), lambda i, j: (0, j))
    seqs = [pltpu.VMEM((l, LRU_BLOCK), F32)] * (3 * 2 * heads)
    return pl.pallas_call(
        functools.partial(_lru_kernel, l=l, heads=heads),
        grid=(b, w // tw),
        in_specs=xspecs + [gspec, wspec, wspec, vspec, vspec, vspec],
        out_specs=gspec,
        out_shape=jax.ShapeDtypeStruct((b, l, w), BF16),
        scratch_shapes=seqs,
        compiler_params=_cparams("parallel", "parallel"),
        name="rglru",
    )(*([xc] * heads), gate, (0.5 * w_a).astype(BF16), (0.5 * w_x).astype(BF16), 0.5 * b_a, 0.5 * b_x, lam)


def _rms(x, g, eps):
    return (x * lax.rsqrt(jnp.mean(x * x, axis=-1, keepdims=True) + eps)) * g


def _outproj_even_kernel(*refs, n_x, bp):
    x_refs, (ya_ref, ys_ref, z_ref, gm_ref, ng_ref, wa_ref, wb_ref, o_ref) = refs[:n_x], refs[n_x:]
    yb = ys_ref[0].astype(F32) * z_ref[0].astype(F32)
    ng = ng_ref[...]
    gw = yb.shape[1] // SSD_GROUPS
    parts = [_rms(yb[:, g * gw:(g + 1) * gw], ng[:, g * gw:(g + 1) * gw], 1e-5) for g in range(SSD_GROUPS)]
    ybn = jnp.concatenate(parts, axis=1)
    d = functools.partial(jnp.dot, preferred_element_type=F32)
    acc = d(ya_ref[0].astype(BF16), wa_ref[...]) + d(ybn.astype(BF16), wb_ref[...])
    o_ref[0] = _load_rows(x_refs, bp) + gm_ref[0] * acc


def _out_projection_even(x, ya, ys, z, gate_mod, norm_g, w_out, tl=OUT_PROJ_ROWS):
    b, l, c = ya.shape
    dm = x[0].shape[2]
    rows = lambda n: pl.BlockSpec((1, tl, n), lambda i, j: (i, j, 0))
    wa, wb = w_out[:c].astype(BF16), w_out[c:].astype(BF16)
    return pl.pallas_call(
        functools.partial(_outproj_even_kernel, n_x=len(x), bp=x[0].shape[0]),
        grid=(b, l // tl),
        in_specs=_row_specs(x, tl, l // tl) + [rows(c), rows(c), rows(c),
                                      pl.BlockSpec((1, 1, dm), lambda i, j: (i, 0, 0)),
                                      pl.BlockSpec((1, c), lambda i, j: (0, 0)),
                                      _resident(wa.shape, lambda i, j: (0, 0)),
                                      _resident(wb.shape, lambda i, j: (0, 0))],
        out_specs=rows(dm),
        out_shape=jax.ShapeDtypeStruct((b, l, dm), F32),
        compiler_params=_cparams("arbitrary", "arbitrary"),
        name="out_proj_even",
    )(*x, ya, ys, z, gate_mod, norm_g, wa, wb)


def _outproj_odd_kernel(x_ref, y_ref, gm_ref, w_ref, fg_ref, *o_refs, bp):
    acc = jnp.dot(y_ref[0].astype(BF16), w_ref[...], preferred_element_type=F32)
    xn = x_ref[0] + gm_ref[0] * acc
    if len(o_refs) == 1:
        o_refs[0][0] = xn
        return
    out = _rms(xn, fg_ref[...], 1e-6)

    @pl.when(pl.program_id(0) < bp)
    def _():
        o_refs[0][0] = out

    @pl.when(pl.program_id(0) >= bp)
    def _():
        o_refs[1][0] = out


def _out_projection_odd(x, y, gate_mod, w_out, final_g, split_at=None, tl=OUT_PROJ_ROWS):
    b, l, dm = x.shape
    w = y.shape[2]
    rows = lambda n: pl.BlockSpec((1, tl, n), lambda i, j: (i, j, 0))
    wb = w_out.astype(BF16)
    if split_at is None:
        out_specs = [rows(dm)]
        out_shape = [jax.ShapeDtypeStruct((b, l, dm), F32)]
    else:
        out_specs = [pl.BlockSpec((1, tl, dm), m) for m in _pair_row_maps(split_at, l // tl)]
        out_shape = [jax.ShapeDtypeStruct((n, l, dm), F32) for n in (split_at, b - split_at)]
    out = pl.pallas_call(
        functools.partial(_outproj_odd_kernel, bp=split_at),
        grid=(b, l // tl),
        in_specs=[rows(dm), rows(w),
                  pl.BlockSpec((1, 1, dm), lambda i, j: (i, 0, 0)),
                  _resident(wb.shape, lambda i, j: (0, 0)),
                  pl.BlockSpec((1, dm), lambda i, j: (0, 0))],
        out_specs=out_specs,
        out_shape=out_shape,
        compiler_params=_cparams("arbitrary", "arbitrary"),
        name="out_proj_odd",
    )(x, y, gate_mod, wb, final_g)
    return out[0] if split_at is None else tuple(out)


def kernel(x_prompt, x_sample, c_prompt, c_sample, mod_w, mod_b, norm_g, final_g, ev_w_in, ev_w_out, hy_conv_w, hy_conv_b, hy_fw1, hy_fb1, hy_fw2, hy_fb2, hy_fw3, hy_freq, hy_bias, ssd_conv_w, ssd_conv_b, ssd_dt_bias, ssd_A_log, ssd_D, ssd_norm_g, od_w_in, od_w_out, lru_conv_w, lru_conv_b, lru_w_a, lru_b_a, lru_w_x, lru_b_x, lru_lam):
    bp, l, dm = x_prompt.shape
    assert x_sample.shape[1:] == (l, dm), "the two request groups are stacked on the batch axis"
    x = (x_prompt, x_sample)
    c = jnp.concatenate([c_prompt, c_sample], axis=0)
    b = c.shape[0]
    depth = mod_w.shape[0]
    assert 2 * l == DFT_N1 * DFT_N2
    assert depth % 2 == 0, "the final norm is fused into the last (odd) layer's output projection"

    b_pad = -(-b // SUBLANES) * SUBLANES
    mods = _modulation(jnp.pad(c, ((0, b_pad - b), (0, 0))), mod_w, mod_b)[:, :b]
    shift, scale, gate_mod = (mods[:, :, k * dm:(k + 1) * dm].reshape(depth, b, 1, dm) for k in range(3))

    hyc = hy_bias.shape[1]
    ssw = ssd_norm_g.shape[1]
    xbw = ssd_conv_w.shape[2]
    o1, o2, o3, o4 = 3 * hyc, 4 * hyc, 4 * hyc + ssw, 4 * hyc + ssw + xbw

    for i in range(depth):
        j = i // 2
        g = norm_g[i][None, :]
        if i % 2 == 0:
            w_in = ev_w_in[j].astype(BF16)
            w_dt = jnp.pad(w_in[:, o4:], ((0, 0), (0, LANES - (w_in.shape[1] - o4))))
            x0c, vf, hyg, z, xc, dt_raw = _in_projection(x, shift[i], scale[i], g, [
                dict(w=w_in[:, :o1], dtype=(BF16, F32), post="hyena", conv_w=hy_conv_w[j],
                     conv_b=hy_conv_b[j][None, :]),
                dict(w=w_in[:, o1:o2], dtype=BF16, post="silu"),
                dict(w=w_in[:, o2:o3], dtype=BF16, post="silu"),
                dict(w=w_in[:, o3:o4], dtype=BF16, post="conv_silu", conv_w=ssd_conv_w[j],
                     conv_b=ssd_conv_b[j][None, :]),
                dict(w=w_dt, dtype=F32)])
            hre, him = _hyena_spectrum(l, hyc, hy_fw1[j], hy_fb1[j], hy_fw2[j], hy_fb2[j], hy_fw3[j], hy_freq[j])
            ya = _hyena_branch(x0c, vf, hyg, hre, him, hy_bias[j])
            ys = _ssd_branch(xc, dt_raw, ssd_dt_bias[j], ssd_A_log[j], ssd_D[j])
            x = (_out_projection_even(x, ya, ys, z, gate_mod[i], ssd_norm_g[j][None, :], ev_w_out[j]),)
        else:
            w_in = od_w_in[j].astype(BF16)
            half = w_in.shape[1] // 2
            xc, gt = _in_projection(x, shift[i], scale[i], g, [
                dict(w=w_in[:, :half], dtype=F32, post="conv", conv_w=lru_conv_w[j], conv_b=lru_conv_b[j][None, :]),
                dict(w=w_in[:, half:], dtype=BF16, post="silu")])
            y = _rglru_mixer(xc, gt, lru_w_a[j], lru_w_x[j], lru_b_a[j], lru_b_x[j], lru_lam[j])
            if i == depth - 1:
                return _out_projection_odd(x[0], y, gate_mod[i], od_w_out[j], final_g[None, :], split_at=bp)
            x = (_out_projection_odd(x[0], y, gate_mod[i], od_w_out[j], final_g[None, :]),)
```

```python
import functools
import math

import numpy as np
import jax
import jax.numpy as jnp
from jax import lax
from jax.experimental import pallas as pl
from jax.experimental.pallas import tpu as pltpu

F32 = jnp.float32
BF16 = jnp.bfloat16
HI = lax.Precision.HIGHEST

V7X_VMEM_BYTES = 64 * 1024 * 1024
VMEM_LIMIT = V7X_VMEM_BYTES - 8 * 1024 * 1024
LANES = 128
SUBLANES = 8

IN_PROJ_ROWS = 512
OUT_PROJ_ROWS = 1024
SSD_ROWS = 1024
FFTCONV_CHANNELS = 256
FILTER_CHANNELS = 128
FILTER_ROWS = 512
LRU_HEADS_PER_STEP = 2

HY_EMB = 33
HY_SHORT_DECAY_PCT = 0.3
HY_LONG_DECAY_PCT = 1.5
HY_DECAY_TARGET = 1e-2
SSD_HEADS = 16
SSD_HEAD_DIM = 64
SSD_GROUPS = 4
SSD_STATE = 128
SSD_CHUNK = 128
LRU_BLOCK = 128
LRU_C = 8.0

DFT_N1 = 64
DFT_N2 = 128
DFT_K1 = DFT_N1 // 2 + 1


def _cparams(*sem):
    return pltpu.CompilerParams(dimension_semantics=sem, vmem_limit_bytes=VMEM_LIMIT)


def _resident(shape, index_map):
    return pl.BlockSpec(shape, index_map, pipeline_mode=pl.Buffered(1))


def _sigmoid(x):
    return 0.5 + 0.5 * jnp.tanh(0.5 * x)


def _silu(x):
    return x * _sigmoid(x)


def _softplus(x):
    return jnp.maximum(x, 0.0) + jnp.log1p(jnp.exp(-jnp.abs(x)))


def _norm_modulate(x, g, shift, scale):
    ms = jnp.mean(x * x, axis=-1, keepdims=True)
    return ((x * lax.rsqrt(ms + 1e-6)) * g) * (1.0 + scale) + shift


def _split3(x):
    hi = x.astype(BF16)
    r1 = x - hi.astype(F32)
    mid = r1.astype(BF16)
    lo = (r1 - mid.astype(F32)).astype(BF16)
    return hi, mid, lo


def _sel_dot(sel, x):
    hi, mid, lo = _split3(x)
    d = functools.partial(jnp.dot, preferred_element_type=F32)
    return d(sel, hi) + d(sel, mid) + d(sel, lo)


def _mod_kernel(c_ref, w_ref, b_ref, o_ref):
    cs = _silu(c_ref[...])
    o_ref[0] = jnp.dot(cs, w_ref[0], precision=HI, preferred_element_type=F32) + b_ref[0]


def _modulation(c_pad, mod_w, mod_b):
    depth, d, d3 = mod_w.shape
    bp = c_pad.shape[0]
    return pl.pallas_call(
        _mod_kernel,
        grid=(depth, d3 // d),
        in_specs=[
            pl.BlockSpec((bp, d), lambda i, j: (0, 0)),
            pl.BlockSpec((1, d, d), lambda i, j: (i, 0, j)),
            pl.BlockSpec((1, 1, d), lambda i, j: (i, 0, j)),
        ],
        out_specs=pl.BlockSpec((1, bp, d), lambda i, j: (i, 0, j)),
        out_shape=jax.ShapeDtypeStruct((depth, bp, d3), F32),
        compiler_params=_cparams("arbitrary", "arbitrary"),
        name="adaln_mod",
    )(c_pad, mod_w, mod_b.reshape(depth, 1, d3))


def _pair_row_maps(bp, nj):
    first = lambda i, j: (jnp.minimum(i, bp - 1), jnp.where(i < bp, j, nj - 1), 0)
    second = lambda i, j: (jnp.maximum(i - bp, 0), jnp.where(i < bp, 0, j), 0)
    return first, second


def _load_rows(x_refs, bp):
    if len(x_refs) == 1:
        return x_refs[0][0]
    return jnp.where(pl.program_id(0) < bp, x_refs[0][0], x_refs[1][0])


def _row_specs(xs, rows, nj, f=lambda j: j):
    d = xs[0].shape[2]
    if len(xs) == 1:
        return [pl.BlockSpec((1, rows, d), lambda i, j: (i, f(j), 0))]
    bp = xs[0].shape[0]
    first = lambda i, j: (jnp.minimum(i, bp - 1), jnp.where(i < bp, f(j), f(nj - 1)), 0)
    second = lambda i, j: (jnp.maximum(i - bp, 0), jnp.where(i < bp, f(0), f(j)), 0)
    return [pl.BlockSpec((1, rows, d), first), pl.BlockSpec((1, rows, d), second)]


_PROJ_GROUP = 2 * LANES


_CONV_POSTS = ("conv", "conv_silu", "hyena")


def _inproj_kernel(*refs, n_x, bp, plan, tl):
    refs = list(refs)
    take = lambda n: [refs.pop(0) for _ in range(n)]
    has_conv = any(post in _CONV_POSTS for post, _, _ in plan)
    x_main = take(n_x)
    x_prev, x_next = (take(n_x), take(n_x)) if has_conv else (None, None)
    sh_ref, sc_ref, g_ref = take(3)
    piece_refs = [take(3 if post in _CONV_POSTS else 1) for post, _, _ in plan]
    out_refs = [take(n_out) for _, _, n_out in plan]
    pad_ref = refs.pop(0) if has_conv else None

    j, last = pl.program_id(1), pl.num_programs(1) - 1
    norm = lambda x: _norm_modulate(x, g_ref[...], sh_ref[0], sc_ref[0])
    d = functools.partial(jnp.dot, preferred_element_type=F32)
    hn = norm(_load_rows(x_main, bp))
    hb = hn.astype(BF16)
    if has_conv:
        hb_ext = jnp.concatenate([norm(_load_rows(x_prev, bp)), hn, norm(_load_rows(x_next, bp))],
                                 axis=0).astype(BF16)
    slabs = _PROJ_GROUP // LANES

    def project_to_pad(w_ref, col, pad):
        u = d(hb_ext, w_ref[:, col:col + _PROJ_GROUP])
        for s in range(slabs):
            ls = slice(s * LANES, (s + 1) * LANES)
            pad[s, pl.ds(0, _HALO), :] = jnp.where(j > 0, u[:_HALO, ls], 0.0)
            pad[s, pl.ds(_HALO, tl), :] = u[_HALO:_HALO + tl, ls]
            pad[s, pl.ds(_HALO + tl, _HALO), :] = jnp.where(j < last, u[_HALO + tl:, ls], 0.0)

    def conv_from_pad(pad, cw_ref, cb_ref, col, taps, r):
        parts = [_conv_rows(pad.at[s], r, cw_ref[:, col + s * LANES:col + (s + 1) * LANES],
                            cb_ref[:, col + s * LANES:col + (s + 1) * LANES], taps, taps // 2)
                 for s in range(slabs)]
        return jnp.concatenate(parts, axis=1)

    group = 0
    for (post, taps, _), p_refs, o_refs in zip(plan, piece_refs, out_refs):
        w_ref = p_refs[0]
        if post not in _CONV_POSTS:
            u = d(hb, w_ref[...])
            o_refs[0][0] = (_silu(u) if post == "silu" else u).astype(o_refs[0].dtype)
            continue
        cw_ref, cb_ref = p_refs[1:]
        parts = 3 if post == "hyena" else 1
        width = w_ref.shape[1] // parts
        for c0 in range(0, width, _PROJ_GROUP):
            pads = pad_ref.at[group % 2]
            group += 1
            for o in range(parts):
                project_to_pad(w_ref, o * width + c0, pads.at[o])
            for r in range(0, tl, _CONV_ROWS):
                rows, cols = pl.ds(r, _CONV_ROWS), slice(c0, c0 + _PROJ_GROUP)
                cv = [conv_from_pad(pads.at[o], cw_ref, cb_ref, o * width + c0, taps, r) for o in range(parts)]
                if post == "hyena":
                    o_refs[0][0, rows, cols] = cv[0].astype(o_refs[0].dtype)
                    o_refs[1][0, rows, cols] = (cv[2] * cv[1]).astype(o_refs[1].dtype)
                else:
                    y = _silu(cv[0]) if post == "conv_silu" else cv[0]
                    o_refs[0][0, rows, cols] = y.astype(o_refs[0].dtype)


def _in_projection(x, shift, scale, g, pieces, tl=IN_PROJ_ROWS):
    b = sum(a.shape[0] for a in x)
    l, d = x[0].shape[1:]
    nj = l // tl
    is_conv = lambda p: p.get("post") in _CONV_POSTS
    outs = []
    plan = []
    for p in pieces:
        n = p["w"].shape[1]
        if p.get("post") == "hyena":
            outs += [(n // 3, p["dtype"][0]), (n // 3, p["dtype"][1])]
        else:
            outs.append((n, p["dtype"]))
        plan.append((p.get("post"), p["conv_w"].shape[0] if is_conv(p) else 0, 2 if p.get("post") == "hyena" else 1))
    has_conv = any(is_conv(p) for p in pieces)
    const = lambda a: _resident(a.shape, lambda i, j: (0,) * a.ndim)
    in_specs, args = _row_specs(x, tl, nj), list(x)
    if has_conv:
        per = tl // _HALO
        in_specs += _row_specs(x, _HALO, nj, lambda j: jnp.maximum(j * per - 1, 0))
        in_specs += _row_specs(x, _HALO, nj, lambda j: jnp.minimum((j + 1) * per, l // _HALO - 1))
        args += list(x) * 2
    in_specs += [pl.BlockSpec((1, 1, d), lambda i, j: (i, 0, 0)), pl.BlockSpec((1, 1, d), lambda i, j: (i, 0, 0)),
                 pl.BlockSpec((1, d), lambda i, j: (0, 0))]
    args += [shift, scale, g]
    for p in pieces:
        for a in [p["w"]] + ([p["conv_w"], p["conv_b"]] if is_conv(p) else []):
            in_specs.append(const(a))
            args.append(a)
    scratch = [pltpu.VMEM((2, 3, _PROJ_GROUP // LANES, tl + 2 * _HALO, LANES), F32)] if has_conv else []
    return pl.pallas_call(
        functools.partial(_inproj_kernel, n_x=len(x), bp=x[0].shape[0], plan=tuple(plan), tl=tl),
        grid=(b, nj),
        in_specs=in_specs,
        out_specs=[pl.BlockSpec((1, tl, n), lambda i, j: (i, j, 0)) for n, _ in outs],
        out_shape=[jax.ShapeDtypeStruct((b, l, n), dt) for n, dt in outs],
        scratch_shapes=scratch,
        compiler_params=_cparams("arbitrary", "arbitrary"),
        name="in_proj",
    )(*args)


_CONV_ROWS = 256
_HALO = SUBLANES


def _conv_rows(pad_ref, r, w, bias, taps, left):
    acc = None
    for k in range(taps):
        term = w[k:k + 1, :] * pad_ref[pl.ds(r + _HALO + k - left, _CONV_ROWS), :]
        acc = term if acc is None else acc + term
    return acc + bias


_SEG_BLOCK = SUBLANES * SUBLANES


def _filter_kernel(z_ref, w1_ref, b1_ref, fr_ref, w2_ref, b2_ref, w3_ref, dl_ref,
                   hfg_ref, nrm_ref, *, tr, c):
    i = pl.program_id(0)
    z = z_ref[...]
    fr = fr_ref[...]
    d = functools.partial(jnp.dot, precision=HI, preferred_element_type=F32)
    h = jnp.sin(fr * (d(z, w1_ref[...]) + b1_ref[...]))
    h = jnp.sin(fr * (d(h, w2_ref[...]) + b2_ref[...]))
    h = jnp.dot(h.astype(BF16), w3_ref[...].astype(BF16), preferred_element_type=F32)
    window = jnp.exp(-z[:, 0:1] * dl_ref[...])
    hf = h[:, :c] * window
    row = lax.broadcasted_iota(jnp.int32, (tr, 1), 0) + i * tr
    g = jnp.where(row == 0, 0.0, h[:, c:] * window)
    hfg_ref[0] = hf
    hfg_ref[1] = g
    part = jnp.sum(jnp.abs(hf) + jnp.abs(g), axis=0, keepdims=True)

    @pl.when(i == 0)
    def _():
        nrm_ref[...] = part

    @pl.when(i != 0)
    def _():
        nrm_ref[...] = nrm_ref[...] + part


def _hyena_filter_taps(zfeat, w1, b1, freq, w2, b2, w3, deltas, tr=FILTER_ROWS):
    l, zp = zfeat.shape
    hp = w1.shape[1]
    c2 = w3.shape[1]
    c = c2 // 2
    full = lambda shape: pl.BlockSpec(shape, lambda i: (0,) * len(shape))
    return pl.pallas_call(
        functools.partial(_filter_kernel, tr=tr, c=c),
        grid=(l // tr,),
        in_specs=[pl.BlockSpec((tr, zp), lambda i: (i, 0)), full((zp, hp)), full((1, hp)),
                  full((1, hp)), full((hp, hp)), full((1, hp)), full((hp, c2)), full((1, c))],
        out_specs=[pl.BlockSpec((2, tr, c), lambda i: (0, i, 0)), full((1, c))],
        out_shape=[jax.ShapeDtypeStruct((2, l, c), F32), jax.ShapeDtypeStruct((1, c), F32)],
        compiler_params=_cparams("arbitrary"),
        name="hyena_filter_mlp",
    )(zfeat, w1, b1, freq, w2, b2, w3, deltas)


def _filter_spec_kernel(h_ref, f1_ref, g_ref, nrm_ref, hre_ref, him_ref, s_ref):
    _, nh, ng, sub, tc = h_ref.shape
    n2 = ng * sub
    nreal = s_ref.shape[1]
    k1 = nreal // 2 + 1
    d = functools.partial(jnp.dot, preferred_element_type=F32)
    f1 = f1_ref[...]
    for q in range(2):
        for g in range(ng):
            x = h_ref[q, :, g].reshape(nh * sub, tc).astype(BF16)
            s_ref[q, :, g] = d(f1, x).reshape(nreal, sub, tc)
    inv = 1.0 / nrm_ref[...]
    zero = jnp.zeros((n2, tc), F32)

    def spectrum(k, has_im):
        xs = []
        for q in range(2):
            re = s_ref[q, k].reshape(n2, tc)
            im = s_ref[q, k1 - 1 + k].reshape(n2, tc) if has_im else zero
            xs.append(d(g_ref[k], jnp.concatenate([re, im], axis=0).astype(BF16)))
        xf, xg = xs
        hre_ref[k] = (xf[:n2] + xg[:n2]) * inv
        him_ref[k] = (xf[n2:] - xg[n2:]) * inv

    spectrum(0, False)
    spectrum(k1 - 1, False)

    def body(k, carry):
        spectrum(k, True)
        return carry

    lax.fori_loop(1, k1 - 1, body, 0, unroll=_LEVEL2_UNROLL)


def _filter_spectrum(hfg, f1k, g, nrm, tc=FILTER_CHANNELS):
    _, l, c = hfg.shape
    nh, sub = DFT_N1 // 2, SUBLANES
    ng = DFT_N2 // sub
    k1, n2 = g.shape[0], DFT_N2
    return pl.pallas_call(
        _filter_spec_kernel,
        grid=(c // tc,),
        in_specs=[pl.BlockSpec((2, nh, ng, sub, tc), lambda j: (0, 0, 0, 0, j)),
                  _resident(f1k.shape, lambda j: (0, 0)), _resident(g.shape, lambda j: (0, 0, 0)),
                  pl.BlockSpec((1, tc), lambda j: (0, j))],
        out_specs=[pl.BlockSpec((k1, n2, tc), lambda j: (0, 0, j))] * 2,
        out_shape=[jax.ShapeDtypeStruct((k1, n2, c), F32)] * 2,
        scratch_shapes=[pltpu.VMEM((2, DFT_N1, ng, sub, tc), F32)],
        compiler_params=_cparams("arbitrary"),
        name="hyena_filter_spectrum",
    )(hfg.reshape(2, nh, ng, sub, c), f1k, g, nrm)


_LEVEL2_UNROLL = 16


def _fftconv_kernel(vf_ref, x0_ref, gt_ref, f1_ref, a1_ref, g_ref, gi_ref, hre_ref, him_ref, bias_ref,
                    o_ref, s_ref, y_ref):
    nh, ng, sub, tc = vf_ref.shape[1:]
    n2 = ng * sub
    nreal = s_ref.shape[0]
    k1 = nreal // 2 + 1
    d = functools.partial(jnp.dot, preferred_element_type=F32)

    f1 = f1_ref[...]
    for g in range(ng):
        x = vf_ref[0, :, g].reshape(nh * sub, tc).astype(BF16)
        s_ref[:, g] = d(f1, x).reshape(nreal, sub, tc)

    def level2(k, re, im):
        x = d(g_ref[k], jnp.concatenate([re, im], axis=0).astype(BF16))
        xr, xi = x[:n2], x[n2:]
        hr, hi = hre_ref[k], him_ref[k]
        y = jnp.concatenate([xr * hr - xi * hi, xr * hi + xi * hr], axis=0).astype(BF16)
        return d(gi_ref[k], y)

    zero = jnp.zeros((n2, tc), F32)
    for k in (0, k1 - 1):
        z = level2(k, s_ref[k].reshape(n2, tc), zero)
        s_ref[k] = z[:n2].reshape(ng, sub, tc)

    def body(k, carry):
        z = level2(k, s_ref[k].reshape(n2, tc), s_ref[k1 - 1 + k].reshape(n2, tc))
        s_ref[k] = z[:n2].reshape(ng, sub, tc)
        s_ref[k1 - 1 + k] = z[n2:].reshape(ng, sub, tc)
        return carry

    lax.fori_loop(1, k1 - 1, body, 0, unroll=_LEVEL2_UNROLL)

    a1 = a1_ref[...]
    for g in range(ng):
        z = s_ref[:, g].reshape(nreal * sub, tc).astype(BF16)
        y_ref[:, g] = d(a1, z).reshape(nh, sub, tc)

    bias = bias_ref[...]
    step = 2
    rows = step * n2

    def epilogue(i, carry):
        r = pl.multiple_of(i * rows, rows)
        y = y_ref[pl.ds(i * step, step)].reshape(rows, tc)
        vf = vf_ref[0, pl.ds(i * step, step)].reshape(rows, tc)
        x0 = x0_ref[0, pl.ds(r, rows), :].astype(F32)
        gt = gt_ref[0, pl.ds(r, rows), :].astype(F32)
        o_ref[0, pl.ds(r, rows), :] = ((x0 * (y + vf * bias)) * gt).astype(o_ref.dtype)
        return carry

    lax.fori_loop(0, nh // step, epilogue, 0)


def _hyena_fftconv(vf, x0c, gate, f1k, a1k, g, ginv, hre, him, bias, tc=FFTCONV_CHANNELS):
    b, l, c = vf.shape
    nh, sub = DFT_N1 // 2, SUBLANES
    ng = DFT_N2 // sub
    kb = g.shape[0]
    n2 = DFT_N2
    rows = pl.BlockSpec((1, l, tc), lambda j, i: (i, 0, j))
    return pl.pallas_call(
        _fftconv_kernel,
        grid=(c // tc, b),
        in_specs=[pl.BlockSpec((1, nh, ng, sub, tc), lambda j, i: (i, 0, 0, 0, j)), rows, rows,
                  _resident(f1k.shape, lambda j, i: (0, 0)), _resident(a1k.shape, lambda j, i: (0, 0)),
                  _resident(g.shape, lambda j, i: (0, 0, 0)), _resident(ginv.shape, lambda j, i: (0, 0, 0)),
                  _resident((kb, n2, tc), lambda j, i: (0, 0, j)), _resident((kb, n2, tc), lambda j, i: (0, 0, j)),
                  pl.BlockSpec((1, tc), lambda j, i: (0, j))],
        out_specs=rows,
        out_shape=jax.ShapeDtypeStruct((b, l, c), BF16),
        scratch_shapes=[pltpu.VMEM((DFT_N1, ng, sub, tc), F32), pltpu.VMEM((nh, ng, sub, tc), F32)],
        compiler_params=_cparams("arbitrary", "arbitrary"),
        name="hyena_fftconv",
    )(vf.reshape(b, nh, ng, sub, c), x0c, gate, f1k, a1k, g, ginv, hre, him, bias)


@functools.lru_cache(maxsize=None)
def _dft_tables():
    n1, n2, k1 = DFT_N1, DFT_N2, DFT_K1
    n = n1 * n2
    nh = n1 // 2
    ang = 2.0 * np.pi * np.arange(k1)[:, None] * np.arange(nh)[None, :] / n1
    fre, fim = np.cos(ang), -np.sin(ang)
    m = np.arange(n2)[None, None, :]
    k2 = np.arange(n2)[None, :, None]
    ka = np.arange(k1)[:, None, None]
    ph = -2.0 * np.pi * m * (ka / n + k2 / n2)
    gr, gi = np.cos(ph), np.sin(ph)
    g = np.concatenate([np.concatenate([gr, -gi], axis=2), np.concatenate([gi, gr], axis=2)], axis=1)
    grt, git = np.swapaxes(gr, 1, 2), np.swapaxes(gi, 1, 2)
    ginv = np.concatenate([np.concatenate([grt, git], axis=2), np.concatenate([-git, grt], axis=2)], axis=1)
    wk = np.where((np.arange(k1) == 0) | (np.arange(k1) == n1 // 2), 1.0, 2.0)[None, :]
    are, aim = wk * fre.T / n, wk * fim.T / n
    eye = np.eye(SUBLANES)
    f1k = np.kron(np.concatenate([fre, fim[1:k1 - 1]], axis=0), eye)
    a1k = np.kron(np.concatenate([are, aim[:, 1:k1 - 1]], axis=1), eye)
    return tuple(np.asarray(a, np.float32).astype(BF16) for a in (g, ginv, f1k, a1k))


def _position_features(l):
    f32 = F32
    pos = jnp.arange(l, dtype=f32)[:, None]
    t = pos / max(l - 1, 1)
    bands = (HY_EMB - 1) // 2
    f = jnp.linspace(1e-4, bands - 1, bands, dtype=f32)[None, :]
    ang = f * pos * (2.0 * math.pi / l)
    z = jnp.concatenate([t, jnp.cos(ang), -jnp.sin(ang)], axis=-1)
    return jnp.pad(z, ((0, 0), (0, LANES - HY_EMB)))


def _pad2(a, rows, cols):
    return jnp.pad(a, ((0, rows - a.shape[0]), (0, cols - a.shape[1])))


def _hyena_spectrum(l, c, fw1, fb1, fw2, fb2, fw3, freq):
    g, _, f1k, _ = _dft_tables()
    hp = LANES
    zfeat = _position_features(l)
    w1 = _pad2(fw1, LANES, hp)
    w2 = _pad2(fw2, hp, hp)
    w3 = _pad2(fw3, hp, fw3.shape[1])
    row = lambda v: _pad2(v[None, :], 1, hp)
    deltas = jnp.abs(jnp.linspace(math.log(HY_DECAY_TARGET) / HY_LONG_DECAY_PCT,
                                  math.log(HY_DECAY_TARGET) / HY_SHORT_DECAY_PCT, c, dtype=F32))[None, :]
    hfg, nrm = _hyena_filter_taps(zfeat, w1, row(fb1), row(freq), w2, row(fb2), w3, deltas)
    return _filter_spectrum(hfg, jnp.asarray(f1k), jnp.asarray(g), nrm)


def _hyena_branch(x0c, vf, hy_gate, hre, him, bias):
    g, ginv, f1k, a1k = (jnp.asarray(t) for t in _dft_tables())
    return _hyena_fftconv(vf, x0c, hy_gate, f1k, a1k, g, ginv, hre, him, bias[None, :])


def _ssd_kernel(xs_ref, bm_ref, cm_ref, dt_ref, dtb_ref, alog_ref, rep_ref, dsk_ref, mask_ref,
                *rest, reverse, lane0, chunks):
    yo_ref = rest[0] if len(rest) == 3 else None
    y_ref, prev_ref = rest[-2:]
    q, n, hd = SSD_CHUNK, SSD_STATE, SSD_HEAD_DIM
    gw = (SSD_HEADS // SSD_GROUPS) * hd

    @pl.when(pl.program_id(1) == 0)
    def _():
        prev_ref[...] = jnp.zeros_like(prev_ref)

    mask = mask_ref[...]
    maskb = mask > 0
    neg_a = -jnp.exp(alog_ref[...])
    lo_lanes = lax.broadcasted_iota(jnp.int32, (1, LANES), 1) < hd
    d = functools.partial(jnp.dot, preferred_element_type=F32)
    npair = gw // LANES
    edge = 0 if reverse else q - 1

    def per_pair(v, c0, shape):
        b0 = jnp.broadcast_to(v[:, c0:c0 + 1], shape)
        b1 = jnp.broadcast_to(v[:, c0 + 1:c0 + 2], shape)
        return jnp.where(lo_lanes, b0, b1)

    order = range(chunks - 1, -1, -1) if reverse else range(chunks)
    for k in order:
        rows = pl.ds(k * q, q)
        xs_b = xs_ref[0, rows, :]
        bm = bm_ref[0, rows, :]
        cm = cm_ref[0, rows, :]
        dt = _softplus(dt_ref[0, rows, :] + dtb_ref[...])
        cum = _sel_dot(mask, dt * neg_a)
        src_t = (cum - jnp.log(dt)).T
        cum_e = cum[edge:edge + 1, :]
        e_in = jnp.exp(cum)
        w_out = dt * jnp.exp(cum_e - cum)
        e_all = jnp.exp(cum_e)
        spread = d(jnp.concatenate([e_in, w_out], axis=0).astype(BF16), rep_ref[...])
        e_x = spread[:q]
        xsd_all = xs_b * spread[q:].astype(BF16)
        ys = []
        for g in range(SSD_GROUPS):
            cg = cm[:, g * n:(g + 1) * n].astype(BF16)
            bg = bm[:, g * n:(g + 1) * n].astype(BF16)
            cb = lax.dot_general(cg, bg, (((1,), (1,)), ((), ())), preferred_element_type=F32)
            sl = slice(g * gw, (g + 1) * gw)
            prev_g = prev_ref[:, sl]
            y_off = d(cg, prev_g.astype(BF16))
            cds = []
            for pair in range(npair):
                lo = g * gw + pair * LANES
                c0 = lane0 + lo // hd
                ms = []
                for c in (c0, c0 + 1):
                    seg_dt = jnp.exp(jnp.where(maskb, cum[:, c:c + 1] - src_t[c:c + 1, :], -jnp.inf))
                    ms.append((cb * seg_dt).astype(BF16))
                xp = xs_b[:, lo:lo + LANES]
                zero = jnp.zeros_like(xp)
                rhs = jnp.concatenate([jnp.where(lo_lanes, xp, zero), jnp.where(lo_lanes, zero, xp)], axis=0)
                yd = d(jnp.concatenate(ms, axis=1), rhs)
                ys.append(yd + y_off[:, pair * LANES:(pair + 1) * LANES] * e_x[:, lo:lo + LANES])
                cds.append(per_pair(e_all, c0, (n, LANES)))
            st = lax.dot_general(bg, xsd_all[:, sl], (((0,), (0,)), ((), ())), preferred_element_type=F32)
            prev_ref[:, sl] = prev_g * jnp.concatenate(cds, axis=1) + st
        y = jnp.concatenate(ys, axis=1)
        if not reverse:
            y = y + xs_b.astype(F32) * dsk_ref[...]
        if yo_ref is not None:
            y = y + yo_ref[0, rows, :].astype(F32)
        y_ref[0, rows, :] = y.astype(y_ref.dtype)


def _ssd_direction(xbc, dt_raw, dtb, alog, rep, dskip, mask, y_other=None, *, reverse, ts=SSD_ROWS):
    b, l, _ = xbc.shape
    hp = SSD_HEADS * SSD_HEAD_DIM
    gn = SSD_GROUPS * SSD_STATE
    nt = l // ts
    tmap = (lambda j: nt - 1 - j) if reverse else (lambda j: j)
    full = lambda shape: pl.BlockSpec(shape, lambda i, j: (0,) * len(shape))
    yspec = pl.BlockSpec((1, ts, hp), lambda i, j: (i, tmap(j), 0))
    extra = [] if y_other is None else [y_other]
    return pl.pallas_call(
        functools.partial(_ssd_kernel, reverse=reverse, lane0=SSD_HEADS if reverse else 0,
                          chunks=ts // SSD_CHUNK),
        grid=(b, nt),
        in_specs=[yspec,
                  pl.BlockSpec((1, ts, gn), lambda i, j: (i, tmap(j), hp // gn)),
                  pl.BlockSpec((1, ts, gn), lambda i, j: (i, tmap(j), hp // gn + 1)),
                  pl.BlockSpec((1, ts, LANES), lambda i, j: (i, tmap(j), 0)),
                  full((1, LANES)), full((1, LANES)), full((LANES, hp)), full((1, hp)),
                  full((SSD_CHUNK, SSD_CHUNK))] + [yspec] * len(extra),
        out_specs=yspec,
        out_shape=jax.ShapeDtypeStruct((b, l, hp), BF16),
        scratch_shapes=[pltpu.VMEM((SSD_STATE, hp), F32)],
        compiler_params=_cparams("parallel", "arbitrary"),
        name="ssd_bwd" if reverse else "ssd_fwd",
    )(xbc, xbc, xbc, dt_raw, dtb, alog, rep, dskip, mask, *extra)


def _ssd_branch(xc, dt_raw, dt_bias, a_log, d_skip):
    pad_row = lambda v: _pad2(v.reshape(1, -1), 1, LANES)
    dtb, alog = pad_row(dt_bias), pad_row(a_log)
    dskip = jnp.repeat(d_skip, SSD_HEAD_DIM)[None, :]
    lane_head = np.arange(SSD_HEADS * SSD_HEAD_DIM)[None, :] // SSD_HEAD_DIM
    src = np.arange(LANES)[:, None]
    idx = np.arange(SSD_CHUNK)
    y = None
    for reverse in (False, True):
        rep = jnp.asarray(src == lane_head + (SSD_HEADS if reverse else 0), dtype=BF16)
        allowed = (idx[None, :] >= idx[:, None]) if reverse else (idx[None, :] <= idx[:, None])
        mask = jnp.asarray(allowed, dtype=BF16)
        y = _ssd_direction(xc, dt_raw, dtb, alog, rep, dskip, mask, y, reverse=reverse)
    return y


_LRU_ROWS = 256


def _lru_kernel(*refs, l, heads):
    x_refs, (gt_ref, wa_ref, wx_ref, ba_ref, bx_ref, lam_ref, o_ref), seq_refs = (
        refs[:heads], refs[heads:heads + 7], refs[heads + 7:])
    blk = _SEG_BLOCK
    d = functools.partial(jnp.dot, preferred_element_type=F32)

    def rows_segment_order(hd, r):
        x = x_refs[hd].at[0]
        return jnp.concatenate([x[pl.ds(r + base + j, SUBLANES, stride=SUBLANES), :]
                                for base in range(0, _CONV_ROWS, blk) for j in range(SUBLANES)], axis=0)

    e0 = (-0.5 * LRU_C / math.log(2.0)) * _softplus(-lam_ref[...])
    ba, bx = ba_ref[...], bx_ref[...]
    a_refs, u_refs, h_refs = ([[seq_refs[(q * 2 + dr) * heads + hd] for hd in range(heads)] for dr in range(2)]
                              for q in range(3))

    def gates(i, carry):
        r = pl.multiple_of(i * _CONV_ROWS, _CONV_ROWS)
        for hd in range(heads):
            ls = slice(hd * LRU_BLOCK, (hd + 1) * LRU_BLOCK)
            xh = rows_segment_order(hd, r)
            xb = xh.astype(BF16)
            xh2 = 0.5 * xh
            for dr in range(2):
                tr = jnp.tanh(d(xb, wa_ref[dr, hd]) + ba[dr:dr + 1, ls])
                ti = jnp.tanh(d(xb, wx_ref[dr, hd]) + bx[dr:dr + 1, ls])
                e = e0[dr:dr + 1, ls]
                a = jnp.exp2(e + e * tr)
                s = 1.0 - a * a
                root = jnp.where(s > 0.0, s * lax.rsqrt(s), 0.0)
                u = (xh2 + xh2 * ti) * root
                a_refs[dr][hd][pl.ds(r, _CONV_ROWS), :] = a
                u_refs[dr][hd][pl.ds(r, _CONV_ROWS), :] = u
        return carry

    lax.fori_loop(0, l // _CONV_ROWS, gates, 0, unroll=8)

    row = lax.broadcasted_iota(jnp.int32, (SUBLANES, LRU_BLOCK), 0)
    nblk = l // blk

    def sublane_scan(a, u, reverse):
        for s in (1, 2, 4):
            sh = (SUBLANES - s) if reverse else s
            valid = (row < SUBLANES - s) if reverse else (row >= s)
            a_sh = jnp.where(valid, pltpu.roll(a, sh, axis=0), 1.0)
            u_sh = jnp.where(valid, pltpu.roll(u, sh, axis=0), 0.0)
            u = u + a * u_sh
            a = a * a_sh
        return a, u

    def block_scan(a_ref, u_ref, h_ref, r, h_in, reverse):
        order = range(SUBLANES - 1, -1, -1) if reverse else range(SUBLANES)
        hs, ps = {}, {}
        h = p = None
        for k in order:
            rows = pl.ds(r + SUBLANES * k, SUBLANES)
            a, u = a_ref[rows, :], u_ref[rows, :]
            h = u if h is None else a * h + u
            p = a if p is None else a * p
            hs[k], ps[k] = h, p
        sp, se = sublane_scan(p, h, reverse)
        full = se + sp * h_in
        edge = 0 if reverse else SUBLANES - 1
        h_out = jnp.broadcast_to(full[edge:edge + 1, :], (SUBLANES, LRU_BLOCK))
        if reverse:
            carry = jnp.where(row == SUBLANES - 1, h_in, pltpu.roll(full, SUBLANES - 1, axis=0))
        else:
            carry = jnp.where(row == 0, h_in, pltpu.roll(full, 1, axis=0))
        for k in order:
            h_ref[pl.ds(r + k, SUBLANES, stride=SUBLANES), :] = hs[k] + ps[k] * carry
        return h_out

    def scan(i, carry):
        rf = pl.multiple_of(i * blk, blk)
        rb = pl.multiple_of((nblk - 1 - i) * blk, blk)
        out = []
        for hd in range(heads):
            hf, hb = carry[2 * hd], carry[2 * hd + 1]
            out.append(block_scan(a_refs[0][hd], u_refs[0][hd], h_refs[0][hd], rf, hf, False))
            out.append(block_scan(a_refs[1][hd], u_refs[1][hd], h_refs[1][hd], rb, hb, True))
        return tuple(out)

    zero = jnp.zeros((SUBLANES, LRU_BLOCK), F32)
    lax.fori_loop(0, nblk, scan, (zero,) * (2 * heads), unroll=4)

    def combine(i, carry):
        r = pl.multiple_of(i * _LRU_ROWS, _LRU_ROWS)
        rows = pl.ds(r, _LRU_ROWS)
        h = jnp.concatenate([h_refs[0][hd][rows, :] + h_refs[1][hd][rows, :] for hd in range(heads)], axis=1)
        o_ref[0, rows, :] = (h * gt_ref[0, rows, :].astype(F32)).astype(o_ref.dtype)
        return carry

    lax.fori_loop(0, l // _LRU_ROWS, combine, 0)


def _rglru_mixer(xc, gate, w_a, w_x, b_a, b_x, lam, heads=LRU_HEADS_PER_STEP):
    b, l, w = xc.shape
    tw = heads * LRU_BLOCK
    gspec = pl.BlockSpec((1, l, tw), lambda i, j: (i, 0, j))
    xspecs = [pl.BlockSpec((1, l, LRU_BLOCK), lambda i, j, hd=hd: (i, 0, j * heads + hd)) for hd in range(heads)]
    wspec = pl.BlockSpec((2, heads, LRU_BLOCK, LRU_BLOCK), lambda i, j: (0, j, 0, 0))
    vspec = pl.BlockSpec((2, tw), lambda i, j: (0, j))
    seqs = [pltpu.VMEM((l, LRU_BLOCK), F32)] * (3 * 2 * heads)
    return pl.pallas_call(
        functools.partial(_lru_kernel, l=l, heads=heads),
        grid=(b, w // tw),
        in_specs=xspecs + [gspec, wspec, wspec, vspec, vspec, vspec],
        out_specs=gspec,
        out_shape=jax.ShapeDtypeStruct((b, l, w), BF16),
        scratch_shapes=seqs,
        compiler_params=_cparams("parallel", "parallel"),
        name="rglru",
    )(*([xc] * heads), gate, (0.5 * w_a).astype(BF16), (0.5 * w_x).astype(BF16), 0.5 * b_a, 0.5 * b_x, lam)


def _rms(x, g, eps):
    return (x * lax.rsqrt(jnp.mean(x * x, axis=-1, keepdims=True) + eps)) * g


def _outproj_even_kernel(*refs, n_x, bp):
    x_refs, (ya_ref, ys_ref, z_ref, gm_ref, ng_ref, wa_ref, wb_ref, o_ref) = refs[:n_x], refs[n_x:]
    yb = ys_ref[0].astype(F32) * z_ref[0].astype(F32)
    ng = ng_ref[...]
    gw = yb.shape[1] // SSD_GROUPS
    parts = [_rms(yb[:, g * gw:(g + 1) * gw], ng[:, g * gw:(g + 1) * gw], 1e-5) for g in range(SSD_GROUPS)]
    ybn = jnp.concatenate(parts, axis=1)
    d = functools.partial(jnp.dot, preferred_element_type=F32)
    acc = d(ya_ref[0].astype(BF16), wa_ref[...]) + d(ybn.astype(BF16), wb_ref[...])
    o_ref[0] = _load_rows(x_refs, bp) + gm_ref[0] * acc


def _out_projection_even(x, ya, ys, z, gate_mod, norm_g, w_out, tl=OUT_PROJ_ROWS):
    b, l, c = ya.shape
    dm = x[0].shape[2]
    rows = lambda n: pl.BlockSpec((1, tl, n), lambda i, j: (i, j, 0))
    wa, wb = w_out[:c].astype(BF16), w_out[c:].astype(BF16)
    return pl.pallas_call(
        functools.partial(_outproj_even_kernel, n_x=len(x), bp=x[0].shape[0]),
        grid=(b, l // tl),
        in_specs=_row_specs(x, tl, l // tl) + [rows(c), rows(c), rows(c),
                                      pl.BlockSpec((1, 1, dm), lambda i, j: (i, 0, 0)),
                                      pl.BlockSpec((1, c), lambda i, j: (0, 0)),
                                      _resident(wa.shape, lambda i, j: (0, 0)),
                                      _resident(wb.shape, lambda i, j: (0, 0))],
        out_specs=rows(dm),
        out_shape=jax.ShapeDtypeStruct((b, l, dm), F32),
        compiler_params=_cparams("arbitrary", "arbitrary"),
        name="out_proj_even",
    )(*x, ya, ys, z, gate_mod, norm_g, wa, wb)


def _outproj_odd_kernel(x_ref, y_ref, gm_ref, w_ref, fg_ref, *o_refs, bp):
    acc = jnp.dot(y_ref[0].astype(BF16), w_ref[...], preferred_element_type=F32)
    xn = x_ref[0] + gm_ref[0] * acc
    if len(o_refs) == 1:
        o_refs[0][0] = xn
        return
    out = _rms(xn, fg_ref[...], 1e-6)

    @pl.when(pl.program_id(0) < bp)
    def _():
        o_refs[0][0] = out

    @pl.when(pl.program_id(0) >= bp)
    def _():
        o_refs[1][0] = out


def _out_projection_odd(x, y, gate_mod, w_out, final_g, split_at=None, tl=OUT_PROJ_ROWS):
    b, l, dm = x.shape
    w = y.shape[2]
    rows = lambda n: pl.BlockSpec((1, tl, n), lambda i, j: (i, j, 0))
    wb = w_out.astype(BF16)
    if split_at is None:
        out_specs = [rows(dm)]
        out_shape = [jax.ShapeDtypeStruct((b, l, dm), F32)]
    else:
        out_specs = [pl.BlockSpec((1, tl, dm), m) for m in _pair_row_maps(split_at, l // tl)]
        out_shape = [jax.ShapeDtypeStruct((n, l, dm), F32) for n in (split_at, b - split_at)]
    out = pl.pallas_call(
        functools.partial(_outproj_odd_kernel, bp=split_at),
        grid=(b, l // tl),
        in_specs=[rows(dm), rows(w),
                  pl.BlockSpec((1, 1, dm), lambda i, j: (i, 0, 0)),
                  _resident(wb.shape, lambda i, j: (0, 0)),
                  pl.BlockSpec((1, dm), lambda i, j: (0, 0))],
        out_specs=out_specs,
        out_shape=out_shape,
        compiler_params=_cparams("arbitrary", "arbitrary"),
        name="out_proj_odd",
    )(x, y, gate_mod, wb, final_g)
    return out[0] if split_at is None else tuple(out)


def kernel(x_prompt, x_sample, c_prompt, c_sample, mod_w, mod_b, norm_g, final_g, ev_w_in, ev_w_out, hy_conv_w, hy_conv_b, hy_fw1, hy_fb1, hy_fw2, hy_fb2, hy_fw3, hy_freq, hy_bias, ssd_conv_w, ssd_conv_b, ssd_dt_bias, ssd_A_log, ssd_D, ssd_norm_g, od_w_in, od_w_out, lru_conv_w, lru_conv_b, lru_w_a, lru_b_a, lru_w_x, lru_b_x, lru_lam):
    bp, l, dm = x_prompt.shape
    assert x_sample.shape[1:] == (l, dm), "the two request groups are stacked on the batch axis"
    x = (x_prompt, x_sample)
    c = jnp.concatenate([c_prompt, c_sample], axis=0)
    b = c.shape[0]
    depth = mod_w.shape[0]
    assert 2 * l == DFT_N1 * DFT_N2
    assert depth % 2 == 0, "the final norm is fused into the last (odd) layer's output projection"

    b_pad = -(-b // SUBLANES) * SUBLANES
    mods = _modulation(jnp.pad(c, ((0, b_pad - b), (0, 0))), mod_w, mod_b)[:, :b]
    shift, scale, gate_mod = (mods[:, :, k * dm:(k + 1) * dm].reshape(depth, b, 1, dm) for k in range(3))

    hyc = hy_bias.shape[1]
    ssw = ssd_norm_g.shape[1]
    xbw = ssd_conv_w.shape[2]
    o1, o2, o3, o4 = 3 * hyc, 4 * hyc, 4 * hyc + ssw, 4 * hyc + ssw + xbw

    for i in range(depth):
        j = i // 2
        g = norm_g[i][None, :]
        if i % 2 == 0:
            w_in = ev_w_in[j].astype(BF16)
            w_dt = jnp.pad(w_in[:, o4:], ((0, 0), (0, LANES - (w_in.shape[1] - o4))))
            x0c, vf, hyg, z, xc, dt_raw = _in_projection(x, shift[i], scale[i], g, [
                dict(w=w_in[:, :o1], dtype=(BF16, F32), post="hyena", conv_w=hy_conv_w[j],
                     conv_b=hy_conv_b[j][None, :]),
                dict(w=w_in[:, o1:o2], dtype=BF16, post="silu"),
                dict(w=w_in[:, o2:o3], dtype=BF16, post="silu"),
                dict(w=w_in[:, o3:o4], dtype=BF16, post="conv_silu", conv_w=ssd_conv_w[j],
                     conv_b=ssd_conv_b[j][None, :]),
                dict(w=w_dt, dtype=F32)])
            hre, him = _hyena_spectrum(l, hyc, hy_fw1[j], hy_fb1[j], hy_fw2[j], hy_fb2[j], hy_fw3[j], hy_freq[j])
            ya = _hyena_branch(x0c, vf, hyg, hre, him, hy_bias[j])
            ys = _ssd_branch(xc, dt_raw, ssd_dt_bias[j], ssd_A_log[j], ssd_D[j])
            x = (_out_projection_even(x, ya, ys, z, gate_mod[i], ssd_norm_g[j][None, :], ev_w_out[j]),)
        else:
            w_in = od_w_in[j].astype(BF16)
            half = w_in.shape[1] // 2
            xc, gt = _in_projection(x, shift[i], scale[i], g, [
                dict(w=w_in[:, :half], dtype=F32, post="conv", conv_w=lru_conv_w[j], conv_b=lru_conv_b[j][None, :]),
                dict(w=w_in[:, half:], dtype=BF16, post="silu")])
            y = _rglru_mixer(xc, gt, lru_w_a[j], lru_w_x[j], lru_b_a[j], lru_b_x[j], lru_lam[j])
            if i == depth - 1:
                return _out_projection_odd(x[0], y, gate_mod[i], od_w_out[j], final_g[None, :], split_at=bp)
            x = (_out_projection_odd(x[0], y, gate_mod[i], od_w_out[j], final_g[None, :]),)
```
